```python
import jax, jax.numpy as jnp
from jax import lax
import numpy as np

D_MODEL = 4096
BATCH = 4
SEQ = 4096
DEPTH = 1

HEAD_DIM = 128
MLA_HEADS = 16
MLA_Q_RANK = 1024
MLA_KV_RANK = 512
MLA_NOPE = 128
MLA_ROPE = 64
MLA_V = 128
DIL_HEADS = 8
DIL_PATTERNS = ((128, 1), (512, 4), (2048, 16))
DIL_GROUPS = len(DIL_PATTERNS)
N_BRANCHES = 2
D_FF = -(-8 * D_MODEL // (3 * 256)) * 256
ROPE_THETA = 10000.0
NORM_EPS = 1e-6
Q_BLOCK = 128
N_MOD = 6

IN_Q = MLA_Q_RANK
IN_KV = MLA_KV_RANK
IN_KR = MLA_ROPE
IN_DIL = 3 * DIL_GROUPS * DIL_HEADS * HEAD_DIM
IN_GATE = N_BRANCHES * D_MODEL
IN_TOTAL = IN_Q + IN_KV + IN_KR + IN_DIL + IN_GATE
IN_SPLITS = (IN_Q, IN_Q + IN_KV, IN_Q + IN_KV + IN_KR, IN_Q + IN_KV + IN_KR + IN_DIL)

kernel_name = 'hybrid_mla_dilated_adaln_block'


def rmsnorm(x, g):
    xf = x.astype(jnp.float32)
    y = xf * lax.rsqrt(jnp.mean(xf * xf, axis=-1, keepdims=True) + NORM_EPS)
    return (y * g.astype(jnp.float32)).astype(x.dtype)


def modulate(h, shift, scale):
    return h * (1 + scale[:, None, :]) + shift[:, None, :]


def rope_tables(positions, dim):
    inv = ROPE_THETA ** (-jnp.arange(0, dim, 2, dtype=jnp.float32) / dim)
    ang = positions.astype(jnp.float32)[..., None] * inv
    return jnp.cos(ang), jnp.sin(ang)


def apply_rope(x, cos, sin):
    half = x.shape[-1] // 2
    shp = cos.shape[:2] + (1,) * (x.ndim - 3) + (half,)
    cos = cos.reshape(shp).astype(x.dtype)
    sin = sin.reshape(shp).astype(x.dtype)
    x1, x2 = x[..., :half], x[..., half:]
    return jnp.concatenate([x1 * cos - x2 * sin, x2 * cos + x1 * sin], axis=-1)


def mla_attention(q_nope, q_rope, k_nope, k_rope, v):
    B, S, H, _ = q_nope.shape
    scale = (MLA_NOPE + MLA_ROPE) ** -0.5

    def block(i):
        start = i * Q_BLOCK
        qn = lax.dynamic_slice_in_dim(q_nope, start, Q_BLOCK, axis=1)
        qr = lax.dynamic_slice_in_dim(q_rope, start, Q_BLOCK, axis=1)
        s = (jnp.einsum('bqhd,bkhd->bhqk', qn, k_nope)
             + jnp.einsum('bqhr,bkr->bhqk', qr, k_rope)).astype(jnp.float32) * scale
        p = jax.nn.softmax(s, axis=-1)
        return jnp.einsum('bhqk,bkhd->bqhd', p.astype(v.dtype), v)

    out = lax.map(block, jnp.arange(S // Q_BLOCK))
    return out.transpose(1, 0, 2, 3, 4).reshape(B, S, H * MLA_V)


def dilated_attention(q, k, v):
    B, S, G, H, dh = q.shape
    scale = dh ** -0.5

    def block(i):
        start = i * Q_BLOCK
        qpos = start + jnp.arange(Q_BLOCK)
        qb = lax.dynamic_slice_in_dim(q, start, Q_BLOCK, axis=1)
        ms, ls, nums = [], [], []
        for g, (w, d) in enumerate(DIL_PATTERNS):
            n_side = w // (2 * d)
            offs = jnp.arange(-n_side, n_side + 1) * d
            idx = qpos[:, None] + offs[None, :]
            valid = (idx >= 0) & (idx < S)
            idx = jnp.clip(idx, 0, S - 1)
            kg = jnp.take(k[:, :, g], idx, axis=1)
            vg = jnp.take(v[:, :, g], idx, axis=1)
            s = jnp.einsum('bqhd,bqkhd->bhqk', qb[:, :, g], kg).astype(jnp.float32) * scale
            s = jnp.where(valid[None, None], s, -jnp.inf)
            m = jnp.max(s, axis=-1)
            p = jnp.exp(s - m[..., None])
            ms.append(m)
            ls.append(jnp.sum(p, axis=-1))
            nums.append(jnp.einsum('bhqk,bqkhd->bhqd', p, vg.astype(jnp.float32)))
        m_all = jnp.stack(ms)
        wts = jnp.exp(m_all - jnp.max(m_all, axis=0))
        den = jnp.sum(wts * jnp.stack(ls), axis=0)
        num = jnp.sum(wts[..., None] * jnp.stack(nums), axis=0)
        return (num / den[..., None]).astype(q.dtype)

    out = lax.map(block, jnp.arange(S // Q_BLOCK))
    return out.transpose(1, 0, 3, 2, 4).reshape(B, S, H * dh)


def setup_inputs(seed: int = 0) -> dict:
    key = jax.random.key(seed)
    ks = jax.random.split(key, 24)

    def nrm(k, shape, fan_in):
        return jax.random.normal(k, shape, jnp.float32) * (fan_in ** -0.5)

    def gain(k, shape):
        return 1.0 + 0.02 * jax.random.normal(k, shape, jnp.float32)

    L = DEPTH
    return {
        'x': jax.random.normal(ks[0], (BATCH, SEQ, D_MODEL), jnp.float32),
        'c': jax.random.normal(ks[1], (BATCH, D_MODEL), jnp.float32),
        'positions': jnp.broadcast_to(jnp.arange(SEQ, dtype=jnp.int32), (BATCH, SEQ)),
        'w_ada': nrm(ks[2], (L, D_MODEL, N_MOD * D_MODEL), D_MODEL),
        'b_ada': 0.02 * jax.random.normal(ks[3], (L, N_MOD * D_MODEL), jnp.float32),
        'norm1_g': gain(ks[4], (L, D_MODEL)),
        'w_in': nrm(ks[5], (L, D_MODEL, IN_TOTAL), D_MODEL),
        'q_norm_g': gain(ks[6], (L, MLA_Q_RANK)),
        'w_uq': nrm(ks[7], (L, MLA_Q_RANK, MLA_HEADS * (MLA_NOPE + MLA_ROPE)), MLA_Q_RANK),
        'kv_norm_g': gain(ks[8], (L, MLA_KV_RANK)),
        'w_ukv': nrm(ks[9], (L, MLA_KV_RANK, MLA_HEADS * (MLA_NOPE + MLA_V)), MLA_KV_RANK),
        'w_proj_a': nrm(ks[10], (L, MLA_HEADS * MLA_V, D_MODEL), MLA_HEADS * MLA_V),
        'w_proj_b': nrm(ks[11], (L, DIL_HEADS * HEAD_DIM, D_MODEL), DIL_HEADS * HEAD_DIM),
        'w_out': nrm(ks[12], (L, D_MODEL, D_MODEL), D_MODEL),
        'norm2_g': gain(ks[13], (L, D_MODEL)),
        'w_gate': nrm(ks[14], (L, D_MODEL, D_FF), D_MODEL),
        'w_up': nrm(ks[15], (L, D_MODEL, D_FF), D_MODEL),
        'w_down': nrm(ks[16], (L, D_FF, D_MODEL), D_FF),
        'final_g': gain(ks[17], (D_MODEL,)),
    }


def reference(x, c, positions, w_ada, b_ada, norm1_g, w_in, q_norm_g, w_uq, kv_norm_g,
              w_ukv, w_proj_a, w_proj_b, w_out, norm2_g, w_gate, w_up, w_down, final_g):
    B, S, D = x.shape
    cos_f, sin_f = rope_tables(positions, HEAD_DIM)
    cos_r, sin_r = rope_tables(positions, MLA_ROPE)
    c_act = jax.nn.silu(c)

    for l in range(DEPTH):
        mod = c_act @ w_ada[l] + b_ada[l]
        sh1, sc1, g1, sh2, sc2, g2 = jnp.split(mod, N_MOD, axis=-1)

        h = modulate(rmsnorm(x, norm1_g[l]), sh1, sc1)
        z = h @ w_in[l]
        zq, zkv, zkr, zdil, zg = jnp.split(z, IN_SPLITS, axis=-1)

        q = (rmsnorm(zq, q_norm_g[l]) @ w_uq[l]).reshape(B, S, MLA_HEADS, MLA_NOPE + MLA_ROPE)
        q_nope = q[..., :MLA_NOPE]
        q_rope = apply_rope(q[..., MLA_NOPE:], cos_r, sin_r)
        kv = (rmsnorm(zkv, kv_norm_g[l]) @ w_ukv[l]).reshape(B, S, MLA_HEADS, MLA_NOPE + MLA_V)
        k_nope, v_a = kv[..., :MLA_NOPE], kv[..., MLA_NOPE:]
        k_rope = apply_rope(zkr[:, :, None, :], cos_r, sin_r)[:, :, 0, :]
        y_a = mla_attention(q_nope, q_rope, k_nope, k_rope, v_a)

        zd = zdil.reshape(B, S, 3, DIL_GROUPS, DIL_HEADS, HEAD_DIM)
        q_d = apply_rope(zd[:, :, 0], cos_f, sin_f)
        k_d = apply_rope(zd[:, :, 1], cos_f, sin_f)
        y_b = dilated_attention(q_d, k_d, zd[:, :, 2])

        gate_a, gate_b = jnp.split(jax.nn.sigmoid(zg), N_BRANCHES, axis=-1)
        mixed = gate_a * (y_a @ w_proj_a[l]) + gate_b * (y_b @ w_proj_b[l])
        x = x + g1[:, None, :] * (mixed @ w_out[l])

        h2 = modulate(rmsnorm(x, norm2_g[l]), sh2, sc2)
        ff = (jax.nn.silu(h2 @ w_gate[l]) * (h2 @ w_up[l])) @ w_down[l]
        x = x + g2[:, None, :] * ff

    return rmsnorm(x, final_g)
```

```python
import functools
import math

import jax
import jax.numpy as jnp
from jax import lax
from jax.experimental import pallas as pl
from jax.experimental.pallas import tpu as pltpu

F32 = jnp.float32
BF16 = jnp.bfloat16

NORM_EPS = 1e-6
ROPE_THETA = 10000.0
HEAD_DIM = 128
MLA_HEADS = 16
MLA_Q_RANK = 1024
MLA_KV_RANK = 512
MLA_NOPE = 128
MLA_ROPE = 64
MLA_V = 128
DIL_HEADS = 8
DIL_PATTERNS = ((128, 1), (512, 4), (2048, 16))
DIL_GROUPS = len(DIL_PATTERNS)
N_MOD = 6
LOG2E = 1.4426950408889634

LANE = 128
MLA_QK_PAD = 256
DIL_COLS = DIL_HEADS * HEAD_DIM
VMEM_LIMIT = 56 * 1024 * 1024

MLA_QSCALE = (MLA_NOPE + MLA_ROPE) ** -0.5 * LOG2E
DIL_QSCALE = HEAD_DIM ** -0.5 * LOG2E


def _cparams(*sem):
    return pltpu.CompilerParams(dimension_semantics=sem, vmem_limit_bytes=VMEM_LIMIT)


def _tile(n, pref):
    return pref if n % pref == 0 else n


def _rms(x, g):
    return x * lax.rsqrt(jnp.mean(x * x, axis=-1, keepdims=True) + NORM_EPS) * g


def _ada_kernel(c_ref, w_ref, b_ref, o_ref):
    c = c_ref[...]
    ca = (c * jax.nn.sigmoid(c)).astype(BF16)
    o_ref[...] = jnp.dot(ca, w_ref[...].astype(BF16), preferred_element_type=F32) + b_ref[...]


def _ada(c, w, b):
    bsz, d = c.shape
    n = w.shape[1]
    rows = 8
    cp = jnp.zeros((rows, d), F32).at[:bsz].set(c)
    tn = _tile(n, 512)
    out = pl.pallas_call(
        _ada_kernel,
        grid=(n // tn,),
        in_specs=[pl.BlockSpec((rows, d), lambda j: (0, 0)),
                  pl.BlockSpec((d, tn), lambda j: (0, j)),
                  pl.BlockSpec((1, tn), lambda j: (0, j))],
        out_specs=pl.BlockSpec((rows, tn), lambda j: (0, j)),
        out_shape=jax.ShapeDtypeStruct((rows, n), F32),
        compiler_params=_cparams("parallel"),
        name="ada",
    )(cp, w, b.reshape(1, n))
    return out[:bsz]


def _rope_tab_kernel(pos_ref, inv_ref, sgn_ref, msk_ref, cos_ref, sin_ref):
    ang = pos_ref[...].astype(F32) * inv_ref[...]
    cos_ref[...] = jnp.cos(ang) * msk_ref[...]
    sin_ref[...] = jnp.sin(ang) * sgn_ref[...]


def _rope_tables(positions):
    m = positions.size
    inv_f = ROPE_THETA ** (-jnp.arange(0, HEAD_DIM, 2, dtype=F32) / HEAD_DIM)
    inv_r = ROPE_THETA ** (-jnp.arange(0, MLA_ROPE, 2, dtype=F32) / MLA_ROPE)
    zpad = jnp.zeros((LANE - MLA_ROPE,), F32)
    hf, hr = HEAD_DIM // 2, MLA_ROPE // 2
    inv = jnp.concatenate([inv_f, inv_f, inv_r, inv_r, zpad]).reshape(1, 2 * LANE)
    sgn = jnp.concatenate([-jnp.ones(hf), jnp.ones(hf), -jnp.ones(hr), jnp.ones(hr), zpad]).astype(F32).reshape(1, 2 * LANE)
    msk = jnp.concatenate([jnp.ones(HEAD_DIM + MLA_ROPE), zpad]).astype(F32).reshape(1, 2 * LANE)
    tm = _tile(m, 1024)
    row = pl.BlockSpec((1, 2 * LANE), lambda i: (0, 0))
    tab = pl.BlockSpec((tm, 2 * LANE), lambda i: (i, 0))
    return pl.pallas_call(
        _rope_tab_kernel,
        grid=(m // tm,),
        in_specs=[pl.BlockSpec((tm, 1), lambda i: (i, 0)), row, row, row],
        out_specs=[tab, tab],
        out_shape=[jax.ShapeDtypeStruct((m, 2 * LANE), F32)] * 2,
        compiler_params=_cparams("parallel"),
        name="rope_tables",
    )(positions.reshape(m, 1), inv, sgn, msk)


def _prenorm_kernel(x_ref, g_ref, sh_ref, sc_ref, o_ref):
    y = _rms(x_ref[...], g_ref[...])
    o_ref[...] = (y * (1.0 + sc_ref[...]) + sh_ref[...]).astype(o_ref.dtype)


def _prenorm(x2d, g, shift, scale, seq):
    m, d = x2d.shape
    bsz = m // seq
    tm = _tile(seq, 256)
    per_b = seq // tm
    vec = pl.BlockSpec((None, 1, d), lambda i: (i // per_b, 0, 0))
    return pl.pallas_call(
        _prenorm_kernel,
        grid=(m // tm,),
        in_specs=[pl.BlockSpec((tm, d), lambda i: (i, 0)),
                  pl.BlockSpec((1, d), lambda i: (0, 0)), vec, vec],
        out_specs=pl.BlockSpec((tm, d), lambda i: (i, 0)),
        out_shape=jax.ShapeDtypeStruct((m, d), BF16),
        compiler_params=_cparams("parallel"),
        name="prenorm",
    )(x2d, g.reshape(1, d), shift.reshape(bsz, 1, d), scale.reshape(bsz, 1, d))


def _mm_kernel(a_ref, w_ref, o_ref):
    o_ref[...] = jnp.dot(a_ref[...], w_ref[...], preferred_element_type=F32).astype(o_ref.dtype)


def _in_small(h, w_s):
    m, d = h.shape
    n = w_s.shape[1]
    tm = _tile(m, 512)
    return pl.pallas_call(
        _mm_kernel,
        grid=(m // tm,),
        in_specs=[pl.BlockSpec((tm, d), lambda i: (i, 0)),
                  pl.BlockSpec((d, n), lambda i: (0, 0))],
        out_specs=pl.BlockSpec((tm, n), lambda i: (i, 0)),
        out_shape=jax.ShapeDtypeStruct((m, n), F32),
        compiler_params=_cparams("parallel"),
        name="in_small",
    )(h, w_s)


def _rope128(x, c, s):
    return x * c + pltpu.roll(x, HEAD_DIM // 2, 1) * s


def _rope_mla(x, c, s):
    half = MLA_ROPE // 2
    return x * c + (pltpu.roll(x, LANE - half, 1) + pltpu.roll(x, half, 1)) * s


def _in_big_kernel(h_ref, w_ref, cos_ref, sin_ref, o_ref, *, n_dil_tiles):
    j = pl.program_id(1)
    acc = jnp.dot(h_ref[...], w_ref[...], preferred_element_type=F32)
    part = j % 3
    is_dil = j < n_dil_tiles

    @pl.when(jnp.logical_and(is_dil, part < 2))
    def _():
        c = cos_ref[...]
        s = sin_ref[...]
        scale = jnp.where(part == 0, DIL_QSCALE, 1.0).astype(F32)
        for hh in range(DIL_HEADS):
            sl = slice(hh * HEAD_DIM, (hh + 1) * HEAD_DIM)
            o_ref[:, sl] = (_rope128(acc[:, sl], c, s) * scale).astype(o_ref.dtype)

    @pl.when(jnp.logical_and(is_dil, part == 2))
    def _():
        o_ref[...] = acc.astype(o_ref.dtype)

    @pl.when(jnp.logical_not(is_dil))
    def _():
        o_ref[...] = jax.nn.sigmoid(acc).astype(o_ref.dtype)


def _in_big(h, w_dg, cos_t, sin_t):
    m, d = h.shape
    n = w_dg.shape[1]
    tm = _tile(m, 1024)
    tn = DIL_COLS
    n_dil_tiles = 3 * DIL_GROUPS
    return pl.pallas_call(
        functools.partial(_in_big_kernel, n_dil_tiles=n_dil_tiles),
        grid=(m // tm, n // tn),
        in_specs=[pl.BlockSpec((tm, d), lambda i, j: (i, 0)),
                  pl.BlockSpec((d, tn), lambda i, j: (0, j)),
                  pl.BlockSpec((tm, LANE), lambda i, j: (i, 0)),
                  pl.BlockSpec((tm, LANE), lambda i, j: (i, 0))],
        out_specs=pl.BlockSpec((tm, tn), lambda i, j: (i, j)),
        out_shape=jax.ShapeDtypeStruct((m, n), BF16),
        compiler_params=_cparams("parallel", "arbitrary"),
        name="in_big",
    )(h, w_dg, cos_t, sin_t)


def _qup_kernel(z_ref, g_ref, w_ref, cos_ref, sin_ref, o_ref, zn_ref):
    @pl.when(pl.program_id(1) == 0)
    def _():
        zn_ref[...] = _rms(z_ref[...], g_ref[...]).astype(BF16)

    acc = jnp.dot(zn_ref[...], w_ref[...], preferred_element_type=F32)
    c = cos_ref[...]
    s = sin_ref[...]
    for hh in range(acc.shape[1] // MLA_QK_PAD):
        lo = hh * MLA_QK_PAD
        o_ref[:, lo:lo + LANE] = (acc[:, lo:lo + LANE] * MLA_QSCALE).astype(o_ref.dtype)
        rp = _rope_mla(acc[:, lo + LANE:lo + 2 * LANE], c, s)
        o_ref[:, lo + LANE:lo + 2 * LANE] = (rp * MLA_QSCALE).astype(o_ref.dtype)


def _qup(z_s, g, w_uq_pad, cos_t, sin_t):
    m = z_s.shape[0]
    k = MLA_Q_RANK
    n = w_uq_pad.shape[1]
    tm = _tile(m, 1024)
    tn = _tile(n, 1024)
    return pl.pallas_call(
        _qup_kernel,
        grid=(m // tm, n // tn),
        in_specs=[pl.BlockSpec((tm, k), lambda i, j: (i, 0)),
                  pl.BlockSpec((1, k), lambda i, j: (0, 0)),
                  pl.BlockSpec((k, tn), lambda i, j: (0, j)),
                  pl.BlockSpec((tm, LANE), lambda i, j: (i, 1)),
                  pl.BlockSpec((tm, LANE), lambda i, j: (i, 1))],
        out_specs=pl.BlockSpec((tm, tn), lambda i, j: (i, j)),
        out_shape=jax.ShapeDtypeStruct((m, n), BF16),
        scratch_shapes=[pltpu.VMEM((tm, k), BF16)],
        compiler_params=_cparams("parallel", "arbitrary"),
        name="q_up",
    )(z_s, g.reshape(1, k), w_uq_pad, cos_t, sin_t)


def _kvup_kernel(z_ref, kr_ref, g_ref, w_ref, cos_ref, sin_ref, k_ref, v_ref, zn_ref):
    @pl.when(pl.program_id(1) == 0)
    def _():
        zn_ref[...] = _rms(z_ref[...], g_ref[...]).astype(BF16)

    acc = jnp.dot(zn_ref[...], w_ref[...], preferred_element_type=F32)
    kr = _rope_mla(kr_ref[...], cos_ref[...], sin_ref[...]).astype(k_ref.dtype)
    for hh in range(acc.shape[1] // (MLA_NOPE + MLA_V)):
        lo = hh * (MLA_NOPE + MLA_V)
        ko = hh * MLA_QK_PAD
        k_ref[:, ko:ko + LANE] = acc[:, lo:lo + MLA_NOPE].astype(k_ref.dtype)
        k_ref[:, ko + LANE:ko + 2 * LANE] = kr
        v_ref[:, hh * MLA_V:(hh + 1) * MLA_V] = acc[:, lo + MLA_NOPE:lo + MLA_NOPE + MLA_V].astype(v_ref.dtype)


def _kvup(z_s, g, w_ukv, cos_t, sin_t):
    m = z_s.shape[0]
    k = MLA_KV_RANK
    n = w_ukv.shape[1]
    tm = _tile(m, 1024)
    tn = _tile(n, 1024)
    nh = tn // (MLA_NOPE + MLA_V)
    kv_blk = MLA_Q_RANK // MLA_KV_RANK
    kr_blk = (MLA_Q_RANK + MLA_KV_RANK) // LANE
    return pl.pallas_call(
        _kvup_kernel,
        grid=(m // tm, n // tn),
        in_specs=[pl.BlockSpec((tm, k), lambda i, j: (i, kv_blk)),
                  pl.BlockSpec((tm, LANE), lambda i, j: (i, kr_blk)),
                  pl.BlockSpec((1, k), lambda i, j: (0, 0)),
                  pl.BlockSpec((k, tn), lambda i, j: (0, j)),
                  pl.BlockSpec((tm, LANE), lambda i, j: (i, 1)),
                  pl.BlockSpec((tm, LANE), lambda i, j: (i, 1))],
        out_specs=[pl.BlockSpec((tm, nh * MLA_QK_PAD), lambda i, j: (i, j)),
                   pl.BlockSpec((tm, nh * MLA_V), lambda i, j: (i, j))],
        out_shape=[jax.ShapeDtypeStruct((m, MLA_HEADS * MLA_QK_PAD), BF16),
                   jax.ShapeDtypeStruct((m, MLA_HEADS * MLA_V), BF16)],
        scratch_shapes=[pltpu.VMEM((tm, k), BF16)],
        compiler_params=_cparams("parallel", "arbitrary"),
        name="kv_up",
    )(z_s, z_s, g.reshape(1, k), w_ukv, cos_t, sin_t)


def _mla_kernel(q_ref, k_ref, v_ref, o_ref):
    s = lax.dot_general(q_ref[...], k_ref[...], (((1,), (1,)), ((), ())),
                        preferred_element_type=F32)
    m = jnp.max(s, axis=-1, keepdims=True)
    p = jnp.exp2(s - m)
    l = jnp.sum(p, axis=-1, keepdims=True)
    o = jnp.dot(p.astype(BF16), v_ref[...], preferred_element_type=F32)
    o_ref[...] = (o / l).astype(o_ref.dtype)


def _mla_attention(q, k, v, bsz, seq):
    q3 = q.reshape(bsz, seq, MLA_HEADS * MLA_QK_PAD)
    k3 = k.reshape(bsz, seq, MLA_HEADS * MLA_QK_PAD)
    v3 = v.reshape(bsz, seq, MLA_HEADS * MLA_V)
    tq = _tile(seq, 256)
    out = pl.pallas_call(
        _mla_kernel,
        grid=(bsz, MLA_HEADS, seq // tq),
        in_specs=[pl.BlockSpec((None, tq, MLA_QK_PAD), lambda b, h, i: (b, i, h)),
                  pl.BlockSpec((None, seq, MLA_QK_PAD), lambda b, h, i: (b, 0, h)),
                  pl.BlockSpec((None, seq, MLA_V), lambda b, h, i: (b, 0, h))],
        out_specs=pl.BlockSpec((None, tq, MLA_V), lambda b, h, i: (b, i, h)),
        out_shape=jax.ShapeDtypeStruct((bsz, seq, MLA_HEADS * MLA_V), BF16),
        compiler_params=_cparams("parallel", "parallel", "arbitrary"),
        name="mla_attn",
    )(q3, k3, v3)
    return out.reshape(bsz * seq, MLA_HEADS * MLA_V)


def _dil_kernel(q_ref, k_ref, v_ref, o_ref, st_ref, *, t_len, tq, kw, n_side, hb):
    lane = lax.broadcasted_iota(jnp.int32, (tq, LANE), 1)
    row = lax.broadcasted_iota(jnp.int32, (tq, kw), 0)
    col = lax.broadcasted_iota(jnp.int32, (tq, kw), 1)

    def tile(qi, carry):
        t0 = pl.multiple_of(qi * tq, tq)
        ws = jnp.clip(t0 - n_side, 0, t_len - kw)
        ws = pl.multiple_of(ws, n_side) if t_len > kw else 0
        valid = jnp.abs((t0 + row) - (ws + col)) <= n_side
        stats = jnp.zeros((tq, LANE), F32)
        for hh in range(hb):
            sl = slice(hh * HEAD_DIM, (hh + 1) * HEAD_DIM)
            q = q_ref[pl.ds(t0, tq), sl]
            k = k_ref[pl.ds(ws, kw), sl]
            v = v_ref[pl.ds(ws, kw), sl]
            s = lax.dot_general(q, k, (((1,), (1,)), ((), ())), preferred_element_type=F32)
            s = jnp.where(valid, s, -jnp.inf)
            m = jnp.max(s, axis=-1, keepdims=True)
            p = jnp.exp2(s - m)
            l = jnp.sum(p, axis=-1, keepdims=True)
            o = jnp.dot(p.astype(BF16), v, preferred_element_type=F32)
            o_ref[pl.ds(t0, tq), sl] = (o / l).astype(o_ref.dtype)
            stats = jnp.where(lane == hh, m, stats)
            stats = jnp.where(lane == hb + hh, l, stats)
        st_ref[pl.ds(t0, tq), :] = stats
        return carry

    lax.fori_loop(0, t_len // tq, tile, 0)


def _dil_heads_per_block(t_len):
    return DIL_HEADS if t_len <= 1024 else DIL_HEADS // 2


def _dilated_group(zd, bsz, seq, dil, window, col0, ncols):
    t_len = seq // dil
    n_side = window // (2 * dil)
    hb = _dil_heads_per_block(t_len)
    nhb = DIL_HEADS // hb
    cw = hb * HEAD_DIM
    tq = min(2 * n_side, t_len)
    kw = min(4 * n_side, t_len)
    if dil == 1:
        zv = zd.reshape(bsz, seq, ncols)
        width = ncols
    else:
        zv = zd[:, col0:col0 + 3 * DIL_COLS].reshape(bsz, t_len, dil * 3 * DIL_COLS)
        width = 3 * DIL_COLS
        col0 = 0
    per_tok = width // cw
    base = col0 // cw
    slab = DIL_COLS // cw

    def in_spec(part):
        return pl.BlockSpec((None, t_len, cw),
                            lambda b, r, hbi: (b, 0, r * per_tok + base + part * slab + hbi))

    o, st = pl.pallas_call(
        functools.partial(_dil_kernel, t_len=t_len, tq=tq, kw=kw, n_side=n_side, hb=hb),
        grid=(bsz, dil, nhb),
        in_specs=[in_spec(0), in_spec(1), in_spec(2)],
        out_specs=[pl.BlockSpec((None, t_len, cw), lambda b, r, hbi: (b, 0, r * nhb + hbi)),
                   pl.BlockSpec((None, t_len, LANE), lambda b, r, hbi: (b, 0, r * nhb + hbi))],
        out_shape=[jax.ShapeDtypeStruct((bsz, t_len, dil * DIL_COLS), BF16),
                   jax.ShapeDtypeStruct((bsz, t_len, dil * nhb * LANE), F32)],
        compiler_params=_cparams("parallel", "parallel", "parallel"),
        name=f"dilated_d{dil}",
    )(zv, zv, zv)
    return o.reshape(bsz * seq, DIL_COLS), st.reshape(bsz * seq, nhb * LANE), hb


def _merge_kernel(*refs, hbs):
    ng = len(hbs)
    o_refs, st_refs, y_ref = refs[:ng], refs[ng:2 * ng], refs[2 * ng]
    sts = [r[...] for r in st_refs]
    for h in range(DIL_HEADS):
        ms, ls = [], []
        for g in range(ng):
            hb = hbs[g]
            lo = (h // hb) * LANE + h % hb
            ms.append(sts[g][:, lo:lo + 1])
            ls.append(sts[g][:, lo + hb:lo + hb + 1])
        m_all = functools.reduce(jnp.maximum, ms)
        a = [jnp.exp2(ms[g] - m_all) * ls[g] for g in range(ng)]
        den = functools.reduce(lambda u, w: u + w, a)
        sl = slice(h * HEAD_DIM, (h + 1) * HEAD_DIM)
        y = None
        for g in range(ng):
            term = (a[g] / den) * o_refs[g][:, sl].astype(F32)
            y = term if y is None else y + term
        y_ref[:, sl] = y.astype(y_ref.dtype)


def _merge_groups(os_, sts, hbs):
    m = os_[0].shape[0]
    tm = _tile(m, 512)
    in_specs = [pl.BlockSpec((tm, DIL_COLS), lambda i: (i, 0)) for _ in os_]
    in_specs += [pl.BlockSpec((tm, s.shape[1]), lambda i: (i, 0)) for s in sts]
    return pl.pallas_call(
        functools.partial(_merge_kernel, hbs=tuple(hbs)),
        grid=(m // tm,),
        in_specs=in_specs,
        out_specs=pl.BlockSpec((tm, DIL_COLS), lambda i: (i, 0)),
        out_shape=jax.ShapeDtypeStruct((m, DIL_COLS), BF16),
        compiler_params=_cparams("parallel"),
        name="dil_merge",
    )(*os_, *sts)


def _mix_kernel(ya_ref, yb_ref, wa_ref, wb_ref, ga_ref, gb_ref, o_ref):
    pa = jnp.dot(ya_ref[...], wa_ref[...], preferred_element_type=F32)
    pb = jnp.dot(yb_ref[...], wb_ref[...], preferred_element_type=F32)
    o_ref[...] = (ga_ref[...].astype(F32) * pa + gb_ref[...].astype(F32) * pb).astype(o_ref.dtype)


def _mix(ya, yb, w_pa, w_pb, zdg, gate_col0):
    m = ya.shape[0]
    d = w_pa.shape[1]
    tm = _tile(m, 1024)
    tn = _tile(d, 1024)
    ga0 = gate_col0 // tn
    gb0 = (gate_col0 + d) // tn
    return pl.pallas_call(
        _mix_kernel,
        grid=(m // tm, d // tn),
        in_specs=[pl.BlockSpec((tm, ya.shape[1]), lambda i, j: (i, 0)),
                  pl.BlockSpec((tm, yb.shape[1]), lambda i, j: (i, 0)),
                  pl.BlockSpec((w_pa.shape[0], tn), lambda i, j: (0, j)),
                  pl.BlockSpec((w_pb.shape[0], tn), lambda i, j: (0, j)),
                  pl.BlockSpec((tm, tn), lambda i, j: (i, ga0 + j)),
                  pl.BlockSpec((tm, tn), lambda i, j: (i, gb0 + j))],
        out_specs=pl.BlockSpec((tm, tn), lambda i, j: (i, j)),
        out_shape=jax.ShapeDtypeStruct((m, d), BF16),
        compiler_params=_cparams("parallel", "arbitrary"),
        name="mix",
    )(ya, yb, w_pa, w_pb, zdg, zdg)


def _outproj_kernel(a_ref, w_ref, x_ref, g_ref, o_ref):
    acc = jnp.dot(a_ref[...], w_ref[...], preferred_element_type=F32)
    o_ref[...] = x_ref[...] + g_ref[...] * acc


def _outproj(a, w, x2d, gate, seq):
    m, k = a.shape
    d = w.shape[1]
    bsz = m // seq
    tm = _tile(seq, 1024)
    tn = _tile(d, 512)
    per_b = seq // tm
    return pl.pallas_call(
        _outproj_kernel,
        grid=(m // tm, d // tn),
        in_specs=[pl.BlockSpec((tm, k), lambda i, j: (i, 0)),
                  pl.BlockSpec((k, tn), lambda i, j: (0, j)),
                  pl.BlockSpec((tm, tn), lambda i, j: (i, j)),
                  pl.BlockSpec((None, 1, tn), lambda i, j: (i // per_b, 0, j))],
        out_specs=pl.BlockSpec((tm, tn), lambda i, j: (i, j)),
        out_shape=jax.ShapeDtypeStruct((m, d), F32),
        compiler_params=_cparams("parallel", "arbitrary"),
        name="out_proj",
    )(a, w, x2d, gate.reshape(bsz, 1, d))


def _ffn_up_kernel(h_ref, wg_ref, wu_ref, o_ref):
    h = h_ref[...]
    g = jnp.dot(h, wg_ref[...], preferred_element_type=F32)
    u = jnp.dot(h, wu_ref[...], preferred_element_type=F32)
    o_ref[...] = (g * jax.nn.sigmoid(g) * u).astype(o_ref.dtype)


def _ffn_up(h, wg, wu):
    m, d = h.shape
    f = wg.shape[1]
    tm = _tile(m, 1024)
    tn = _tile(f, 512)
    return pl.pallas_call(
        _ffn_up_kernel,
        grid=(m // tm, f // tn),
        in_specs=[pl.BlockSpec((tm, d), lambda i, j: (i, 0)),
                  pl.BlockSpec((d, tn), lambda i, j: (0, j)),
                  pl.BlockSpec((d, tn), lambda i, j: (0, j))],
        out_specs=pl.BlockSpec((tm, tn), lambda i, j: (i, j)),
        out_shape=jax.ShapeDtypeStruct((m, f), BF16),
        compiler_params=_cparams("parallel", "arbitrary"),
        name="ffn_up",
    )(h, wg, wu)


def _ffn_down_kernel(a_ref, w_ref, x_ref, g_ref, fg_ref, o_ref, *, nk, final_norm):
    k = pl.program_id(1)
    part = jnp.dot(a_ref[...], w_ref[...], preferred_element_type=F32)

    @pl.when(k == 0)
    def _():
        o_ref[...] = part

    @pl.when(k > 0)
    def _():
        o_ref[...] += part

    @pl.when(k == nk - 1)
    def _():
        x2 = x_ref[...] + g_ref[...] * o_ref[...]
        o_ref[...] = _rms(x2, fg_ref[...]) if final_norm else x2


def _ffn_down(a, w, x2d, gate, final_g, seq, final_norm):
    m, f = a.shape
    d = w.shape[1]
    bsz = m // seq
    tm = _tile(seq, 512)
    tk = _tile(f, 512)
    nk = f // tk
    per_b = seq // tm
    return pl.pallas_call(
        functools.partial(_ffn_down_kernel, nk=nk, final_norm=final_norm),
        grid=(m // tm, nk),
        in_specs=[pl.BlockSpec((tm, tk), lambda i, k: (i, k)),
                  pl.BlockSpec((tk, d), lambda i, k: (k, 0)),
                  pl.BlockSpec((tm, d), lambda i, k: (i, 0), pipeline_mode=pl.Buffered(1)),
                  pl.BlockSpec((None, 1, d), lambda i, k: (i // per_b, 0, 0)),
                  pl.BlockSpec((1, d), lambda i, k: (0, 0))],
        out_specs=pl.BlockSpec((tm, d), lambda i, k: (i, 0)),
        out_shape=jax.ShapeDtypeStruct((m, d), F32),
        compiler_params=_cparams("parallel", "arbitrary"),
        name="ffn_down",
    )(a, w, x2d, gate.reshape(bsz, 1, d), final_g.reshape(1, d))


def _pad_cols(w, mult):
    n = w.shape[1]
    pad = (-n) % mult
    return jnp.pad(w, ((0, 0), (0, pad))) if pad else w


def _prep_in_weights(w_in, d):
    n_small = MLA_Q_RANK + MLA_KV_RANK + MLA_ROPE
    n_dil = 3 * DIL_GROUPS * DIL_COLS
    w_s = _pad_cols(w_in[:, :n_small], LANE).astype(BF16)
    w_dil = w_in[:, n_small:n_small + n_dil].reshape(d, 3, DIL_GROUPS, DIL_COLS)
    w_dil = w_dil.transpose(0, 2, 1, 3).reshape(d, n_dil)
    w_dg = jnp.concatenate([w_dil, w_in[:, n_small + n_dil:]], axis=1).astype(BF16)
    return w_s, w_dg, n_dil


def kernel(x, c, positions, w_ada, b_ada, norm1_g, w_in, q_norm_g, w_uq, kv_norm_g, w_ukv,
           w_proj_a, w_proj_b, w_out, norm2_g, w_gate, w_up, w_down, final_g):
    bsz, seq, d = x.shape
    depth = w_ada.shape[0]
    m = bsz * seq
    cos_t, sin_t = _rope_tables(positions)
    x2d = x.reshape(m, d)

    for l in range(depth):
        mod = _ada(c, w_ada[l], b_ada[l])
        sh1, sc1, g1, sh2, sc2, g2 = jnp.split(mod, N_MOD, axis=-1)

        w_s, w_dg, n_dil = _prep_in_weights(w_in[l], d)
        h = _prenorm(x2d, norm1_g[l], sh1, sc1, seq)
        z_s = _in_small(h, w_s)
        zdg = _in_big(h, w_dg, cos_t, sin_t)

        w_uq_pad = jnp.pad(w_uq[l].reshape(MLA_Q_RANK, MLA_HEADS, MLA_NOPE + MLA_ROPE),
                           ((0, 0), (0, 0), (0, MLA_QK_PAD - MLA_NOPE - MLA_ROPE)))
        w_uq_pad = w_uq_pad.reshape(MLA_Q_RANK, MLA_HEADS * MLA_QK_PAD).astype(BF16)
        q = _qup(z_s, q_norm_g[l], w_uq_pad, cos_t, sin_t)
        k, v = _kvup(z_s, kv_norm_g[l], w_ukv[l].astype(BF16), cos_t, sin_t)
        y_a = _mla_attention(q, k, v, bsz, seq)

        os_, sts, hbs = [], [], []
        for g, (window, dil) in enumerate(DIL_PATTERNS):
            o_g, st_g, hb = _dilated_group(zdg, bsz, seq, dil, window, g * 3 * DIL_COLS, zdg.shape[1])
            os_.append(o_g)
            sts.append(st_g)
            hbs.append(hb)
        y_b = _merge_groups(os_, sts, hbs)

        mixed = _mix(y_a, y_b, w_proj_a[l].astype(BF16), w_proj_b[l].astype(BF16), zdg, n_dil)
        x2d = _outproj(mixed, w_out[l].astype(BF16), x2d, g1, seq)

        h2 = _prenorm(x2d, norm2_g[l], sh2, sc2, seq)
        ff_mult = 512
        wg = _pad_cols(w_gate[l], ff_mult).astype(BF16)
        wu = _pad_cols(w_up[l], ff_mult).astype(BF16)
        wd = jnp.pad(w_down[l], ((0, wg.shape[1] - w_down.shape[1]), (0, 0))).astype(BF16)
        hmid = _ffn_up(h2, wg, wu)
        x2d = _ffn_down(hmid, wd, x2d, g2, final_g, seq, final_norm=(l == depth - 1))

    return x2d.reshape(bsz, seq, d)
```

```python
import functools
import math

import jax
import jax.numpy as jnp
from jax import lax
from jax.experimental import pallas as pl
from jax.experimental.pallas import tpu as pltpu

F32 = jnp.float32
BF16 = jnp.bfloat16

NORM_EPS = 1e-6
ROPE_THETA = 10000.0
HEAD_DIM = 128
MLA_HEADS = 16
MLA_Q_RANK = 1024
MLA_KV_RANK = 512
MLA_NOPE = 128
MLA_ROPE = 64
MLA_V = 128
DIL_HEADS = 8
DIL_PATTERNS = ((128, 1), (512, 4), (2048, 16))
DIL_GROUPS = len(DIL_PATTERNS)
N_MOD = 6
LOG2E = 1.4426950408889634

LANE = 128
MLA_QK_PAD = 256
DIL_COLS = DIL_HEADS * HEAD_DIM
VMEM_LIMIT = 56 * 1024 * 1024

MLA_QSCALE = (MLA_NOPE + MLA_ROPE) ** -0.5 * LOG2E
DIL_QSCALE = HEAD_DIM ** -0.5 * LOG2E


def _cparams(*sem):
    return pltpu.CompilerParams(dimension_semantics=sem, vmem_limit_bytes=VMEM_LIMIT)


def _tile(n, pref):
    return pref if n % pref == 0 else n


def _rms(x, g):
    return x * lax.rsqrt(jnp.mean(x * x, axis=-1, keepdims=True) + NORM_EPS) * g


def _ada_kernel(c_ref, w_ref, b_ref, o_ref):
    c = c_ref[...]
    ca = (c * jax.nn.sigmoid(c)).astype(BF16)
    o_ref[...] = jnp.dot(ca, w_ref[...].astype(BF16), preferred_element_type=F32) + b_ref[...]


def _ada(c, w, b):
    bsz, d = c.shape
    n = w.shape[1]
    rows = 8
    cp = jnp.zeros((rows, d), F32).at[:bsz].set(c)
    tn = _tile(n, 512)
    out = pl.pallas_call(
        _ada_kernel,
        grid=(n // tn,),
        in_specs=[pl.BlockSpec((rows, d), lambda j: (0, 0)),
                  pl.BlockSpec((d, tn), lambda j: (0, j)),
                  pl.BlockSpec((1, tn), lambda j: (0, j))],
        out_specs=pl.BlockSpec((rows, tn), lambda j: (0, j)),
        out_shape=jax.ShapeDtypeStruct((rows, n), F32),
        compiler_params=_cparams("parallel"),
        name="ada",
    )(cp, w, b.reshape(1, n))
    return out[:bsz]


def _rope_tab_kernel(pos_ref, inv_ref, sgn_ref, msk_ref, cos_ref, sin_ref):
    ang = pos_ref[...].astype(F32) * inv_ref[...]
    cos_ref[...] = jnp.cos(ang) * msk_ref[...]
    sin_ref[...] = jnp.sin(ang) * sgn_ref[...]


def _rope_tables(positions):
    m = positions.size
    inv_f = ROPE_THETA ** (-jnp.arange(0, HEAD_DIM, 2, dtype=F32) / HEAD_DIM)
    inv_r = ROPE_THETA ** (-jnp.arange(0, MLA_ROPE, 2, dtype=F32) / MLA_ROPE)
    zpad = jnp.zeros((LANE - MLA_ROPE,), F32)
    hf, hr = HEAD_DIM // 2, MLA_ROPE // 2
    inv = jnp.concatenate([inv_f, inv_f, inv_r, inv_r, zpad]).reshape(1, 2 * LANE)
    sgn = jnp.concatenate([-jnp.ones(hf), jnp.ones(hf), -jnp.ones(hr), jnp.ones(hr), zpad]).astype(F32).reshape(1, 2 * LANE)
    msk = jnp.concatenate([jnp.ones(HEAD_DIM + MLA_ROPE), zpad]).astype(F32).reshape(1, 2 * LANE)
    tm = _tile(m, 1024)
    row = pl.BlockSpec((1, 2 * LANE), lambda i: (0, 0))
    tab = pl.BlockSpec((tm, 2 * LANE), lambda i: (i, 0))
    return pl.pallas_call(
        _rope_tab_kernel,
        grid=(m // tm,),
        in_specs=[pl.BlockSpec((tm, 1), lambda i: (i, 0)), row, row, row],
        out_specs=[tab, tab],
        out_shape=[jax.ShapeDtypeStruct((m, 2 * LANE), F32)] * 2,
        compiler_params=_cparams("parallel"),
        name="rope_tables",
    )(positions.reshape(m, 1), inv, sgn, msk)


def _prenorm_kernel(x_ref, g_ref, sh_ref, sc_ref, o_ref):
    y = _rms(x_ref[...], g_ref[...])
    o_ref[...] = (y * (1.0 + sc_ref[...]) + sh_ref[...]).astype(o_ref.dtype)


def _prenorm(x2d, g, shift, scale, seq):
    m, d = x2d.shape
    bsz = m // seq
    tm = _tile(seq, 256)
    per_b = seq // tm
    vec = pl.BlockSpec((None, 1, d), lambda i: (i // per_b, 0, 0))
    return pl.pallas_call(
        _prenorm_kernel,
        grid=(m // tm,),
        in_specs=[pl.BlockSpec((tm, d), lambda i: (i, 0)),
                  pl.BlockSpec((1, d), lambda i: (0, 0)), vec, vec],
        out_specs=pl.BlockSpec((tm, d), lambda i: (i, 0)),
        out_shape=jax.ShapeDtypeStruct((m, d), BF16),
        compiler_params=_cparams("parallel"),
        name="prenorm",
    )(x2d, g.reshape(1, d), shift.reshape(bsz, 1, d), scale.reshape(bsz, 1, d))


def _mm_kernel(a_ref, w_ref, o_ref):
    o_ref[...] = jnp.dot(a_ref[...], w_ref[...], preferred_element_type=F32).astype(o_ref.dtype)


def _in_small(h, w_s):
    m, d = h.shape
    n = w_s.shape[1]
    tm = _tile(m, 512)
    return pl.pallas_call(
        _mm_kernel,
        grid=(m // tm,),
        in_specs=[pl.BlockSpec((tm, d), lambda i: (i, 0)),
                  pl.BlockSpec((d, n), lambda i: (0, 0))],
        out_specs=pl.BlockSpec((tm, n), lambda i: (i, 0)),
        out_shape=jax.ShapeDtypeStruct((m, n), F32),
        compiler_params=_cparams("parallel"),
        name="in_small",
    )(h, w_s)


def _rope128(x, c, s):
    return x * c + pltpu.roll(x, HEAD_DIM // 2, 1) * s


def _rope_mla(x, c, s):
    half = MLA_ROPE // 2
    return x * c + (pltpu.roll(x, LANE - half, 1) + pltpu.roll(x, half, 1)) * s


def _proj_rope_kernel(h_ref, w_ref, cos_ref, sin_ref, o_ref, *, scale):
    acc = jnp.dot(h_ref[...], w_ref[...], preferred_element_type=F32)
    c = cos_ref[...]
    s = sin_ref[...]
    for hh in range(acc.shape[1] // HEAD_DIM):
        sl = slice(hh * HEAD_DIM, (hh + 1) * HEAD_DIM)
        y = _rope128(acc[:, sl], c, s)
        o_ref[:, sl] = (y * scale if scale != 1.0 else y).astype(o_ref.dtype)


def _proj_sigmoid_kernel(h_ref, w_ref, o_ref):
    acc = jnp.dot(h_ref[...], w_ref[...], preferred_element_type=F32)
    o_ref[...] = jax.nn.sigmoid(acc).astype(o_ref.dtype)


def _in_proj(h, w, kind, cos_t=None, sin_t=None, scale=1.0):
    m, d = h.shape
    n = w.shape[1]
    tm = _tile(m, 1024)
    tn = _tile(n, DIL_COLS)
    in_specs = [pl.BlockSpec((tm, d), lambda j, i: (i, 0)),
                pl.BlockSpec((d, tn), lambda j, i: (0, j))]
    args = [h, w]
    if kind == "rope":
        body = functools.partial(_proj_rope_kernel, scale=scale)
        in_specs += [pl.BlockSpec((tm, LANE), lambda j, i: (i, 0))] * 2
        args += [cos_t, sin_t]
    elif kind == "sigmoid":
        body = _proj_sigmoid_kernel
    else:
        body = _mm_kernel
    return pl.pallas_call(
        body,
        grid=(n // tn, m // tm),
        in_specs=in_specs,
        out_specs=pl.BlockSpec((tm, tn), lambda j, i: (i, j)),
        out_shape=jax.ShapeDtypeStruct((m, n), BF16),
        compiler_params=_cparams("parallel", "arbitrary"),
        name=f"in_proj_{kind}",
    )(*args)


def _qup_kernel(z_ref, g_ref, w_ref, cos_ref, sin_ref, o_ref, zn_ref):
    @pl.when(pl.program_id(1) == 0)
    def _():
        zn_ref[...] = _rms(z_ref[...], g_ref[...]).astype(BF16)

    acc = jnp.dot(zn_ref[...], w_ref[...], preferred_element_type=F32)
    c = cos_ref[...]
    s = sin_ref[...]
    for hh in range(acc.shape[1] // MLA_QK_PAD):
        lo = hh * MLA_QK_PAD
        o_ref[:, lo:lo + LANE] = (acc[:, lo:lo + LANE] * MLA_QSCALE).astype(o_ref.dtype)
        rp = _rope_mla(acc[:, lo + LANE:lo + 2 * LANE], c, s)
        o_ref[:, lo + LANE:lo + 2 * LANE] = (rp * MLA_QSCALE).astype(o_ref.dtype)


def _qup(z_s, g, w_uq_pad, cos_t, sin_t):
    m = z_s.shape[0]
    k = MLA_Q_RANK
    n = w_uq_pad.shape[1]
    tm = _tile(m, 1024)
    tn = _tile(n, 1024)
    return pl.pallas_call(
        _qup_kernel,
        grid=(m // tm, n // tn),
        in_specs=[pl.BlockSpec((tm, k), lambda i, j: (i, 0)),
                  pl.BlockSpec((1, k), lambda i, j: (0, 0)),
                  pl.BlockSpec((k, tn), lambda i, j: (0, j)),
                  pl.BlockSpec((tm, LANE), lambda i, j: (i, 1)),
                  pl.BlockSpec((tm, LANE), lambda i, j: (i, 1))],
        out_specs=pl.BlockSpec((tm, tn), lambda i, j: (i, j)),
        out_shape=jax.ShapeDtypeStruct((m, n), BF16),
        scratch_shapes=[pltpu.VMEM((tm, k), BF16)],
        compiler_params=_cparams("parallel", "arbitrary"),
        name="q_up",
    )(z_s, g.reshape(1, k), w_uq_pad, cos_t, sin_t)


def _kvup_kernel(z_ref, kr_ref, g_ref, w_ref, cos_ref, sin_ref, k_ref, v_ref, zn_ref):
    @pl.when(pl.program_id(1) == 0)
    def _():
        zn_ref[...] = _rms(z_ref[...], g_ref[...]).astype(BF16)

    acc = jnp.dot(zn_ref[...], w_ref[...], preferred_element_type=F32)
    kr = _rope_mla(kr_ref[...], cos_ref[...], sin_ref[...]).astype(k_ref.dtype)
    for hh in range(acc.shape[1] // (MLA_NOPE + MLA_V)):
        lo = hh * (MLA_NOPE + MLA_V)
        ko = hh * MLA_QK_PAD
        k_ref[:, ko:ko + LANE] = acc[:, lo:lo + MLA_NOPE].astype(k_ref.dtype)
        k_ref[:, ko + LANE:ko + 2 * LANE] = kr
        v_ref[:, hh * MLA_V:(hh + 1) * MLA_V] = acc[:, lo + MLA_NOPE:lo + MLA_NOPE + MLA_V].astype(v_ref.dtype)


def _kvup(z_s, g, w_ukv, cos_t, sin_t):
    m = z_s.shape[0]
    k = MLA_KV_RANK
    n = w_ukv.shape[1]
    tm = _tile(m, 1024)
    tn = _tile(n, 1024)
    nh = tn // (MLA_NOPE + MLA_V)
    kv_blk = MLA_Q_RANK // MLA_KV_RANK
    kr_blk = (MLA_Q_RANK + MLA_KV_RANK) // LANE
    return pl.pallas_call(
        _kvup_kernel,
        grid=(m // tm, n // tn),
        in_specs=[pl.BlockSpec((tm, k), lambda i, j: (i, kv_blk)),
                  pl.BlockSpec((tm, LANE), lambda i, j: (i, kr_blk)),
                  pl.BlockSpec((1, k), lambda i, j: (0, 0)),
                  pl.BlockSpec((k, tn), lambda i, j: (0, j)),
                  pl.BlockSpec((tm, LANE), lambda i, j: (i, 1)),
                  pl.BlockSpec((tm, LANE), lambda i, j: (i, 1))],
        out_specs=[pl.BlockSpec((tm, nh * MLA_QK_PAD), lambda i, j: (i, j)),
                   pl.BlockSpec((tm, nh * MLA_V), lambda i, j: (i, j))],
        out_shape=[jax.ShapeDtypeStruct((m, MLA_HEADS * MLA_QK_PAD), BF16),
                   jax.ShapeDtypeStruct((m, MLA_HEADS * MLA_V), BF16)],
        scratch_shapes=[pltpu.VMEM((tm, k), BF16)],
        compiler_params=_cparams("parallel", "arbitrary"),
        name="kv_up",
    )(z_s, z_s, g.reshape(1, k), w_ukv, cos_t, sin_t)


def _mla_kernel(q_ref, k_ref, v_ref, o_ref, *, sub):
    for r0 in range(0, q_ref.shape[0], sub):
        s = lax.dot_general(q_ref[r0:r0 + sub, :], k_ref[...], (((1,), (1,)), ((), ())),
                            preferred_element_type=F32)
        m = jnp.max(s, axis=-1, keepdims=True)
        p = jnp.exp2(s - m)
        l = jnp.sum(p, axis=-1, keepdims=True)
        o = jnp.dot(p.astype(BF16), v_ref[...], preferred_element_type=F32)
        o_ref[r0:r0 + sub, :] = (o / l).astype(o_ref.dtype)


def _mla_attention(q, k, v, bsz, seq):
    q3 = q.reshape(bsz, seq, MLA_HEADS * MLA_QK_PAD)
    k3 = k.reshape(bsz, seq, MLA_HEADS * MLA_QK_PAD)
    v3 = v.reshape(bsz, seq, MLA_HEADS * MLA_V)
    tq = _tile(seq, 1024)
    out = pl.pallas_call(
        functools.partial(_mla_kernel, sub=min(tq, 256)),
        grid=(bsz, MLA_HEADS, seq // tq),
        in_specs=[pl.BlockSpec((None, tq, MLA_QK_PAD), lambda b, h, i: (b, i, h)),
                  pl.BlockSpec((None, seq, MLA_QK_PAD), lambda b, h, i: (b, 0, h)),
                  pl.BlockSpec((None, seq, MLA_V), lambda b, h, i: (b, 0, h))],
        out_specs=pl.BlockSpec((None, tq, MLA_V), lambda b, h, i: (b, i, h)),
        out_shape=jax.ShapeDtypeStruct((bsz, seq, MLA_HEADS * MLA_V), BF16),
        compiler_params=_cparams("parallel", "parallel", "arbitrary"),
        name="mla_attn",
    )(q3, k3, v3)
    return out.reshape(bsz * seq, MLA_HEADS * MLA_V)


def _dil_kernel(q_ref, k_ref, v_ref, o_ref, st_ref, *, t_len, tq, kw, n_side, hb):
    lane = lax.broadcasted_iota(jnp.int32, (tq, LANE), 1)
    row = lax.broadcasted_iota(jnp.int32, (tq, kw), 0)
    col = lax.broadcasted_iota(jnp.int32, (tq, kw), 1)

    def tile(qi, carry):
        t0 = pl.multiple_of(qi * tq, tq)
        ws = jnp.clip(t0 - n_side, 0, t_len - kw)
        ws = pl.multiple_of(ws, n_side) if t_len > kw else 0
        valid = jnp.abs((t0 + row) - (ws + col)) <= n_side
        stats = jnp.zeros((tq, LANE), F32)
        for hh in range(hb):
            sl = slice(hh * HEAD_DIM, (hh + 1) * HEAD_DIM)
            q = q_ref[pl.ds(t0, tq), sl]
            k = k_ref[pl.ds(ws, kw), sl]
            v = v_ref[pl.ds(ws, kw), sl]
            s = lax.dot_general(q, k, (((1,), (1,)), ((), ())), preferred_element_type=F32)
            s = jnp.where(valid, s, -jnp.inf)
            m = jnp.max(s, axis=-1, keepdims=True)
            p = jnp.exp2(s - m)
            l = jnp.sum(p, axis=-1, keepdims=True)
            o = jnp.dot(p.astype(BF16), v, preferred_element_type=F32)
            o_ref[pl.ds(t0, tq), sl] = (o / l).astype(o_ref.dtype)
            stats = jnp.where(lane == hh, m, stats)
            stats = jnp.where(lane == hb + hh, l, stats)
        st_ref[pl.ds(t0, tq), :] = stats
        return carry

    lax.fori_loop(0, t_len // tq, tile, 0)


def _dil_heads_per_block(t_len):
    return DIL_HEADS if t_len <= 1024 else DIL_HEADS // 2


def _dilated_group(zq, zk, zv, g, bsz, seq, dil, window):
    t_len = seq // dil
    n_side = window // (2 * dil)
    hb = _dil_heads_per_block(t_len)
    nhb = DIL_HEADS // hb
    cw = hb * HEAD_DIM
    tq = min(2 * n_side, t_len)
    kw = min(4 * n_side, t_len)
    if dil == 1:
        views = [z.reshape(bsz, seq, z.shape[1]) for z in (zq, zk, zv)]
        per_tok = zq.shape[1] // cw
        base = g * DIL_COLS // cw
    else:
        views = [z[:, g * DIL_COLS:(g + 1) * DIL_COLS].reshape(bsz, t_len, dil * DIL_COLS)
                 for z in (zq, zk, zv)]
        per_tok = DIL_COLS // cw
        base = 0
    in_spec = pl.BlockSpec((None, t_len, cw), lambda b, r, hbi: (b, 0, r * per_tok + base + hbi))

    o, st = pl.pallas_call(
        functools.partial(_dil_kernel, t_len=t_len, tq=tq, kw=kw, n_side=n_side, hb=hb),
        grid=(bsz, dil, nhb),
        in_specs=[in_spec, in_spec, in_spec],
        out_specs=[pl.BlockSpec((None, t_len, cw), lambda b, r, hbi: (b, 0, r * nhb + hbi)),
                   pl.BlockSpec((None, t_len, LANE), lambda b, r, hbi: (b, 0, r * nhb + hbi))],
        out_shape=[jax.ShapeDtypeStruct((bsz, t_len, dil * DIL_COLS), BF16),
                   jax.ShapeDtypeStruct((bsz, t_len, dil * nhb * LANE), F32)],
        compiler_params=_cparams("parallel", "parallel", "parallel"),
        name=f"dilated_d{dil}",
    )(*views)
    return o.reshape(bsz * seq, DIL_COLS), st.reshape(bsz * seq, nhb * LANE), hb


def _merge_kernel(*refs, hbs):
    ng = len(hbs)
    o_refs, st_refs, y_ref = refs[:ng], refs[ng:2 * ng], refs[2 * ng]
    sts = [r[...] for r in st_refs]
    for h in range(DIL_HEADS):
        ms, ls = [], []
        for g in range(ng):
            hb = hbs[g]
            lo = (h // hb) * LANE + h % hb
            ms.append(sts[g][:, lo:lo + 1])
            ls.append(sts[g][:, lo + hb:lo + hb + 1])
        m_all = functools.reduce(jnp.maximum, ms)
        a = [jnp.exp2(ms[g] - m_all) * ls[g] for g in range(ng)]
        den = functools.reduce(lambda u, w: u + w, a)
        sl = slice(h * HEAD_DIM, (h + 1) * HEAD_DIM)
        y = None
        for g in range(ng):
            term = (a[g] / den) * o_refs[g][:, sl].astype(F32)
            y = term if y is None else y + term
        y_ref[:, sl] = y.astype(y_ref.dtype)


def _merge_groups(os_, sts, hbs):
    m = os_[0].shape[0]
    tm = _tile(m, 512)
    in_specs = [pl.BlockSpec((tm, DIL_COLS), lambda i: (i, 0)) for _ in os_]
    in_specs += [pl.BlockSpec((tm, s.shape[1]), lambda i: (i, 0)) for s in sts]
    return pl.pallas_call(
        functools.partial(_merge_kernel, hbs=tuple(hbs)),
        grid=(m // tm,),
        in_specs=in_specs,
        out_specs=pl.BlockSpec((tm, DIL_COLS), lambda i: (i, 0)),
        out_shape=jax.ShapeDtypeStruct((m, DIL_COLS), BF16),
        compiler_params=_cparams("parallel"),
        name="dil_merge",
    )(*os_, *sts)


def _mix_kernel(ya_ref, yb_ref, wa_ref, wb_ref, ga_ref, gb_ref, o_ref):
    pa = jnp.dot(ya_ref[...], wa_ref[...], preferred_element_type=F32)
    pb = jnp.dot(yb_ref[...], wb_ref[...], preferred_element_type=F32)
    o_ref[...] = (ga_ref[...].astype(F32) * pa + gb_ref[...].astype(F32) * pb).astype(o_ref.dtype)


def _mix(ya, yb, w_pa, w_pb, gates):
    m = ya.shape[0]
    d = w_pa.shape[1]
    tm = _tile(m, 1024)
    tn = _tile(d, 1024)
    ga0 = 0
    gb0 = d // tn
    return pl.pallas_call(
        _mix_kernel,
        grid=(m // tm, d // tn),
        in_specs=[pl.BlockSpec((tm, ya.shape[1]), lambda i, j: (i, 0)),
                  pl.BlockSpec((tm, yb.shape[1]), lambda i, j: (i, 0)),
                  pl.BlockSpec((w_pa.shape[0], tn), lambda i, j: (0, j)),
                  pl.BlockSpec((w_pb.shape[0], tn), lambda i, j: (0, j)),
                  pl.BlockSpec((tm, tn), lambda i, j: (i, ga0 + j)),
                  pl.BlockSpec((tm, tn), lambda i, j: (i, gb0 + j))],
        out_specs=pl.BlockSpec((tm, tn), lambda i, j: (i, j)),
        out_shape=jax.ShapeDtypeStruct((m, d), BF16),
        compiler_params=_cparams("parallel", "arbitrary"),
        name="mix",
    )(ya, yb, w_pa, w_pb, gates, gates)


def _outproj_kernel(a_ref, w_ref, x_ref, g_ref, o_ref):
    acc = jnp.dot(a_ref[...], w_ref[...], preferred_element_type=F32)
    o_ref[...] = x_ref[...] + g_ref[...] * acc


def _outproj(a, w, x2d, gate, seq):
    m, k = a.shape
    d = w.shape[1]
    bsz = m // seq
    tm = _tile(seq, 1024)
    tn = _tile(d, 512)
    per_b = seq // tm
    return pl.pallas_call(
        _outproj_kernel,
        grid=(m // tm, d // tn),
        in_specs=[pl.BlockSpec((tm, k), lambda i, j: (i, 0)),
                  pl.BlockSpec((k, tn), lambda i, j: (0, j)),
                  pl.BlockSpec((tm, tn), lambda i, j: (i, j)),
                  pl.BlockSpec((None, 1, tn), lambda i, j: (i // per_b, 0, j))],
        out_specs=pl.BlockSpec((tm, tn), lambda i, j: (i, j)),
        out_shape=jax.ShapeDtypeStruct((m, d), F32),
        compiler_params=_cparams("parallel", "arbitrary"),
        name="out_proj",
    )(a, w, x2d, gate.reshape(bsz, 1, d))


def _ffn_up_kernel(h_ref, wg_ref, wu_ref, o_ref):
    h = h_ref[...]
    g = jnp.dot(h, wg_ref[...], preferred_element_type=F32)
    u = jnp.dot(h, wu_ref[...], preferred_element_type=F32)
    o_ref[...] = (g * jax.nn.sigmoid(g) * u).astype(o_ref.dtype)


def _ffn_up(h, wg, wu):
    m, d = h.shape
    f = wg.shape[1]
    tm = _tile(m, 1024)
    tn = _tile(f, 512)
    return pl.pallas_call(
        _ffn_up_kernel,
        grid=(m // tm, f // tn),
        in_specs=[pl.BlockSpec((tm, d), lambda i, j: (i, 0)),
                  pl.BlockSpec((d, tn), lambda i, j: (0, j)),
                  pl.BlockSpec((d, tn), lambda i, j: (0, j))],
        out_specs=pl.BlockSpec((tm, tn), lambda i, j: (i, j)),
        out_shape=jax.ShapeDtypeStruct((m, f), BF16),
        compiler_params=_cparams("parallel", "arbitrary"),
        name="ffn_up",
    )(h, wg, wu)


def _ffn_down_kernel(a_ref, w_ref, x_ref, g_ref, fg_ref, o_ref, *, nk, final_norm):
    k = pl.program_id(1)

    @pl.when(k == 0)
    def _():
        o_ref[...] = jnp.dot(a_ref[...], w_ref[...], preferred_element_type=F32)

    @pl.when(k > 0)
    def _():
        o_ref[...] = jnp.dot(a_ref[...], w_ref[...], preferred_element_type=F32) + o_ref[...]

    @pl.when(k == nk - 1)
    def _():
        x2 = x_ref[...] + g_ref[...] * o_ref[...]
        o_ref[...] = _rms(x2, fg_ref[...]) if final_norm else x2


def _ffn_down(a, w, x2d, gate, final_g, seq, final_norm):
    m, f = a.shape
    d = w.shape[1]
    bsz = m // seq
    tm = _tile(seq, 512)
    tk = _tile(f, 1024)
    nk = f // tk
    per_b = seq // tm
    return pl.pallas_call(
        functools.partial(_ffn_down_kernel, nk=nk, final_norm=final_norm),
        grid=(m // tm, nk),
        in_specs=[pl.BlockSpec((tm, tk), lambda i, k: (i, k)),
                  pl.BlockSpec((tk, d), lambda i, k: (k, 0)),
                  pl.BlockSpec((tm, d), lambda i, k: (i, 0), pipeline_mode=pl.Buffered(1)),
                  pl.BlockSpec((None, 1, d), lambda i, k: (i // per_b, 0, 0)),
                  pl.BlockSpec((1, d), lambda i, k: (0, 0))],
        out_specs=pl.BlockSpec((tm, d), lambda i, k: (i, 0)),
        out_shape=jax.ShapeDtypeStruct((m, d), F32),
        compiler_params=_cparams("parallel", "arbitrary"),
        name="ffn_down",
    )(a, w, x2d, gate.reshape(bsz, 1, d), final_g.reshape(1, d))


def _cast_pad(w, axis, mult):
    wb = w.astype(BF16)
    pad = (-w.shape[axis]) % mult
    if not pad:
        return wb
    zshape = list(w.shape)
    zshape[axis] = pad
    return jnp.concatenate([wb, jnp.zeros(zshape, BF16)], axis=axis)


def kernel(x, c, positions, w_ada, b_ada, norm1_g, w_in, q_norm_g, w_uq, kv_norm_g, w_ukv,
           w_proj_a, w_proj_b, w_out, norm2_g, w_gate, w_up, w_down, final_g):
    bsz, seq, d = x.shape
    depth = w_ada.shape[0]
    m = bsz * seq
    cos_t, sin_t = _rope_tables(positions)
    x2d = x.reshape(m, d)

    for l in range(depth):
        mod = _ada(c, w_ada[l], b_ada[l])
        sh1, sc1, g1, sh2, sc2, g2 = jnp.split(mod, N_MOD, axis=-1)

        n_small = MLA_Q_RANK + MLA_KV_RANK + MLA_ROPE
        n_slab = DIL_GROUPS * DIL_COLS
        w_l = w_in[l]
        h = _prenorm(x2d, norm1_g[l], sh1, sc1, seq)
        z_s = _in_small(h, _cast_pad(w_l[:, :n_small], 1, LANE))
        zq = _in_proj(h, w_l[:, n_small:n_small + n_slab].astype(BF16), "rope", cos_t, sin_t, DIL_QSCALE)
        zk = _in_proj(h, w_l[:, n_small + n_slab:n_small + 2 * n_slab].astype(BF16), "rope", cos_t, sin_t)
        zv = _in_proj(h, w_l[:, n_small + 2 * n_slab:n_small + 3 * n_slab].astype(BF16), "plain")
        gates = _in_proj(h, w_l[:, n_small + 3 * n_slab:].astype(BF16), "sigmoid")

        w_uq_pad = jnp.pad(w_uq[l].reshape(MLA_Q_RANK, MLA_HEADS, MLA_NOPE + MLA_ROPE),
                           ((0, 0), (0, 0), (0, MLA_QK_PAD - MLA_NOPE - MLA_ROPE)))
        w_uq_pad = w_uq_pad.reshape(MLA_Q_RANK, MLA_HEADS * MLA_QK_PAD).astype(BF16)
        q = _qup(z_s, q_norm_g[l], w_uq_pad, cos_t, sin_t)
        k, v = _kvup(z_s, kv_norm_g[l], w_ukv[l].astype(BF16), cos_t, sin_t)
        y_a = _mla_attention(q, k, v, bsz, seq)

        os_, sts, hbs = [], [], []
        for g, (window, dil) in enumerate(DIL_PATTERNS):
            o_g, st_g, hb = _dilated_group(zq, zk, zv, g, bsz, seq, dil, window)
            os_.append(o_g)
            sts.append(st_g)
            hbs.append(hb)
        y_b = _merge_groups(os_, sts, hbs)

        mixed = _mix(y_a, y_b, w_proj_a[l].astype(BF16), w_proj_b[l].astype(BF16), gates)
        x2d = _outproj(mixed, w_out[l].astype(BF16), x2d, g1, seq)

        h2 = _prenorm(x2d, norm2_g[l], sh2, sc2, seq)
        ff_mult = 1024
        wg = _cast_pad(w_gate[l], 1, ff_mult)
        wu = _cast_pad(w_up[l], 1, ff_mult)
        wd = _cast_pad(w_down[l], 0, ff_mult)
        hmid = _ffn_up(h2, wg, wu)
        x2d = _ffn_down(hmid, wd, x2d, g2, final_g, seq, final_norm=(l == depth - 1))

    return x2d.reshape(bsz, seq, d)
```

```python
import functools
import math

import jax
import jax.numpy as jnp
from jax import lax
from jax.experimental import pallas as pl
from jax.experimental.pallas import tpu as pltpu

F32 = jnp.float32
BF16 = jnp.bfloat16

NORM_EPS = 1e-6
ROPE_THETA = 10000.0
HEAD_DIM = 128
MLA_HEADS = 16
MLA_Q_RANK = 1024
MLA_KV_RANK = 512
MLA_NOPE = 128
MLA_ROPE = 64
MLA_V = 128
DIL_HEADS = 8
DIL_PATTERNS = ((128, 1), (512, 4), (2048, 16))
DIL_GROUPS = len(DIL_PATTERNS)
N_MOD = 6
LOG2E = 1.4426950408889634

LANE = 128
MLA_QK_PAD = 256
DIL_COLS = DIL_HEADS * HEAD_DIM
VMEM_LIMIT = 56 * 1024 * 1024

MLA_QSCALE = (MLA_NOPE + MLA_ROPE) ** -0.5 * LOG2E
DIL_QSCALE = HEAD_DIM ** -0.5 * LOG2E


def _cparams(*sem):
    return pltpu.CompilerParams(dimension_semantics=sem, vmem_limit_bytes=VMEM_LIMIT)


def _tile(n, pref):
    return pref if n % pref == 0 else n


def _rms(x, g):
    return x * lax.rsqrt(jnp.mean(x * x, axis=-1, keepdims=True) + NORM_EPS) * g


def _ada_kernel(c_ref, w_ref, b_ref, o_ref):
    c = c_ref[...]
    ca = (c * jax.nn.sigmoid(c)).astype(BF16)
    o_ref[...] = jnp.dot(ca, w_ref[...].astype(BF16), preferred_element_type=F32) + b_ref[...]


def _ada(c, w, b):
    bsz, d = c.shape
    n = w.shape[1]
    rows = 8
    cp = jnp.zeros((rows, d), F32).at[:bsz].set(c)
    tn = _tile(n, 512)
    out = pl.pallas_call(
        _ada_kernel,
        grid=(n // tn,),
        in_specs=[pl.BlockSpec((rows, d), lambda j: (0, 0)),
                  pl.BlockSpec((d, tn), lambda j: (0, j)),
                  pl.BlockSpec((1, tn), lambda j: (0, j))],
        out_specs=pl.BlockSpec((rows, tn), lambda j: (0, j)),
        out_shape=jax.ShapeDtypeStruct((rows, n), F32),
        compiler_params=_cparams("parallel"),
        name="ada",
    )(cp, w, b.reshape(1, n))
    return out[:bsz]


def _rope_tab_kernel(pos_ref, inv_ref, sgn_ref, msk_ref, cos_ref, sin_ref):
    ang = pos_ref[...].astype(F32) * inv_ref[...]
    cos_ref[...] = jnp.cos(ang) * msk_ref[...]
    sin_ref[...] = jnp.sin(ang) * sgn_ref[...]


def _rope_tables(positions):
    m = positions.size
    inv_f = ROPE_THETA ** (-jnp.arange(0, HEAD_DIM, 2, dtype=F32) / HEAD_DIM)
    inv_r = ROPE_THETA ** (-jnp.arange(0, MLA_ROPE, 2, dtype=F32) / MLA_ROPE)
    zpad = jnp.zeros((LANE - MLA_ROPE,), F32)
    hf, hr = HEAD_DIM // 2, MLA_ROPE // 2
    inv = jnp.concatenate([inv_f, inv_f, inv_r, inv_r, zpad]).reshape(1, 2 * LANE)
    sgn = jnp.concatenate([-jnp.ones(hf), jnp.ones(hf), -jnp.ones(hr), jnp.ones(hr), zpad]).astype(F32).reshape(1, 2 * LANE)
    msk = jnp.concatenate([jnp.ones(HEAD_DIM + MLA_ROPE), zpad]).astype(F32).reshape(1, 2 * LANE)
    tm = _tile(m, 1024)
    row = pl.BlockSpec((1, 2 * LANE), lambda i: (0, 0))
    tab = pl.BlockSpec((tm, 2 * LANE), lambda i: (i, 0))
    return pl.pallas_call(
        _rope_tab_kernel,
        grid=(m // tm,),
        in_specs=[pl.BlockSpec((tm, 1), lambda i: (i, 0)), row, row, row],
        out_specs=[tab, tab],
        out_shape=[jax.ShapeDtypeStruct((m, 2 * LANE), F32)] * 2,
        compiler_params=_cparams("parallel"),
        name="rope_tables",
    )(positions.reshape(m, 1), inv, sgn, msk)


def _prenorm_kernel(x_ref, g_ref, sh_ref, sc_ref, o_ref):
    y = _rms(x_ref[...], g_ref[...])
    o_ref[...] = (y * (1.0 + sc_ref[...]) + sh_ref[...]).astype(o_ref.dtype)


def _prenorm(x2d, g, shift, scale, seq):
    m, d = x2d.shape
    bsz = m // seq
    tm = _tile(seq, 256)
    per_b = seq // tm
    vec = pl.BlockSpec((None, 1, d), lambda i: (i // per_b, 0, 0))
    return pl.pallas_call(
        _prenorm_kernel,
        grid=(m // tm,),
        in_specs=[pl.BlockSpec((tm, d), lambda i: (i, 0)),
                  pl.BlockSpec((1, d), lambda i: (0, 0)), vec, vec],
        out_specs=pl.BlockSpec((tm, d), lambda i: (i, 0)),
        out_shape=jax.ShapeDtypeStruct((m, d), BF16),
        compiler_params=_cparams("parallel"),
        name="prenorm",
    )(x2d, g.reshape(1, d), shift.reshape(bsz, 1, d), scale.reshape(bsz, 1, d))


def _mm_kernel(a_ref, w_ref, o_ref):
    o_ref[...] = jnp.dot(a_ref[...], w_ref[...], preferred_element_type=F32).astype(o_ref.dtype)


def _in_small(h, w_s):
    m, d = h.shape
    n = w_s.shape[1]
    tm = _tile(m, 512)
    return pl.pallas_call(
        _mm_kernel,
        grid=(m // tm,),
        in_specs=[pl.BlockSpec((tm, d), lambda i: (i, 0)),
                  pl.BlockSpec((d, n), lambda i: (0, 0))],
        out_specs=pl.BlockSpec((tm, n), lambda i: (i, 0)),
        out_shape=jax.ShapeDtypeStruct((m, n), F32),
        compiler_params=_cparams("parallel"),
        name="in_small",
    )(h, w_s)


def _rope128(x, c, s):
    return x * c + pltpu.roll(x, HEAD_DIM // 2, 1) * s


def _rope_mla(x, c, s):
    half = MLA_ROPE // 2
    return x * c + (pltpu.roll(x, LANE - half, 1) + pltpu.roll(x, half, 1)) * s


def _dil_proj_kernel(*refs, rope, scale, dil):
    if rope:
        h_ref, w_ref, cos_ref, sin_ref, o_ref = refs[:5]
    else:
        h_ref, w_ref, o_ref = refs[:3]
    acc = jnp.dot(h_ref[...], w_ref[...], preferred_element_type=F32)
    tm = acc.shape[0]

    def finish(y, c, s):
        if rope:
            y = _rope128(y, c, s)
            if scale != 1.0:
                y = y * scale
        return y.astype(o_ref.dtype)

    if dil == 1:
        c, s = (cos_ref[...], sin_ref[...]) if rope else (None, None)
        for hh in range(DIL_HEADS):
            sl = slice(hh * HEAD_DIM, (hh + 1) * HEAD_DIM)
            o_ref[:, sl] = finish(acc[:, sl], c, s)
        return

    scr_ref = refs[-1]
    rows = tm // dil
    for hh in range(DIL_HEADS):
        scr_ref[hh] = acc[:, hh * HEAD_DIM:(hh + 1) * HEAD_DIM]
    for r in range(dil):
        take = pl.ds(r, rows, stride=dil)
        c, s = (cos_ref[take, :], sin_ref[take, :]) if rope else (None, None)
        for hh in range(DIL_HEADS):
            lo = r * DIL_COLS + hh * HEAD_DIM
            o_ref[:, lo:lo + HEAD_DIM] = finish(scr_ref[hh, take, :], c, s)


def _dil_proj(h, w_dg, col_tile, dil, seq, cos_t=None, sin_t=None, scale=1.0):
    m, d = h.shape
    bsz = m // seq
    tm = _tile(seq, 1024)
    per_b = seq // tm
    rope = cos_t is not None
    in_specs = [pl.BlockSpec((tm, d), lambda i: (i, 0)),
                pl.BlockSpec((d, DIL_COLS), lambda i: (0, col_tile))]
    args = [h, w_dg]
    if rope:
        in_specs += [pl.BlockSpec((tm, LANE), lambda i: (i, 0))] * 2
        args += [cos_t, sin_t]
    scratch = [pltpu.VMEM((DIL_HEADS, tm, HEAD_DIM), F32)] if dil > 1 else []
    return pl.pallas_call(
        functools.partial(_dil_proj_kernel, rope=rope, scale=scale, dil=dil),
        grid=(m // tm,),
        in_specs=in_specs,
        out_specs=pl.BlockSpec((None, tm // dil, dil * DIL_COLS), lambda i: (i // per_b, i % per_b, 0)),
        out_shape=jax.ShapeDtypeStruct((bsz, seq // dil, dil * DIL_COLS), BF16),
        scratch_shapes=scratch,
        compiler_params=_cparams("parallel"),
        name=f"dil_proj_d{dil}",
    )(*args)


def _proj_sigmoid_kernel(h_ref, w_ref, o_ref):
    acc = jnp.dot(h_ref[...], w_ref[...], preferred_element_type=F32)
    o_ref[...] = jax.nn.sigmoid(acc).astype(o_ref.dtype)


def _gate_proj(h, w_dg, col_tile0, n):
    m, d = h.shape
    tm = _tile(m, 1024)
    tn = DIL_COLS
    return pl.pallas_call(
        _proj_sigmoid_kernel,
        grid=(n // tn, m // tm),
        in_specs=[pl.BlockSpec((tm, d), lambda j, i: (i, 0)),
                  pl.BlockSpec((d, tn), lambda j, i: (0, col_tile0 + j))],
        out_specs=pl.BlockSpec((tm, tn), lambda j, i: (i, j)),
        out_shape=jax.ShapeDtypeStruct((m, n), BF16),
        compiler_params=_cparams("parallel", "arbitrary"),
        name="gate_proj",
    )(h, w_dg)


def _qup_kernel(z_ref, g_ref, w_ref, cos_ref, sin_ref, o_ref, zn_ref):
    @pl.when(pl.program_id(1) == 0)
    def _():
        zn_ref[...] = _rms(z_ref[...], g_ref[...]).astype(BF16)

    acc = jnp.dot(zn_ref[...], w_ref[...], preferred_element_type=F32)
    c = cos_ref[...]
    s = sin_ref[...]
    for hh in range(acc.shape[1] // MLA_QK_PAD):
        lo = hh * MLA_QK_PAD
        o_ref[:, lo:lo + LANE] = (acc[:, lo:lo + LANE] * MLA_QSCALE).astype(o_ref.dtype)
        rp = _rope_mla(acc[:, lo + LANE:lo + 2 * LANE], c, s)
        o_ref[:, lo + LANE:lo + 2 * LANE] = (rp * MLA_QSCALE).astype(o_ref.dtype)


def _qup(z_s, g, w_uq_pad, cos_t, sin_t):
    m = z_s.shape[0]
    k = MLA_Q_RANK
    n = w_uq_pad.shape[1]
    tm = _tile(m, 1024)
    tn = _tile(n, 1024)
    return pl.pallas_call(
        _qup_kernel,
        grid=(m // tm, n // tn),
        in_specs=[pl.BlockSpec((tm, k), lambda i, j: (i, 0)),
                  pl.BlockSpec((1, k), lambda i, j: (0, 0)),
                  pl.BlockSpec((k, tn), lambda i, j: (0, j)),
                  pl.BlockSpec((tm, LANE), lambda i, j: (i, 1)),
                  pl.BlockSpec((tm, LANE), lambda i, j: (i, 1))],
        out_specs=pl.BlockSpec((tm, tn), lambda i, j: (i, j)),
        out_shape=jax.ShapeDtypeStruct((m, n), BF16),
        scratch_shapes=[pltpu.VMEM((tm, k), BF16)],
        compiler_params=_cparams("parallel", "arbitrary"),
        name="q_up",
    )(z_s, g.reshape(1, k), w_uq_pad, cos_t, sin_t)


def _kvup_kernel(z_ref, kr_ref, g_ref, w_ref, cos_ref, sin_ref, k_ref, v_ref, zn_ref):
    @pl.when(pl.program_id(1) == 0)
    def _():
        zn_ref[...] = _rms(z_ref[...], g_ref[...]).astype(BF16)

    acc = jnp.dot(zn_ref[...], w_ref[...], preferred_element_type=F32)
    kr = _rope_mla(kr_ref[...], cos_ref[...], sin_ref[...]).astype(k_ref.dtype)
    for hh in range(acc.shape[1] // (MLA_NOPE + MLA_V)):
        lo = hh * (MLA_NOPE + MLA_V)
        ko = hh * MLA_QK_PAD
        k_ref[:, ko:ko + LANE] = acc[:, lo:lo + MLA_NOPE].astype(k_ref.dtype)
        k_ref[:, ko + LANE:ko + 2 * LANE] = kr
        v_ref[:, hh * MLA_V:(hh + 1) * MLA_V] = acc[:, lo + MLA_NOPE:lo + MLA_NOPE + MLA_V].astype(v_ref.dtype)


def _kvup(z_s, g, w_ukv, cos_t, sin_t):
    m = z_s.shape[0]
    k = MLA_KV_RANK
    n = w_ukv.shape[1]
    tm = _tile(m, 1024)
    tn = _tile(n, 1024)
    nh = tn // (MLA_NOPE + MLA_V)
    kv_blk = MLA_Q_RANK // MLA_KV_RANK
    kr_blk = (MLA_Q_RANK + MLA_KV_RANK) // LANE
    return pl.pallas_call(
        _kvup_kernel,
        grid=(m // tm, n // tn),
        in_specs=[pl.BlockSpec((tm, k), lambda i, j: (i, kv_blk)),
                  pl.BlockSpec((tm, LANE), lambda i, j: (i, kr_blk)),
                  pl.BlockSpec((1, k), lambda i, j: (0, 0)),
                  pl.BlockSpec((k, tn), lambda i, j: (0, j)),
                  pl.BlockSpec((tm, LANE), lambda i, j: (i, 1)),
                  pl.BlockSpec((tm, LANE), lambda i, j: (i, 1))],
        out_specs=[pl.BlockSpec((tm, nh * MLA_QK_PAD), lambda i, j: (i, j)),
                   pl.BlockSpec((tm, nh * MLA_V), lambda i, j: (i, j))],
        out_shape=[jax.ShapeDtypeStruct((m, MLA_HEADS * MLA_QK_PAD), BF16),
                   jax.ShapeDtypeStruct((m, MLA_HEADS * MLA_V), BF16)],
        scratch_shapes=[pltpu.VMEM((tm, k), BF16)],
        compiler_params=_cparams("parallel", "arbitrary"),
        name="kv_up",
    )(z_s, z_s, g.reshape(1, k), w_ukv, cos_t, sin_t)


def _mla_kernel(q_ref, k_ref, v_ref, o_ref, *, sub):
    for r0 in range(0, q_ref.shape[0], sub):
        s = lax.dot_general(q_ref[r0:r0 + sub, :], k_ref[...], (((1,), (1,)), ((), ())),
                            preferred_element_type=F32)
        m = jnp.max(s, axis=-1, keepdims=True)
        p = jnp.exp2(s - m)
        l = jnp.sum(p, axis=-1, keepdims=True)
        o = jnp.dot(p.astype(BF16), v_ref[...], preferred_element_type=F32)
        o_ref[r0:r0 + sub, :] = (o / l).astype(o_ref.dtype)


def _mla_attention(q, k, v, bsz, seq):
    q3 = q.reshape(bsz, seq, MLA_HEADS * MLA_QK_PAD)
    k3 = k.reshape(bsz, seq, MLA_HEADS * MLA_QK_PAD)
    v3 = v.reshape(bsz, seq, MLA_HEADS * MLA_V)
    tq = _tile(seq, 1024)
    out = pl.pallas_call(
        functools.partial(_mla_kernel, sub=min(tq, 256)),
        grid=(bsz, MLA_HEADS, seq // tq),
        in_specs=[pl.BlockSpec((None, tq, MLA_QK_PAD), lambda b, h, i: (b, i, h)),
                  pl.BlockSpec((None, seq, MLA_QK_PAD), lambda b, h, i: (b, 0, h)),
                  pl.BlockSpec((None, seq, MLA_V), lambda b, h, i: (b, 0, h))],
        out_specs=pl.BlockSpec((None, tq, MLA_V), lambda b, h, i: (b, i, h)),
        out_shape=jax.ShapeDtypeStruct((bsz, seq, MLA_HEADS * MLA_V), BF16),
        compiler_params=_cparams("parallel", "parallel", "arbitrary"),
        name="mla_attn",
    )(q3, k3, v3)
    return out.reshape(bsz * seq, MLA_HEADS * MLA_V)


def _dil_kernel(q_ref, k_ref, v_ref, o_ref, st_ref, *, t_len, tq, kw, n_side, hb):
    lane = lax.broadcasted_iota(jnp.int32, (tq, LANE), 1)
    row = lax.broadcasted_iota(jnp.int32, (tq, kw), 0)
    col = lax.broadcasted_iota(jnp.int32, (tq, kw), 1)

    def tile(qi, carry):
        t0 = pl.multiple_of(qi * tq, tq)
        ws = jnp.clip(t0 - n_side, 0, t_len - kw)
        ws = pl.multiple_of(ws, n_side) if t_len > kw else 0
        valid = jnp.abs((t0 + row) - (ws + col)) <= n_side
        stats = jnp.zeros((tq, LANE), F32)
        for hh in range(hb):
            sl = slice(hh * HEAD_DIM, (hh + 1) * HEAD_DIM)
            q = q_ref[pl.ds(t0, tq), sl]
            k = k_ref[pl.ds(ws, kw), sl]
            v = v_ref[pl.ds(ws, kw), sl]
            s = lax.dot_general(q, k, (((1,), (1,)), ((), ())), preferred_element_type=F32)
            s = jnp.where(valid, s, -jnp.inf)
            m = jnp.max(s, axis=-1, keepdims=True)
            p = jnp.exp2(s - m)
            l = jnp.sum(p, axis=-1, keepdims=True)
            o = jnp.dot(p.astype(BF16), v, preferred_element_type=F32)
            o_ref[pl.ds(t0, tq), sl] = (o / l).astype(o_ref.dtype)
            stats = jnp.where(lane == hh, m, stats)
            stats = jnp.where(lane == hb + hh, l, stats)
        st_ref[pl.ds(t0, tq), :] = stats
        return carry

    lax.fori_loop(0, t_len // tq, tile, 0)


def _dil_heads_per_block(t_len):
    return DIL_HEADS if t_len <= 1024 else DIL_HEADS // 2


def _dilated_group(q3, k3, v3, dil, window):
    bsz, t_len, _ = q3.shape
    n_side = window // (2 * dil)
    hb = _dil_heads_per_block(t_len)
    nhb = DIL_HEADS // hb
    cw = hb * HEAD_DIM
    tq = min(2 * n_side, t_len)
    kw = min(4 * n_side, t_len)
    in_spec = pl.BlockSpec((None, t_len, cw), lambda b, r, hbi: (b, 0, r * nhb + hbi))

    o, st = pl.pallas_call(
        functools.partial(_dil_kernel, t_len=t_len, tq=tq, kw=kw, n_side=n_side, hb=hb),
        grid=(bsz, dil, nhb),
        in_specs=[in_spec, in_spec, in_spec],
        out_specs=[pl.BlockSpec((None, t_len, cw), lambda b, r, hbi: (b, 0, r * nhb + hbi)),
                   pl.BlockSpec((None, t_len, LANE), lambda b, r, hbi: (b, 0, r * nhb + hbi))],
        out_shape=[jax.ShapeDtypeStruct((bsz, t_len, dil * DIL_COLS), BF16),
                   jax.ShapeDtypeStruct((bsz, t_len, dil * nhb * LANE), F32)],
        compiler_params=_cparams("parallel", "parallel", "parallel"),
        name=f"dilated_d{dil}",
    )(q3, k3, v3)
    return o, st, hb


def _merge_kernel(*refs, hbs, dils):
    ng = len(hbs)
    o_refs, st_refs, y_ref = refs[:ng], refs[ng:2 * ng], refs[2 * ng]
    o_scr, st_scr = refs[2 * ng + 1:]
    tm = y_ref.shape[0]

    for g in range(ng):
        d = dils[g]
        if d == 1:
            continue
        rows = tm // d
        nblk = DIL_HEADS // hbs[g]
        for r in range(d):
            put = pl.ds(r, rows, stride=d)
            for hh in range(DIL_HEADS):
                lo = r * DIL_COLS + hh * HEAD_DIM
                o_scr[g, hh, put, :] = o_refs[g][:, lo:lo + HEAD_DIM].astype(F32)
            for blk in range(nblk):
                lo = (r * nblk + blk) * LANE
                st_scr[g, blk, put, :] = st_refs[g][:, lo:lo + LANE]

    def o_tile(g, h):
        if dils[g] == 1:
            return o_refs[g][:, h * HEAD_DIM:(h + 1) * HEAD_DIM].astype(F32)
        return o_scr[g, h]

    def st_tile(g, blk):
        if dils[g] == 1:
            return st_refs[g][:, blk * LANE:(blk + 1) * LANE]
        return st_scr[g, blk]

    for h in range(DIL_HEADS):
        ms, ls = [], []
        for g in range(ng):
            hb = hbs[g]
            st = st_tile(g, h // hb)
            lo = h % hb
            ms.append(st[:, lo:lo + 1])
            ls.append(st[:, lo + hb:lo + hb + 1])
        m_all = functools.reduce(jnp.maximum, ms)
        a = [jnp.exp2(ms[g] - m_all) * ls[g] for g in range(ng)]
        den = functools.reduce(lambda u, w: u + w, a)
        y = None
        for g in range(ng):
            term = (a[g] / den) * o_tile(g, h)
            y = term if y is None else y + term
        y_ref[:, h * HEAD_DIM:(h + 1) * HEAD_DIM] = y.astype(y_ref.dtype)


def _merge_groups(os_, sts, hbs, dils, seq):
    bsz = os_[0].shape[0]
    m = bsz * seq
    tm = _tile(seq, 512)
    per_b = seq // tm

    def spec(arr, d):
        return pl.BlockSpec((None, tm // d, arr.shape[2]), lambda i: (i // per_b, i % per_b, 0))

    in_specs = [spec(o, d) for o, d in zip(os_, dils)] + [spec(s, d) for s, d in zip(sts, dils)]
    ng = len(os_)
    max_blk = max(DIL_HEADS // hb for hb in hbs)
    return pl.pallas_call(
        functools.partial(_merge_kernel, hbs=tuple(hbs), dils=tuple(dils)),
        grid=(m // tm,),
        in_specs=in_specs,
        out_specs=pl.BlockSpec((tm, DIL_COLS), lambda i: (i, 0)),
        out_shape=jax.ShapeDtypeStruct((m, DIL_COLS), BF16),
        scratch_shapes=[pltpu.VMEM((ng, DIL_HEADS, tm, HEAD_DIM), F32),
                        pltpu.VMEM((ng, max_blk, tm, LANE), F32)],
        compiler_params=_cparams("parallel"),
        name="dil_merge",
    )(*os_, *sts)


def _mix_kernel(ya_ref, yb_ref, wa_ref, wb_ref, ga_ref, gb_ref, o_ref):
    pa = jnp.dot(ya_ref[...], wa_ref[...], preferred_element_type=F32)
    pb = jnp.dot(yb_ref[...], wb_ref[...], preferred_element_type=F32)
    o_ref[...] = (ga_ref[...].astype(F32) * pa + gb_ref[...].astype(F32) * pb).astype(o_ref.dtype)


def _mix(ya, yb, w_pa, w_pb, gates):
    m = ya.shape[0]
    d = w_pa.shape[1]
    tm = _tile(m, 1024)
    tn = _tile(d, 1024)
    ga0 = 0
    gb0 = d // tn
    return pl.pallas_call(
        _mix_kernel,
        grid=(m // tm, d // tn),
        in_specs=[pl.BlockSpec((tm, ya.shape[1]), lambda i, j: (i, 0)),
                  pl.BlockSpec((tm, yb.shape[1]), lambda i, j: (i, 0)),
                  pl.BlockSpec((w_pa.shape[0], tn), lambda i, j: (0, j)),
                  pl.BlockSpec((w_pb.shape[0], tn), lambda i, j: (0, j)),
                  pl.BlockSpec((tm, tn), lambda i, j: (i, ga0 + j)),
                  pl.BlockSpec((tm, tn), lambda i, j: (i, gb0 + j))],
        out_specs=pl.BlockSpec((tm, tn), lambda i, j: (i, j)),
        out_shape=jax.ShapeDtypeStruct((m, d), BF16),
        compiler_params=_cparams("parallel", "arbitrary"),
        name="mix",
    )(ya, yb, w_pa, w_pb, gates, gates)


def _outproj_kernel(a_ref, w_ref, x_ref, g_ref, o_ref):
    acc = jnp.dot(a_ref[...], w_ref[...], preferred_element_type=F32)
    o_ref[...] = x_ref[...] + g_ref[...] * acc


def _outproj(a, w, x2d, gate, seq):
    m, k = a.shape
    d = w.shape[1]
    bsz = m // seq
    tm = _tile(seq, 1024)
    tn = _tile(d, 512)
    per_b = seq // tm
    return pl.pallas_call(
        _outproj_kernel,
        grid=(m // tm, d // tn),
        in_specs=[pl.BlockSpec((tm, k), lambda i, j: (i, 0)),
                  pl.BlockSpec((k, tn), lambda i, j: (0, j)),
                  pl.BlockSpec((tm, tn), lambda i, j: (i, j)),
                  pl.BlockSpec((None, 1, tn), lambda i, j: (i // per_b, 0, j))],
        out_specs=pl.BlockSpec((tm, tn), lambda i, j: (i, j)),
        out_shape=jax.ShapeDtypeStruct((m, d), F32),
        compiler_params=_cparams("parallel", "arbitrary"),
        name="out_proj",
    )(a, w, x2d, gate.reshape(bsz, 1, d))


def _ffn_up_kernel(h_ref, wg_ref, wu_ref, o_ref):
    h = h_ref[...]
    g = jnp.dot(h, wg_ref[...], preferred_element_type=F32)
    u = jnp.dot(h, wu_ref[...], preferred_element_type=F32)
    o_ref[...] = (g * jax.nn.sigmoid(g) * u).astype(o_ref.dtype)


def _ffn_up(h, wg, wu):
    m, d = h.shape
    f = wg.shape[1]
    tm = _tile(m, 1024)
    tn = _tile(f, 512)
    return pl.pallas_call(
        _ffn_up_kernel,
        grid=(m // tm, f // tn),
        in_specs=[pl.BlockSpec((tm, d), lambda i, j: (i, 0)),
                  pl.BlockSpec((d, tn), lambda i, j: (0, j)),
                  pl.BlockSpec((d, tn), lambda i, j: (0, j))],
        out_specs=pl.BlockSpec((tm, tn), lambda i, j: (i, j)),
        out_shape=jax.ShapeDtypeStruct((m, f), BF16),
        compiler_params=_cparams("parallel", "arbitrary"),
        name="ffn_up",
    )(h, wg, wu)


def _ffn_down_kernel(a_ref, w_ref, x_ref, g_ref, fg_ref, o_ref, *, nk, final_norm):
    k = pl.program_id(1)

    @pl.when(k == 0)
    def _():
        o_ref[...] = jnp.dot(a_ref[...], w_ref[...], preferred_element_type=F32)

    @pl.when(k > 0)
    def _():
        o_ref[...] = jnp.dot(a_ref[...], w_ref[...], preferred_element_type=F32) + o_ref[...]

    @pl.when(k == nk - 1)
    def _():
        x2 = x_ref[...] + g_ref[...] * o_ref[...]
        o_ref[...] = _rms(x2, fg_ref[...]) if final_norm else x2


def _ffn_down(a, w, x2d, gate, final_g, seq, final_norm):
    m, f = a.shape
    d = w.shape[1]
    bsz = m // seq
    tm = _tile(seq, 512)
    tk = _tile(f, 1024)
    nk = f // tk
    per_b = seq // tm
    return pl.pallas_call(
        functools.partial(_ffn_down_kernel, nk=nk, final_norm=final_norm),
        grid=(m // tm, nk),
        in_specs=[pl.BlockSpec((tm, tk), lambda i, k: (i, k)),
                  pl.BlockSpec((tk, d), lambda i, k: (k, 0)),
                  pl.BlockSpec((tm, d), lambda i, k: (i, 0), pipeline_mode=pl.Buffered(1)),
                  pl.BlockSpec((None, 1, d), lambda i, k: (i // per_b, 0, 0)),
                  pl.BlockSpec((1, d), lambda i, k: (0, 0))],
        out_specs=pl.BlockSpec((tm, d), lambda i, k: (i, 0)),
        out_shape=jax.ShapeDtypeStruct((m, d), F32),
        compiler_params=_cparams("parallel", "arbitrary"),
        name="ffn_down",
    )(a, w, x2d, gate.reshape(bsz, 1, d), final_g.reshape(1, d))


def _cast_pad(w, axis, mult):
    wb = w.astype(BF16)
    pad = (-w.shape[axis]) % mult
    if not pad:
        return wb
    zshape = list(w.shape)
    zshape[axis] = pad
    return jnp.concatenate([wb, jnp.zeros(zshape, BF16)], axis=axis)


def _cast_pad_kernel(w_ref, o_ref, *, rows, cols):
    tr, tc = o_ref.shape
    r = pl.program_id(0) * tr + lax.broadcasted_iota(jnp.int32, (tr, tc), 0)
    c = pl.program_id(1) * tc + lax.broadcasted_iota(jnp.int32, (tr, tc), 1)
    inside = jnp.logical_and(r < rows, c < cols)
    o_ref[...] = jnp.where(inside, w_ref[...], 0.0).astype(o_ref.dtype)


def _cast_pad_big(w, row_mult, col_mult):
    rows, cols = w.shape
    prow = rows + (-rows) % row_mult
    pcol = cols + (-cols) % col_mult
    tr = _tile(prow, 512)
    tc = _tile(pcol, 1024)
    return pl.pallas_call(
        functools.partial(_cast_pad_kernel, rows=rows, cols=cols),
        grid=(prow // tr, pcol // tc),
        in_specs=[pl.BlockSpec((tr, tc), lambda i, j: (i, j))],
        out_specs=pl.BlockSpec((tr, tc), lambda i, j: (i, j)),
        out_shape=jax.ShapeDtypeStruct((prow, pcol), BF16),
        compiler_params=_cparams("parallel", "parallel"),
        name="cast_pad",
    )(w)


def _shift_cast_kernel(a_ref, b_ref, c_ref, o_ref):
    tr = o_ref.shape[0]
    upper = lax.broadcasted_iota(jnp.int32, (tr, LANE), 1) >= LANE // 2
    na, nb = a_ref.shape[1] // LANE, b_ref.shape[1] // LANE
    src = ([a_ref[:, k * LANE:(k + 1) * LANE] for k in range(na)]
           + [b_ref[:, k * LANE:(k + 1) * LANE] for k in range(nb)] + [c_ref[...]])
    for k in range(na + nb):
        o_ref[:, k * LANE:(k + 1) * LANE] = pltpu.roll(
            jnp.where(upper, src[k], src[k + 1]), LANE // 2, 1).astype(o_ref.dtype)


def _shift_cast(w, col0, ncols):
    rows = w.shape[0]
    half = DIL_COLS // 2
    base = col0 - LANE // 2
    assert base % half == 0 and ncols % DIL_COLS == 0
    tr = _tile(rows, 512)
    b0 = base // half
    c0 = (base + DIL_COLS) // LANE
    return pl.pallas_call(
        _shift_cast_kernel,
        grid=(rows // tr, ncols // DIL_COLS),
        in_specs=[pl.BlockSpec((tr, half), lambda i, t: (i, b0 + 2 * t)),
                  pl.BlockSpec((tr, half), lambda i, t: (i, b0 + 2 * t + 1)),
                  pl.BlockSpec((tr, LANE), lambda i, t: (i, c0 + (DIL_COLS // LANE) * t))],
        out_specs=pl.BlockSpec((tr, DIL_COLS), lambda i, t: (i, t)),
        out_shape=jax.ShapeDtypeStruct((rows, ncols), BF16),
        compiler_params=_cparams("parallel", "parallel"),
        name="shift_cast",
    )(w, w, w)


def kernel(x, c, positions, w_ada, b_ada, norm1_g, w_in, q_norm_g, w_uq, kv_norm_g, w_ukv,
           w_proj_a, w_proj_b, w_out, norm2_g, w_gate, w_up, w_down, final_g):
    bsz, seq, d = x.shape
    depth = w_ada.shape[0]
    m = bsz * seq
    cos_t, sin_t = _rope_tables(positions)
    x2d = x.reshape(m, d)

    for l in range(depth):
        mod = _ada(c, w_ada[l], b_ada[l])
        sh1, sc1, g1, sh2, sc2, g2 = jnp.split(mod, N_MOD, axis=-1)

        n_small = MLA_Q_RANK + MLA_KV_RANK + MLA_ROPE
        n_slab = DIL_GROUPS * DIL_COLS
        w_l = w_in[l]
        h = _prenorm(x2d, norm1_g[l], sh1, sc1, seq)
        z_s = _in_small(h, _cast_pad(w_l[:, :n_small], 1, LANE))
        w_dg = _shift_cast(w_l, n_small, w_l.shape[1] - n_small)
        gates = _gate_proj(h, w_dg, 3 * DIL_GROUPS, 2 * d)

        w_uq_pad = jnp.pad(w_uq[l].reshape(MLA_Q_RANK, MLA_HEADS, MLA_NOPE + MLA_ROPE),
                           ((0, 0), (0, 0), (0, MLA_QK_PAD - MLA_NOPE - MLA_ROPE)))
        w_uq_pad = w_uq_pad.reshape(MLA_Q_RANK, MLA_HEADS * MLA_QK_PAD).astype(BF16)
        q = _qup(z_s, q_norm_g[l], w_uq_pad, cos_t, sin_t)
        k, v = _kvup(z_s, kv_norm_g[l], w_ukv[l].astype(BF16), cos_t, sin_t)
        y_a = _mla_attention(q, k, v, bsz, seq)

        os_, sts, hbs, dils = [], [], [], []
        for g, (window, dil) in enumerate(DIL_PATTERNS):
            q3 = _dil_proj(h, w_dg, g, dil, seq, cos_t, sin_t, DIL_QSCALE)
            k3 = _dil_proj(h, w_dg, DIL_GROUPS + g, dil, seq, cos_t, sin_t)
            v3 = _dil_proj(h, w_dg, 2 * DIL_GROUPS + g, dil, seq)
            o_g, st_g, hb = _dilated_group(q3, k3, v3, dil, window)
            os_.append(o_g)
            sts.append(st_g)
            hbs.append(hb)
            dils.append(dil)
        y_b = _merge_groups(os_, sts, hbs, dils, seq)

        mixed = _mix(y_a, y_b, w_proj_a[l].astype(BF16), w_proj_b[l].astype(BF16), gates)
        x2d = _outproj(mixed, w_out[l].astype(BF16), x2d, g1, seq)

        h2 = _prenorm(x2d, norm2_g[l], sh2, sc2, seq)
        ff_mult = 1024
        wg = _cast_pad_big(w_gate[l], 1, ff_mult)
        wu = _cast_pad_big(w_up[l], 1, ff_mult)
        wd = _cast_pad_big(w_down[l], ff_mult, 1)
        hmid = _ffn_up(h2, wg, wu)
        x2d = _ffn_down(hmid, wd, x2d, g2, final_g, seq, final_norm=(l == depth - 1))

    return x2d.reshape(bsz, seq, d)
```

```python
import functools
import math

import jax
import jax.numpy as jnp
from jax import lax
from jax.experimental import pallas as pl
from jax.experimental.pallas import tpu as pltpu

F32 = jnp.float32
BF16 = jnp.bfloat16

NORM_EPS = 1e-6
ROPE_THETA = 10000.0
HEAD_DIM = 128
MLA_HEADS = 16
MLA_Q_RANK = 1024
MLA_KV_RANK = 512
MLA_NOPE = 128
MLA_ROPE = 64
MLA_V = 128
DIL_HEADS = 8
DIL_PATTERNS = ((128, 1), (512, 4), (2048, 16))
DIL_GROUPS = len(DIL_PATTERNS)
N_MOD = 6
LOG2E = 1.4426950408889634

LANE = 128
MLA_QK_PAD = 256
DIL_COLS = DIL_HEADS * HEAD_DIM
VMEM_LIMIT = 56 * 1024 * 1024

MLA_QSCALE = (MLA_NOPE + MLA_ROPE) ** -0.5 * LOG2E
DIL_QSCALE = HEAD_DIM ** -0.5 * LOG2E


def _cparams(*sem):
    return pltpu.CompilerParams(dimension_semantics=sem, vmem_limit_bytes=VMEM_LIMIT)


def _tile(n, pref):
    return pref if n % pref == 0 else n


def _rms(x, g):
    return x * lax.rsqrt(jnp.mean(x * x, axis=-1, keepdims=True) + NORM_EPS) * g


def _ada_kernel(c_ref, w_ref, b_ref, o_ref):
    c = c_ref[...]
    ca = (c * jax.nn.sigmoid(c)).astype(BF16)
    o_ref[...] = jnp.dot(ca, w_ref[...].astype(BF16), preferred_element_type=F32) + b_ref[...]


def _ada(c, w, b):
    bsz, d = c.shape
    n = w.shape[1]
    rows = 8
    cp = jnp.zeros((rows, d), F32).at[:bsz].set(c)
    tn = _tile(n, 512)
    out = pl.pallas_call(
        _ada_kernel,
        grid=(n // tn,),
        in_specs=[pl.BlockSpec((rows, d), lambda j: (0, 0)),
                  pl.BlockSpec((d, tn), lambda j: (0, j)),
                  pl.BlockSpec((1, tn), lambda j: (0, j))],
        out_specs=pl.BlockSpec((rows, tn), lambda j: (0, j)),
        out_shape=jax.ShapeDtypeStruct((rows, n), F32),
        compiler_params=_cparams("parallel"),
        name="ada",
    )(cp, w, b.reshape(1, n))
    return out[:bsz]


def _rope_tab_kernel(pos_ref, inv_ref, sgn_ref, msk_ref, cos_ref, sin_ref):
    ang = pos_ref[...].astype(F32) * inv_ref[...]
    cos_ref[...] = jnp.cos(ang) * msk_ref[...]
    sin_ref[...] = jnp.sin(ang) * sgn_ref[...]


def _rope_tables(positions):
    m = positions.size
    inv_f = ROPE_THETA ** (-jnp.arange(0, HEAD_DIM, 2, dtype=F32) / HEAD_DIM)
    inv_r = ROPE_THETA ** (-jnp.arange(0, MLA_ROPE, 2, dtype=F32) / MLA_ROPE)
    zpad = jnp.zeros((LANE - MLA_ROPE,), F32)
    hf, hr = HEAD_DIM // 2, MLA_ROPE // 2
    inv = jnp.concatenate([inv_f, inv_f, inv_r, inv_r, zpad]).reshape(1, 2 * LANE)
    sgn = jnp.concatenate([-jnp.ones(hf), jnp.ones(hf), -jnp.ones(hr), jnp.ones(hr), zpad]).astype(F32).reshape(1, 2 * LANE)
    msk = jnp.concatenate([jnp.ones(HEAD_DIM + MLA_ROPE), zpad]).astype(F32).reshape(1, 2 * LANE)
    tm = _tile(m, 1024)
    row = pl.BlockSpec((1, 2 * LANE), lambda i: (0, 0))
    tab = pl.BlockSpec((tm, 2 * LANE), lambda i: (i, 0))
    return pl.pallas_call(
        _rope_tab_kernel,
        grid=(m // tm,),
        in_specs=[pl.BlockSpec((tm, 1), lambda i: (i, 0)), row, row, row],
        out_specs=[tab, tab],
        out_shape=[jax.ShapeDtypeStruct((m, 2 * LANE), F32)] * 2,
        compiler_params=_cparams("parallel"),
        name="rope_tables",
    )(positions.reshape(m, 1), inv, sgn, msk)


def _prenorm_kernel(x_ref, g_ref, sh_ref, sc_ref, o_ref):
    y = _rms(x_ref[...], g_ref[...])
    o_ref[...] = (y * (1.0 + sc_ref[...]) + sh_ref[...]).astype(o_ref.dtype)


def _prenorm(x2d, g, shift, scale, seq):
    m, d = x2d.shape
    bsz = m // seq
    tm = _tile(seq, 256)
    per_b = seq // tm
    vec = pl.BlockSpec((None, 1, d), lambda i: (i // per_b, 0, 0))
    return pl.pallas_call(
        _prenorm_kernel,
        grid=(m // tm,),
        in_specs=[pl.BlockSpec((tm, d), lambda i: (i, 0)),
                  pl.BlockSpec((1, d), lambda i: (0, 0)), vec, vec],
        out_specs=pl.BlockSpec((tm, d), lambda i: (i, 0)),
        out_shape=jax.ShapeDtypeStruct((m, d), BF16),
        compiler_params=_cparams("parallel"),
        name="prenorm",
    )(x2d, g.reshape(1, d), shift.reshape(bsz, 1, d), scale.reshape(bsz, 1, d))


def _dot_nt(a, wt):
    return lax.dot_general(a, wt, (((1,), (1,)), ((), ())), preferred_element_type=F32)


def _mm_nt_kernel(a_ref, wt_ref, o_ref):
    o_ref[...] = _dot_nt(a_ref[...], wt_ref[...]).astype(o_ref.dtype)


def _in_small(h, wt_s):
    m, d = h.shape
    n = -(-N_SMALL // LANE) * LANE
    tm = _tile(m, 512)
    return pl.pallas_call(
        _mm_nt_kernel,
        grid=(m // tm,),
        in_specs=[pl.BlockSpec((tm, d), lambda i: (i, 0)),
                  pl.BlockSpec((n, d), lambda i: (0, 0))],
        out_specs=pl.BlockSpec((tm, n), lambda i: (i, 0)),
        out_shape=jax.ShapeDtypeStruct((m, n), F32),
        compiler_params=_cparams("parallel"),
        name="in_small",
    )(h, wt_s)


def _rope128(x, c, s):
    return x * c + pltpu.roll(x, HEAD_DIM // 2, 1) * s


def _rope_mla(x, c, s):
    half = MLA_ROPE // 2
    return x * c + (pltpu.roll(x, LANE - half, 1) + pltpu.roll(x, half, 1)) * s


N_SMALL = MLA_Q_RANK + MLA_KV_RANK + MLA_ROPE


def _wt_rows_spec(d, tile_of):
    def index_map(*ids):
        row = jnp.asarray(N_SMALL + DIL_COLS * tile_of(*ids), jnp.int32)
        return (pl.multiple_of(row, MLA_ROPE), 0)
    return pl.BlockSpec((pl.Element(DIL_COLS), pl.Element(d)), index_map)


def _dil_proj_kernel(*refs, rope, scale, dil):
    if rope:
        h_ref, w_ref, cos_ref, sin_ref, o_ref = refs[:5]
    else:
        h_ref, w_ref, o_ref = refs[:3]
    acc = _dot_nt(h_ref[...], w_ref[...])
    tm = acc.shape[0]

    def finish(y, c, s):
        if rope:
            y = _rope128(y, c, s)
            if scale != 1.0:
                y = y * scale
        return y.astype(o_ref.dtype)

    if dil == 1:
        c, s = (cos_ref[...], sin_ref[...]) if rope else (None, None)
        for hh in range(DIL_HEADS):
            sl = slice(hh * HEAD_DIM, (hh + 1) * HEAD_DIM)
            o_ref[:, sl] = finish(acc[:, sl], c, s)
        return

    scr_ref = refs[-1]
    rows = tm // dil
    for hh in range(DIL_HEADS):
        scr_ref[hh] = acc[:, hh * HEAD_DIM:(hh + 1) * HEAD_DIM]
    for r in range(dil):
        take = pl.ds(r, rows, stride=dil)
        c, s = (cos_ref[take, :], sin_ref[take, :]) if rope else (None, None)
        for hh in range(DIL_HEADS):
            lo = r * DIL_COLS + hh * HEAD_DIM
            o_ref[:, lo:lo + HEAD_DIM] = finish(scr_ref[hh, take, :], c, s)


def _dil_proj(h, wt_dg, col_tile, dil, seq, cos_t=None, sin_t=None, scale=1.0):
    m, d = h.shape
    bsz = m // seq
    tm = _tile(seq, 1024)
    per_b = seq // tm
    rope = cos_t is not None
    in_specs = [pl.BlockSpec((tm, d), lambda i: (i, 0)),
                _wt_rows_spec(d, lambda i: col_tile)]
    args = [h, wt_dg]
    if rope:
        in_specs += [pl.BlockSpec((tm, LANE), lambda i: (i, 0))] * 2
        args += [cos_t, sin_t]
    scratch = [pltpu.VMEM((DIL_HEADS, tm, HEAD_DIM), F32)] if dil > 1 else []
    return pl.pallas_call(
        functools.partial(_dil_proj_kernel, rope=rope, scale=scale, dil=dil),
        grid=(m // tm,),
        in_specs=in_specs,
        out_specs=pl.BlockSpec((None, tm // dil, dil * DIL_COLS), lambda i: (i // per_b, i % per_b, 0)),
        out_shape=jax.ShapeDtypeStruct((bsz, seq // dil, dil * DIL_COLS), BF16),
        scratch_shapes=scratch,
        compiler_params=_cparams("parallel"),
        name=f"dil_proj_d{dil}",
    )(*args)


def _proj_sigmoid_kernel(h_ref, w_ref, o_ref):
    acc = _dot_nt(h_ref[...], w_ref[...])
    o_ref[...] = jax.nn.sigmoid(acc).astype(o_ref.dtype)


def _gate_proj(h, wt_dg, col_tile0, n):
    m, d = h.shape
    tm = _tile(m, 1024)
    tn = DIL_COLS
    return pl.pallas_call(
        _proj_sigmoid_kernel,
        grid=(n // tn, m // tm),
        in_specs=[pl.BlockSpec((tm, d), lambda j, i: (i, 0)),
                  _wt_rows_spec(d, lambda j, i: col_tile0 + j)],
        out_specs=pl.BlockSpec((tm, tn), lambda j, i: (i, j)),
        out_shape=jax.ShapeDtypeStruct((m, n), BF16),
        compiler_params=_cparams("parallel", "arbitrary"),
        name="gate_proj",
    )(h, wt_dg)


def _qup_kernel(z_ref, g_ref, w_ref, cos_ref, sin_ref, o_ref, zn_ref):
    @pl.when(pl.program_id(1) == 0)
    def _():
        zn_ref[...] = _rms(z_ref[...], g_ref[...]).astype(BF16)

    acc = jnp.dot(zn_ref[...], w_ref[...], preferred_element_type=F32)
    c = cos_ref[...]
    s = sin_ref[...]
    for hh in range(acc.shape[1] // MLA_QK_PAD):
        lo = hh * MLA_QK_PAD
        o_ref[:, lo:lo + LANE] = (acc[:, lo:lo + LANE] * MLA_QSCALE).astype(o_ref.dtype)
        rp = _rope_mla(acc[:, lo + LANE:lo + 2 * LANE], c, s)
        o_ref[:, lo + LANE:lo + 2 * LANE] = (rp * MLA_QSCALE).astype(o_ref.dtype)


def _qup(z_s, g, w_uq_pad, cos_t, sin_t):
    m = z_s.shape[0]
    k = MLA_Q_RANK
    n = w_uq_pad.shape[1]
    tm = _tile(m, 1024)
    tn = _tile(n, 1024)
    return pl.pallas_call(
        _qup_kernel,
        grid=(m // tm, n // tn),
        in_specs=[pl.BlockSpec((tm, k), lambda i, j: (i, 0)),
                  pl.BlockSpec((1, k), lambda i, j: (0, 0)),
                  pl.BlockSpec((k, tn), lambda i, j: (0, j)),
                  pl.BlockSpec((tm, LANE), lambda i, j: (i, 1)),
                  pl.BlockSpec((tm, LANE), lambda i, j: (i, 1))],
        out_specs=pl.BlockSpec((tm, tn), lambda i, j: (i, j)),
        out_shape=jax.ShapeDtypeStruct((m, n), BF16),
        scratch_shapes=[pltpu.VMEM((tm, k), BF16)],
        compiler_params=_cparams("parallel", "arbitrary"),
        name="q_up",
    )(z_s, g.reshape(1, k), w_uq_pad, cos_t, sin_t)


def _kvup_kernel(z_ref, kr_ref, g_ref, w_ref, cos_ref, sin_ref, k_ref, v_ref, zn_ref):
    @pl.when(pl.program_id(1) == 0)
    def _():
        zn_ref[...] = _rms(z_ref[...], g_ref[...]).astype(BF16)

    acc = jnp.dot(zn_ref[...], w_ref[...], preferred_element_type=F32)
    kr = kr_ref[...]
    kr = jnp.where(lax.broadcasted_iota(jnp.int32, kr.shape, 1) < MLA_ROPE, kr, 0.0)
    kr = _rope_mla(kr, cos_ref[...], sin_ref[...]).astype(k_ref.dtype)
    for hh in range(acc.shape[1] // (MLA_NOPE + MLA_V)):
        lo = hh * (MLA_NOPE + MLA_V)
        ko = hh * MLA_QK_PAD
        k_ref[:, ko:ko + LANE] = acc[:, lo:lo + MLA_NOPE].astype(k_ref.dtype)
        k_ref[:, ko + LANE:ko + 2 * LANE] = kr
        v_ref[:, hh * MLA_V:(hh + 1) * MLA_V] = acc[:, lo + MLA_NOPE:lo + MLA_NOPE + MLA_V].astype(v_ref.dtype)


def _kvup(z_s, g, w_ukv, cos_t, sin_t):
    m = z_s.shape[0]
    k = MLA_KV_RANK
    n = w_ukv.shape[1]
    tm = _tile(m, 1024)
    tn = _tile(n, 1024)
    nh = tn // (MLA_NOPE + MLA_V)
    kv_blk = MLA_Q_RANK // MLA_KV_RANK
    kr_blk = (MLA_Q_RANK + MLA_KV_RANK) // LANE
    return pl.pallas_call(
        _kvup_kernel,
        grid=(m // tm, n // tn),
        in_specs=[pl.BlockSpec((tm, k), lambda i, j: (i, kv_blk)),
                  pl.BlockSpec((tm, LANE), lambda i, j: (i, kr_blk)),
                  pl.BlockSpec((1, k), lambda i, j: (0, 0)),
                  pl.BlockSpec((k, tn), lambda i, j: (0, j)),
                  pl.BlockSpec((tm, LANE), lambda i, j: (i, 1)),
                  pl.BlockSpec((tm, LANE), lambda i, j: (i, 1))],
        out_specs=[pl.BlockSpec((tm, nh * MLA_QK_PAD), lambda i, j: (i, j)),
                   pl.BlockSpec((tm, nh * MLA_V), lambda i, j: (i, j))],
        out_shape=[jax.ShapeDtypeStruct((m, MLA_HEADS * MLA_QK_PAD), BF16),
                   jax.ShapeDtypeStruct((m, MLA_HEADS * MLA_V), BF16)],
        scratch_shapes=[pltpu.VMEM((tm, k), BF16)],
        compiler_params=_cparams("parallel", "arbitrary"),
        name="kv_up",
    )(z_s, z_s, g.reshape(1, k), w_ukv, cos_t, sin_t)


def _mla_kernel(q_ref, k_ref, v_ref, o_ref, *, sub):
    for r0 in range(0, q_ref.shape[0], sub):
        s = lax.dot_general(q_ref[r0:r0 + sub, :], k_ref[...], (((1,), (1,)), ((), ())),
                            preferred_element_type=F32)
        m = jnp.max(s, axis=-1, keepdims=True)
        p = jnp.exp2(s - m)
        l = jnp.sum(p, axis=-1, keepdims=True)
        o = jnp.dot(p.astype(BF16), v_ref[...], preferred_element_type=F32)
        o_ref[r0:r0 + sub, :] = (o / l).astype(o_ref.dtype)


def _mla_attention(q, k, v, bsz, seq):
    q3 = q.reshape(bsz, seq, MLA_HEADS * MLA_QK_PAD)
    k3 = k.reshape(bsz, seq, MLA_HEADS * MLA_QK_PAD)
    v3 = v.reshape(bsz, seq, MLA_HEADS * MLA_V)
    tq = _tile(seq, 1024)
    out = pl.pallas_call(
        functools.partial(_mla_kernel, sub=min(tq, 256)),
        grid=(bsz, MLA_HEADS, seq // tq),
        in_specs=[pl.BlockSpec((None, tq, MLA_QK_PAD), lambda b, h, i: (b, i, h)),
                  pl.BlockSpec((None, seq, MLA_QK_PAD), lambda b, h, i: (b, 0, h)),
                  pl.BlockSpec((None, seq, MLA_V), lambda b, h, i: (b, 0, h))],
        out_specs=pl.BlockSpec((None, tq, MLA_V), lambda b, h, i: (b, i, h)),
        out_shape=jax.ShapeDtypeStruct((bsz, seq, MLA_HEADS * MLA_V), BF16),
        compiler_params=_cparams("parallel", "parallel", "arbitrary"),
        name="mla_attn",
    )(q3, k3, v3)
    return out.reshape(bsz * seq, MLA_HEADS * MLA_V)


def _dil_kernel(q_ref, k_ref, v_ref, o_ref, st_ref, *, t_len, tq, kw, n_side, hb):
    lane = lax.broadcasted_iota(jnp.int32, (tq, LANE), 1)
    head0 = pl.program_id(2) * hb
    row = lax.broadcasted_iota(jnp.int32, (tq, kw), 0)
    col = lax.broadcasted_iota(jnp.int32, (tq, kw), 1)

    def tile(qi, carry):
        t0 = pl.multiple_of(qi * tq, tq)
        ws = jnp.clip(t0 - n_side, 0, t_len - kw)
        ws = pl.multiple_of(ws, n_side) if t_len > kw else 0
        valid = jnp.abs((t0 + row) - (ws + col)) <= n_side
        stats = jnp.zeros((tq, LANE), F32)
        for hh in range(hb):
            sl = slice(hh * HEAD_DIM, (hh + 1) * HEAD_DIM)
            q = q_ref[pl.ds(t0, tq), sl]
            k = k_ref[pl.ds(ws, kw), sl]
            v = v_ref[pl.ds(ws, kw), sl]
            s = lax.dot_general(q, k, (((1,), (1,)), ((), ())), preferred_element_type=F32)
            s = jnp.where(valid, s, -jnp.inf)
            m = jnp.max(s, axis=-1, keepdims=True)
            p = jnp.exp2(s - m)
            l = jnp.sum(p, axis=-1, keepdims=True)
            o = jnp.dot(p.astype(BF16), v, preferred_element_type=F32)
            o_ref[pl.ds(t0, tq), sl] = (o / l).astype(o_ref.dtype)
            stats = jnp.where(lane == head0 + hh, m, stats)
            stats = jnp.where(lane == head0 + (DIL_HEADS + hh), l, stats)
        st_ref[pl.ds(t0, tq), :] = stats
        return carry

    lax.fori_loop(0, t_len // tq, tile, 0)


def _dil_heads_per_block(t_len):
    return DIL_HEADS if t_len <= 1024 else DIL_HEADS // 2


def _dilated_group(q3, k3, v3, dil, window):
    bsz, t_len, _ = q3.shape
    n_side = window // (2 * dil)
    hb = _dil_heads_per_block(t_len)
    nhb = DIL_HEADS // hb
    cw = hb * HEAD_DIM
    tq = min(2 * n_side, t_len)
    kw = min(4 * n_side, t_len)
    in_spec = pl.BlockSpec((None, t_len, cw), lambda b, r, hbi: (b, 0, r * nhb + hbi))

    o, st = pl.pallas_call(
        functools.partial(_dil_kernel, t_len=t_len, tq=tq, kw=kw, n_side=n_side, hb=hb),
        grid=(bsz, dil, nhb),
        in_specs=[in_spec, in_spec, in_spec],
        out_specs=[pl.BlockSpec((None, t_len, cw), lambda b, r, hbi: (b, 0, r * nhb + hbi)),
                   pl.BlockSpec((None, t_len, LANE), lambda b, r, hbi: (b, 0, r * nhb + hbi))],
        out_shape=[jax.ShapeDtypeStruct((bsz, t_len, dil * DIL_COLS), BF16),
                   jax.ShapeDtypeStruct((bsz, t_len, dil * nhb * LANE), F32)],
        compiler_params=_cparams("parallel", "parallel", "parallel"),
        name=f"dilated_d{dil}",
    )(q3, k3, v3)
    return o, st, hb


def _merge_kernel(*refs, hbs, dils):
    ng = len(hbs)
    o_refs, st_refs, y_ref = refs[:ng], refs[ng:2 * ng], refs[2 * ng]
    o_scr, st_scr = refs[2 * ng + 1:]
    tm = y_ref.shape[0]

    for g in range(ng):
        d = dils[g]
        if d == 1:
            continue
        rows = tm // d
        nblk = DIL_HEADS // hbs[g]
        for r in range(d):
            put = pl.ds(r, rows, stride=d)
            for hh in range(DIL_HEADS):
                lo = r * DIL_COLS + hh * HEAD_DIM
                o_scr[g, hh, put, :] = o_refs[g][:, lo:lo + HEAD_DIM].astype(F32)
            for blk in range(nblk):
                lo = (r * nblk + blk) * LANE
                st_scr[g, blk, put, :] = st_refs[g][:, lo:lo + LANE]

    def o_tile(g, h):
        if dils[g] == 1:
            return o_refs[g][:, h * HEAD_DIM:(h + 1) * HEAD_DIM].astype(F32)
        return o_scr[g, h]

    def st_tile(g):
        nblk = DIL_HEADS // hbs[g]
        if dils[g] == 1:
            tiles = [st_refs[g][:, blk * LANE:(blk + 1) * LANE] for blk in range(nblk)]
        else:
            tiles = [st_scr[g, blk] for blk in range(nblk)]
        return functools.reduce(lambda u, w: u + w, tiles)

    sts = [st_tile(g) for g in range(ng)]
    m_all = functools.reduce(jnp.maximum, sts)
    a = [jnp.exp2(sts[g] - m_all) * pltpu.roll(sts[g], LANE - DIL_HEADS, 1) for g in range(ng)]
    den = functools.reduce(lambda u, w: u + w, a)
    head_lane = lax.broadcasted_iota(jnp.int32, (tm, LANE), 1) < DIL_HEADS
    spread = (lax.broadcasted_iota(jnp.int32, (LANE, DIL_COLS), 0)
              == jnp.right_shift(lax.broadcasted_iota(jnp.int32, (LANE, DIL_COLS), 1),
                                 HEAD_DIM.bit_length() - 1)).astype(BF16)
    coef = []
    for g in range(ng):
        cg = jnp.where(head_lane, a[g] / den, 0.0)
        hi = cg.astype(BF16)
        lo = (cg - hi.astype(F32)).astype(BF16)
        coef.append(jnp.dot(hi, spread, preferred_element_type=F32)
                    + jnp.dot(lo, spread, preferred_element_type=F32))
    for h in range(DIL_HEADS):
        sl = slice(h * HEAD_DIM, (h + 1) * HEAD_DIM)
        y = None
        for g in range(ng):
            term = coef[g][:, sl] * o_tile(g, h)
            y = term if y is None else y + term
        y_ref[:, sl] = y.astype(y_ref.dtype)


def _merge_groups(os_, sts, hbs, dils, seq):
    bsz = os_[0].shape[0]
    m = bsz * seq
    tm = _tile(seq, 512)
    per_b = seq // tm

    def spec(arr, d):
        return pl.BlockSpec((None, tm // d, arr.shape[2]), lambda i: (i // per_b, i % per_b, 0))

    in_specs = [spec(o, d) for o, d in zip(os_, dils)] + [spec(s, d) for s, d in zip(sts, dils)]
    ng = len(os_)
    max_blk = max(DIL_HEADS // hb for hb in hbs)
    return pl.pallas_call(
        functools.partial(_merge_kernel, hbs=tuple(hbs), dils=tuple(dils)),
        grid=(m // tm,),
        in_specs=in_specs,
        out_specs=pl.BlockSpec((tm, DIL_COLS), lambda i: (i, 0)),
        out_shape=jax.ShapeDtypeStruct((m, DIL_COLS), BF16),
        scratch_shapes=[pltpu.VMEM((ng, DIL_HEADS, tm, HEAD_DIM), F32),
                        pltpu.VMEM((ng, max_blk, tm, LANE), F32)],
        compiler_params=_cparams("parallel"),
        name="dil_merge",
    )(*os_, *sts)


def _mix_kernel(ya_ref, yb_ref, wa_ref, wb_ref, ga_ref, gb_ref, o_ref):
    pa = jnp.dot(ya_ref[...], wa_ref[...], preferred_element_type=F32)
    pb = jnp.dot(yb_ref[...], wb_ref[...], preferred_element_type=F32)
    o_ref[...] = (ga_ref[...].astype(F32) * pa + gb_ref[...].astype(F32) * pb).astype(o_ref.dtype)


def _mix(ya, yb, w_pa, w_pb, gates):
    m = ya.shape[0]
    d = w_pa.shape[1]
    tm = _tile(m, 1024)
    tn = _tile(d, 1024)
    ga0 = 0
    gb0 = d // tn
    return pl.pallas_call(
        _mix_kernel,
        grid=(m // tm, d // tn),
        in_specs=[pl.BlockSpec((tm, ya.shape[1]), lambda i, j: (i, 0)),
                  pl.BlockSpec((tm, yb.shape[1]), lambda i, j: (i, 0)),
                  pl.BlockSpec((w_pa.shape[0], tn), lambda i, j: (0, j)),
                  pl.BlockSpec((w_pb.shape[0], tn), lambda i, j: (0, j)),
                  pl.BlockSpec((tm, tn), lambda i, j: (i, ga0 + j)),
                  pl.BlockSpec((tm, tn), lambda i, j: (i, gb0 + j))],
        out_specs=pl.BlockSpec((tm, tn), lambda i, j: (i, j)),
        out_shape=jax.ShapeDtypeStruct((m, d), BF16),
        compiler_params=_cparams("parallel", "arbitrary"),
        name="mix",
    )(ya, yb, w_pa, w_pb, gates, gates)


def _outproj_kernel(a_ref, w_ref, x_ref, g_ref, o_ref):
    acc = jnp.dot(a_ref[...], w_ref[...], preferred_element_type=F32)
    o_ref[...] = x_ref[...] + g_ref[...] * acc


def _outproj(a, w, x2d, gate, seq):
    m, k = a.shape
    d = w.shape[1]
    bsz = m // seq
    tm = _tile(seq, 1024)
    tn = _tile(d, 512)
    per_b = seq // tm
    return pl.pallas_call(
        _outproj_kernel,
        grid=(m // tm, d // tn),
        in_specs=[pl.BlockSpec((tm, k), lambda i, j: (i, 0)),
                  pl.BlockSpec((k, tn), lambda i, j: (0, j)),
                  pl.BlockSpec((tm, tn), lambda i, j: (i, j)),
                  pl.BlockSpec((None, 1, tn), lambda i, j: (i // per_b, 0, j))],
        out_specs=pl.BlockSpec((tm, tn), lambda i, j: (i, j)),
        out_shape=jax.ShapeDtypeStruct((m, d), F32),
        compiler_params=_cparams("parallel", "arbitrary"),
        name="out_proj",
    )(a, w, x2d, gate.reshape(bsz, 1, d))


def _ffn_up_kernel(h_ref, wg_ref, wu_ref, o_ref):
    h = h_ref[...]
    g = jnp.dot(h, wg_ref[...], preferred_element_type=F32)
    u = jnp.dot(h, wu_ref[...], preferred_element_type=F32)
    o_ref[...] = (g * jax.nn.sigmoid(g) * u).astype(o_ref.dtype)


def _ffn_up(h, wg, wu):
    m, d = h.shape
    f = wg.shape[1]
    tm = _tile(m, 1024)
    tn = _tile(f, 512)
    return pl.pallas_call(
        _ffn_up_kernel,
        grid=(m // tm, f // tn),
        in_specs=[pl.BlockSpec((tm, d), lambda i, j: (i, 0)),
                  pl.BlockSpec((d, tn), lambda i, j: (0, j)),
                  pl.BlockSpec((d, tn), lambda i, j: (0, j))],
        out_specs=pl.BlockSpec((tm, tn), lambda i, j: (i, j)),
        out_shape=jax.ShapeDtypeStruct((m, f), BF16),
        compiler_params=_cparams("parallel", "arbitrary"),
        name="ffn_up",
    )(h, wg, wu)


def _ffn_down_kernel(a_ref, w_ref, x_ref, g_ref, fg_ref, o_ref, *, nk, final_norm):
    k = pl.program_id(1)

    @pl.when(k == 0)
    def _():
        o_ref[...] = jnp.dot(a_ref[...], w_ref[...], preferred_element_type=F32)

    @pl.when(k > 0)
    def _():
        o_ref[...] = jnp.dot(a_ref[...], w_ref[...], preferred_element_type=F32) + o_ref[...]

    @pl.when(k == nk - 1)
    def _():
        x2 = x_ref[...] + g_ref[...] * o_ref[...]
        o_ref[...] = _rms(x2, fg_ref[...]) if final_norm else x2


def _ffn_down(a, w, x2d, gate, final_g, seq, final_norm):
    m, f = a.shape
    d = w.shape[1]
    bsz = m // seq
    tm = _tile(seq, 512)
    tk = _tile(f, 1024)
    nk = f // tk
    per_b = seq // tm
    return pl.pallas_call(
        functools.partial(_ffn_down_kernel, nk=nk, final_norm=final_norm),
        grid=(m // tm, nk),
        in_specs=[pl.BlockSpec((tm, tk), lambda i, k: (i, k)),
                  pl.BlockSpec((tk, d), lambda i, k: (k, 0)),
                  pl.BlockSpec((tm, d), lambda i, k: (i, 0), pipeline_mode=pl.Buffered(1)),
                  pl.BlockSpec((None, 1, d), lambda i, k: (i // per_b, 0, 0)),
                  pl.BlockSpec((1, d), lambda i, k: (0, 0))],
        out_specs=pl.BlockSpec((tm, d), lambda i, k: (i, 0)),
        out_shape=jax.ShapeDtypeStruct((m, d), F32),
        compiler_params=_cparams("parallel", "arbitrary"),
        name="ffn_down",
    )(a, w, x2d, gate.reshape(bsz, 1, d), final_g.reshape(1, d))


def _cast_pad_kernel(w_ref, o_ref, *, rows, cols):
    tr, tc = o_ref.shape
    r = pl.program_id(0) * tr + lax.broadcasted_iota(jnp.int32, (tr, tc), 0)
    c = pl.program_id(1) * tc + lax.broadcasted_iota(jnp.int32, (tr, tc), 1)
    inside = jnp.logical_and(r < rows, c < cols)
    o_ref[...] = jnp.where(inside, w_ref[...], 0.0).astype(o_ref.dtype)


def _cast_pad_big(w, row_mult, col_mult):
    rows, cols = w.shape
    prow = rows + (-rows) % row_mult
    pcol = cols + (-cols) % col_mult
    tr = _tile(prow, 1024)
    tc = _tile(pcol, 2816) if pcol > 4096 else _tile(pcol, 2048)
    return pl.pallas_call(
        functools.partial(_cast_pad_kernel, rows=rows, cols=cols),
        grid=(prow // tr, pcol // tc),
        in_specs=[pl.BlockSpec((tr, tc), lambda i, j: (i, j))],
        out_specs=pl.BlockSpec((tr, tc), lambda i, j: (i, j)),
        out_shape=jax.ShapeDtypeStruct((prow, pcol), BF16),
        compiler_params=_cparams("parallel", "parallel"),
        name="cast_pad",
    )(w)


def kernel(x, c, positions, w_ada, b_ada, norm1_g, w_in, q_norm_g, w_uq, kv_norm_g, w_ukv,
           w_proj_a, w_proj_b, w_out, norm2_g, w_gate, w_up, w_down, final_g):
    bsz, seq, d = x.shape
    depth = w_ada.shape[0]
    m = bsz * seq
    cos_t, sin_t = _rope_tables(positions)
    x2d = x.reshape(m, d)

    for l in range(depth):
        mod = _ada(c, w_ada[l], b_ada[l])
        sh1, sc1, g1, sh2, sc2, g2 = jnp.split(mod, N_MOD, axis=-1)

        w_dg = w_in[l].T.astype(BF16)
        h = _prenorm(x2d, norm1_g[l], sh1, sc1, seq)
        z_s = _in_small(h, w_dg)
        gates = _gate_proj(h, w_dg, 3 * DIL_GROUPS, 2 * d)

        w_uq_pad = jnp.pad(w_uq[l].reshape(MLA_Q_RANK, MLA_HEADS, MLA_NOPE + MLA_ROPE),
                           ((0, 0), (0, 0), (0, MLA_QK_PAD - MLA_NOPE - MLA_ROPE)))
        w_uq_pad = w_uq_pad.reshape(MLA_Q_RANK, MLA_HEADS * MLA_QK_PAD).astype(BF16)
        q = _qup(z_s, q_norm_g[l], w_uq_pad, cos_t, sin_t)
        k, v = _kvup(z_s, kv_norm_g[l], w_ukv[l].astype(BF16), cos_t, sin_t)
        y_a = _mla_attention(q, k, v, bsz, seq)

        os_, sts, hbs, dils = [], [], [], []
        for g, (window, dil) in enumerate(DIL_PATTERNS):
            q3 = _dil_proj(h, w_dg, g, dil, seq, cos_t, sin_t, DIL_QSCALE)
            k3 = _dil_proj(h, w_dg, DIL_GROUPS + g, dil, seq, cos_t, sin_t)
            v3 = _dil_proj(h, w_dg, 2 * DIL_GROUPS + g, dil, seq)
            o_g, st_g, hb = _dilated_group(q3, k3, v3, dil, window)
            os_.append(o_g)
            sts.append(st_g)
            hbs.append(hb)
            dils.append(dil)
        y_b = _merge_groups(os_, sts, hbs, dils, seq)

        mixed = _mix(y_a, y_b, w_proj_a[l].astype(BF16), w_proj_b[l].astype(BF16), gates)
        x2d = _outproj(mixed, w_out[l].astype(BF16), x2d, g1, seq)

        h2 = _prenorm(x2d, norm2_g[l], sh2, sc2, seq)
        ff_mult = 1024
        wg = _cast_pad_big(w_gate[l], 1, ff_mult)
        wu = _cast_pad_big(w_up[l], 1, ff_mult)
        wd = _cast_pad_big(w_down[l], ff_mult, 1)
        hmid = _ffn_up(h2, wg, wu)
        x2d = _ffn_down(hmid, wd, x2d, g2, final_g, seq, final_norm=(l == depth - 1))

    return x2d.reshape(bsz, seq, d)
```

```python
import functools
import math

import jax
import jax.numpy as jnp
from jax import lax
from jax.experimental import pallas as pl
from jax.experimental.pallas import tpu as pltpu

F32 = jnp.float32
BF16 = jnp.bfloat16

NORM_EPS = 1e-6
ROPE_THETA = 10000.0
HEAD_DIM = 128
MLA_HEADS = 16
MLA_Q_RANK = 1024
MLA_KV_RANK = 512
MLA_NOPE = 128
MLA_ROPE = 64
MLA_V = 128
DIL_HEADS = 8
DIL_PATTERNS = ((128, 1), (512, 4), (2048, 16))
DIL_GROUPS = len(DIL_PATTERNS)
N_MOD = 6
LOG2E = 1.4426950408889634

LANE = 128
MLA_QK_PAD = 256
DIL_COLS = DIL_HEADS * HEAD_DIM
VMEM_LIMIT = 56 * 1024 * 1024

MLA_QSCALE = (MLA_NOPE + MLA_ROPE) ** -0.5 * LOG2E
DIL_QSCALE = HEAD_DIM ** -0.5 * LOG2E


def _cparams(*sem):
    return pltpu.CompilerParams(dimension_semantics=sem, vmem_limit_bytes=VMEM_LIMIT)


def _tile(n, pref):
    return pref if n % pref == 0 else n


def _rms(x, g):
    return x * lax.rsqrt(jnp.mean(x * x, axis=-1, keepdims=True) + NORM_EPS) * g


def _ada_kernel(c_ref, w_ref, b_ref, o_ref):
    c = c_ref[...]
    ca = (c * jax.nn.sigmoid(c)).astype(BF16)
    o_ref[...] = jnp.dot(ca, w_ref[...].astype(BF16), preferred_element_type=F32) + b_ref[...]


def _ada(c, w, b):
    bsz, d = c.shape
    n = w.shape[1]
    rows = 8
    cp = jnp.zeros((rows, d), F32).at[:bsz].set(c)
    tn = _tile(n, 512)
    out = pl.pallas_call(
        _ada_kernel,
        grid=(n // tn,),
        in_specs=[pl.BlockSpec((rows, d), lambda j: (0, 0)),
                  pl.BlockSpec((d, tn), lambda j: (0, j)),
                  pl.BlockSpec((1, tn), lambda j: (0, j))],
        out_specs=pl.BlockSpec((rows, tn), lambda j: (0, j)),
        out_shape=jax.ShapeDtypeStruct((rows, n), F32),
        compiler_params=_cparams("parallel"),
        name="ada",
    )(cp, w, b.reshape(1, n))
    return out[:bsz]


def _rope_tab_kernel(pos_ref, inv_ref, sgn_ref, msk_ref, cos_ref, sin_ref):
    ang = pos_ref[...].astype(F32) * inv_ref[...]
    cos_ref[...] = jnp.cos(ang) * msk_ref[...]
    sin_ref[...] = jnp.sin(ang) * sgn_ref[...]


def _rope_tables(positions):
    m = positions.size
    inv_f = ROPE_THETA ** (-jnp.arange(0, HEAD_DIM, 2, dtype=F32) / HEAD_DIM)
    inv_r = ROPE_THETA ** (-jnp.arange(0, MLA_ROPE, 2, dtype=F32) / MLA_ROPE)
    zpad = jnp.zeros((LANE - MLA_ROPE,), F32)
    hf, hr = HEAD_DIM // 2, MLA_ROPE // 2
    inv = jnp.concatenate([inv_f, inv_f, inv_r, inv_r, zpad]).reshape(1, 2 * LANE)
    sgn = jnp.concatenate([-jnp.ones(hf), jnp.ones(hf), -jnp.ones(hr), jnp.ones(hr), zpad]).astype(F32).reshape(1, 2 * LANE)
    msk = jnp.concatenate([jnp.ones(HEAD_DIM + MLA_ROPE), zpad]).astype(F32).reshape(1, 2 * LANE)
    tm = _tile(m, 1024)
    row = pl.BlockSpec((1, 2 * LANE), lambda i: (0, 0))
    tab = pl.BlockSpec((tm, 2 * LANE), lambda i: (i, 0))
    return pl.pallas_call(
        _rope_tab_kernel,
        grid=(m // tm,),
        in_specs=[pl.BlockSpec((tm, 1), lambda i: (i, 0)), row, row, row],
        out_specs=[tab, tab],
        out_shape=[jax.ShapeDtypeStruct((m, 2 * LANE), F32)] * 2,
        compiler_params=_cparams("parallel"),
        name="rope_tables",
    )(positions.reshape(m, 1), inv, sgn, msk)


def _prenorm_kernel(x_ref, g_ref, sh_ref, sc_ref, o_ref):
    y = _rms(x_ref[...], g_ref[...])
    o_ref[...] = (y * (1.0 + sc_ref[...]) + sh_ref[...]).astype(o_ref.dtype)


def _prenorm(x2d, g, shift, scale, seq):
    m, d = x2d.shape
    bsz = m // seq
    tm = _tile(seq, 256)
    per_b = seq // tm
    vec = pl.BlockSpec((None, 1, d), lambda i: (i // per_b, 0, 0))
    return pl.pallas_call(
        _prenorm_kernel,
        grid=(m // tm,),
        in_specs=[pl.BlockSpec((tm, d), lambda i: (i, 0)),
                  pl.BlockSpec((1, d), lambda i: (0, 0)), vec, vec],
        out_specs=pl.BlockSpec((tm, d), lambda i: (i, 0)),
        out_shape=jax.ShapeDtypeStruct((m, d), BF16),
        compiler_params=_cparams("parallel"),
        name="prenorm",
    )(x2d, g.reshape(1, d), shift.reshape(bsz, 1, d), scale.reshape(bsz, 1, d))


def _dot_nt(a, wt):
    return lax.dot_general(a, wt, (((1,), (1,)), ((), ())), preferred_element_type=F32)


def _mm_nt_kernel(a_ref, wt_ref, o_ref):
    o_ref[...] = _dot_nt(a_ref[...], wt_ref[...]).astype(o_ref.dtype)


def _in_small(h, wt_s):
    m, d = h.shape
    n = -(-N_SMALL // LANE) * LANE
    tm = _tile(m, 512)
    return pl.pallas_call(
        _mm_nt_kernel,
        grid=(m // tm,),
        in_specs=[pl.BlockSpec((tm, d), lambda i: (i, 0)),
                  pl.BlockSpec((n, d), lambda i: (0, 0))],
        out_specs=pl.BlockSpec((tm, n), lambda i: (i, 0)),
        out_shape=jax.ShapeDtypeStruct((m, n), F32),
        compiler_params=_cparams("parallel"),
        name="in_small",
    )(h, wt_s)


def _rope128(x, c, s):
    return x * c + pltpu.roll(x, HEAD_DIM // 2, 1) * s


def _rope_mla(x, c, s):
    half = MLA_ROPE // 2
    return x * c + (pltpu.roll(x, LANE - half, 1) + pltpu.roll(x, half, 1)) * s


N_SMALL = MLA_Q_RANK + MLA_KV_RANK + MLA_ROPE


def _wt_rows_spec(d, tile_of):
    def index_map(*ids):
        row = jnp.asarray(N_SMALL + DIL_COLS * tile_of(*ids), jnp.int32)
        return (pl.multiple_of(row, MLA_ROPE), 0)
    return pl.BlockSpec((pl.Element(DIL_COLS), pl.Element(d)), index_map)


def _dil_proj_kernel(*refs, rope, scale, dil):
    if rope:
        h_ref, w_ref, cos_ref, sin_ref, o_ref = refs[:5]
    else:
        h_ref, w_ref, o_ref = refs[:3]
    acc = _dot_nt(h_ref[...], w_ref[...])
    tm = acc.shape[0]

    def finish(y, c, s):
        if rope:
            y = _rope128(y, c, s)
            if scale != 1.0:
                y = y * scale
        return y.astype(o_ref.dtype)

    if dil == 1:
        c, s = (cos_ref[...], sin_ref[...]) if rope else (None, None)
        for hh in range(DIL_HEADS):
            sl = slice(hh * HEAD_DIM, (hh + 1) * HEAD_DIM)
            o_ref[:, sl] = finish(acc[:, sl], c, s)
        return

    scr_ref = refs[-1]
    rows = tm // dil
    for hh in range(DIL_HEADS):
        scr_ref[hh] = acc[:, hh * HEAD_DIM:(hh + 1) * HEAD_DIM]
    for r in range(dil):
        take = pl.ds(r, rows, stride=dil)
        c, s = (cos_ref[take, :], sin_ref[take, :]) if rope else (None, None)
        for hh in range(DIL_HEADS):
            lo = r * DIL_COLS + hh * HEAD_DIM
            o_ref[:, lo:lo + HEAD_DIM] = finish(scr_ref[hh, take, :], c, s)


def _dil_proj(h, wt_dg, col_tile, dil, seq, cos_t=None, sin_t=None, scale=1.0):
    m, d = h.shape
    bsz = m // seq
    tm = _tile(seq, 1024)
    per_b = seq // tm
    rope = cos_t is not None
    in_specs = [pl.BlockSpec((tm, d), lambda i: (i, 0)),
                _wt_rows_spec(d, lambda i: col_tile)]
    args = [h, wt_dg]
    if rope:
        in_specs += [pl.BlockSpec((tm, LANE), lambda i: (i, 0))] * 2
        args += [cos_t, sin_t]
    scratch = [pltpu.VMEM((DIL_HEADS, tm, HEAD_DIM), F32)] if dil > 1 else []
    return pl.pallas_call(
        functools.partial(_dil_proj_kernel, rope=rope, scale=scale, dil=dil),
        grid=(m // tm,),
        in_specs=in_specs,
        out_specs=pl.BlockSpec((None, tm // dil, dil * DIL_COLS), lambda i: (i // per_b, i % per_b, 0)),
        out_shape=jax.ShapeDtypeStruct((bsz, seq // dil, dil * DIL_COLS), BF16),
        scratch_shapes=scratch,
        compiler_params=_cparams("parallel"),
        name=f"dil_proj_d{dil}",
    )(*args)


def _proj_sigmoid_kernel(h_ref, w_ref, o_ref):
    acc = _dot_nt(h_ref[...], w_ref[...])
    o_ref[...] = jax.nn.sigmoid(acc).astype(o_ref.dtype)


def _gate_proj(h, wt_dg, col_tile0, n):
    m, d = h.shape
    tm = _tile(m, 1024)
    tn = DIL_COLS
    return pl.pallas_call(
        _proj_sigmoid_kernel,
        grid=(n // tn, m // tm),
        in_specs=[pl.BlockSpec((tm, d), lambda j, i: (i, 0)),
                  _wt_rows_spec(d, lambda j, i: col_tile0 + j)],
        out_specs=pl.BlockSpec((tm, tn), lambda j, i: (i, j)),
        out_shape=jax.ShapeDtypeStruct((m, n), BF16),
        compiler_params=_cparams("parallel", "arbitrary"),
        name="gate_proj",
    )(h, wt_dg)


def _qup_kernel(z_ref, g_ref, w_ref, cos_ref, sin_ref, o_ref, zn_ref):
    @pl.when(pl.program_id(1) == 0)
    def _():
        zn_ref[...] = _rms(z_ref[...], g_ref[...]).astype(BF16)

    acc = jnp.dot(zn_ref[...], w_ref[...], preferred_element_type=F32)
    c = cos_ref[...]
    s = sin_ref[...]
    for hh in range(acc.shape[1] // MLA_QK_PAD):
        lo = hh * MLA_QK_PAD
        o_ref[:, lo:lo + LANE] = (acc[:, lo:lo + LANE] * MLA_QSCALE).astype(o_ref.dtype)
        rp = _rope_mla(acc[:, lo + LANE:lo + 2 * LANE], c, s)
        o_ref[:, lo + LANE:lo + 2 * LANE] = (rp * MLA_QSCALE).astype(o_ref.dtype)


def _qup(z_s, g, w_uq_pad, cos_t, sin_t):
    m = z_s.shape[0]
    k = MLA_Q_RANK
    n = w_uq_pad.shape[1]
    tm = _tile(m, 1024)
    tn = _tile(n, 1024)
    return pl.pallas_call(
        _qup_kernel,
        grid=(m // tm, n // tn),
        in_specs=[pl.BlockSpec((tm, k), lambda i, j: (i, 0)),
                  pl.BlockSpec((1, k), lambda i, j: (0, 0)),
                  pl.BlockSpec((k, tn), lambda i, j: (0, j)),
                  pl.BlockSpec((tm, LANE), lambda i, j: (i, 1)),
                  pl.BlockSpec((tm, LANE), lambda i, j: (i, 1))],
        out_specs=pl.BlockSpec((tm, tn), lambda i, j: (i, j)),
        out_shape=jax.ShapeDtypeStruct((m, n), BF16),
        scratch_shapes=[pltpu.VMEM((tm, k), BF16)],
        compiler_params=_cparams("parallel", "arbitrary"),
        name="q_up",
    )(z_s, g.reshape(1, k), w_uq_pad, cos_t, sin_t)


def _kvup_kernel(z_ref, kr_ref, g_ref, wk_ref, wvt_ref, cos_ref, sin_ref, k_ref, vt_ref, zn_ref):
    @pl.when(pl.program_id(1) == 0)
    def _():
        zn_ref[...] = _rms(z_ref[...], g_ref[...]).astype(BF16)

    zn = zn_ref[...]
    k_nope = jnp.dot(zn, wk_ref[...], preferred_element_type=F32)
    vt_ref[...] = _dot_nt(wvt_ref[...], zn).astype(vt_ref.dtype)
    kr = kr_ref[...]
    kr = jnp.where(lax.broadcasted_iota(jnp.int32, kr.shape, 1) < MLA_ROPE, kr, 0.0)
    kr = _rope_mla(kr, cos_ref[...], sin_ref[...]).astype(k_ref.dtype)
    for hh in range(k_nope.shape[1] // MLA_NOPE):
        ko = hh * MLA_QK_PAD
        k_ref[:, ko:ko + LANE] = k_nope[:, hh * MLA_NOPE:(hh + 1) * MLA_NOPE].astype(k_ref.dtype)
        k_ref[:, ko + LANE:ko + 2 * LANE] = kr


def _kvup(z_s, g, w_k, w_vt, cos_t, sin_t, seq):
    m = z_s.shape[0]
    bsz = m // seq
    k = MLA_KV_RANK
    tm = _tile(seq, 1024)
    per_b = seq // tm
    nh = 4
    kv_blk = MLA_Q_RANK // MLA_KV_RANK
    kr_blk = (MLA_Q_RANK + MLA_KV_RANK) // LANE
    return pl.pallas_call(
        _kvup_kernel,
        grid=(m // tm, MLA_HEADS // nh),
        in_specs=[pl.BlockSpec((tm, k), lambda i, j: (i, kv_blk)),
                  pl.BlockSpec((tm, LANE), lambda i, j: (i, kr_blk)),
                  pl.BlockSpec((1, k), lambda i, j: (0, 0)),
                  pl.BlockSpec((k, nh * MLA_NOPE), lambda i, j: (0, j)),
                  pl.BlockSpec((nh * MLA_V, k), lambda i, j: (j, 0)),
                  pl.BlockSpec((tm, LANE), lambda i, j: (i, 1)),
                  pl.BlockSpec((tm, LANE), lambda i, j: (i, 1))],
        out_specs=[pl.BlockSpec((tm, nh * MLA_QK_PAD), lambda i, j: (i, j)),
                   pl.BlockSpec((None, nh * MLA_V, tm), lambda i, j: (i // per_b, j, i % per_b))],
        out_shape=[jax.ShapeDtypeStruct((m, MLA_HEADS * MLA_QK_PAD), BF16),
                   jax.ShapeDtypeStruct((bsz, MLA_HEADS * MLA_V, seq), BF16)],
        scratch_shapes=[pltpu.VMEM((tm, k), BF16)],
        compiler_params=_cparams("parallel", "arbitrary"),
        name="kv_up",
    )(z_s, z_s, g.reshape(1, k), w_k, w_vt, cos_t, sin_t)


def _mla_kernel(q_ref, k_ref, vt_ref, o_ref):
    s_t = _dot_nt(k_ref[...], q_ref[...])
    m = jnp.max(s_t, axis=0, keepdims=True)
    p = jnp.exp2(s_t - m)
    l = jnp.sum(p, axis=0, keepdims=True)
    o_t = jnp.dot(vt_ref[...], p.astype(BF16), preferred_element_type=F32)
    o_ref[...] = (o_t / l).T.astype(o_ref.dtype)


def _mla_attention(q, k, v_t, bsz, seq):
    q3 = q.reshape(bsz, seq, MLA_HEADS * MLA_QK_PAD)
    k3 = k.reshape(bsz, seq, MLA_HEADS * MLA_QK_PAD)
    tq = _tile(seq, 1024)
    out = pl.pallas_call(
        _mla_kernel,
        grid=(bsz, MLA_HEADS, seq // tq),
        in_specs=[pl.BlockSpec((None, tq, MLA_QK_PAD), lambda b, h, i: (b, i, h)),
                  pl.BlockSpec((None, seq, MLA_QK_PAD), lambda b, h, i: (b, 0, h)),
                  pl.BlockSpec((None, MLA_V, seq), lambda b, h, i: (b, h, 0))],
        out_specs=pl.BlockSpec((None, tq, MLA_V), lambda b, h, i: (b, i, h)),
        out_shape=jax.ShapeDtypeStruct((bsz, seq, MLA_HEADS * MLA_V), BF16),
        compiler_params=_cparams("parallel", "parallel", "arbitrary"),
        name="mla_attn",
    )(q3, k3, v_t)
    return out.reshape(bsz * seq, MLA_HEADS * MLA_V)


def _dil_key_row0(ti, *, t_len, tb, kb, n_side):
    return jnp.clip(ti * tb - n_side, 0, t_len - kb)


def _dil_kernel(q_ref, k_ref, v_ref, o_ref, st_ref, *, t_len, tb, kb, tq, kw, n_side):
    ti = pl.program_id(2)
    k_row0 = _dil_key_row0(ti, t_len=t_len, tb=tb, kb=kb, n_side=n_side)
    lane = lax.broadcasted_iota(jnp.int32, (tq, LANE), 1)
    row = lax.broadcasted_iota(jnp.int32, (tq, kw), 0)
    col = lax.broadcasted_iota(jnp.int32, (tq, kw), 1)

    def tile(qi, carry):
        t0 = pl.multiple_of(qi * tq, tq)
        pos0 = ti * tb + t0
        ws = jnp.clip(pos0 - n_side, 0, t_len - kw)
        ws_loc = pl.multiple_of(ws - k_row0, n_side) if t_len > kw else 0
        valid = jnp.abs((pos0 + row) - (ws + col)) <= n_side
        stats = jnp.zeros((tq, LANE), F32)
        for hh in range(DIL_HEADS):
            sl = slice(hh * HEAD_DIM, (hh + 1) * HEAD_DIM)
            q = q_ref[pl.ds(t0, tq), sl]
            k = k_ref[0, pl.ds(ws_loc, kw), sl]
            v = v_ref[0, pl.ds(ws_loc, kw), sl]
            s = lax.dot_general(q, k, (((1,), (1,)), ((), ())), preferred_element_type=F32)
            s = jnp.where(valid, s, -jnp.inf)
            m = jnp.max(s, axis=-1, keepdims=True)
            p = jnp.exp2(s - m)
            l = jnp.sum(p, axis=-1, keepdims=True)
            o = jnp.dot(p.astype(BF16), v, preferred_element_type=F32)
            o_ref[pl.ds(t0, tq), sl] = (o / l).astype(o_ref.dtype)
            stats = jnp.where(lane == hh, m, stats)
            stats = jnp.where(lane == DIL_HEADS + hh, l, stats)
        st_ref[pl.ds(t0, tq), :] = stats
        return carry

    lax.fori_loop(0, tb // tq, tile, 0)


def _dil_rows_per_block(t_len):
    return t_len if t_len <= 2048 else 2048


def _dilated_group(q3, k3, v3, dil, window):
    bsz, t_len, _ = q3.shape
    n_side = window // (2 * dil)
    tq = min(2 * n_side, t_len)
    kw = min(4 * n_side, t_len)
    tb = _dil_rows_per_block(t_len)
    kb = min(tb + 2 * n_side, t_len)
    geom = dict(t_len=t_len, tb=tb, kb=kb, n_side=n_side)

    def kv_index(b, r, ti):
        row0 = _dil_key_row0(ti, **geom)
        return (b, pl.multiple_of(row0, n_side) if kb < t_len else 0, r * DIL_COLS)

    kv_spec = pl.BlockSpec((pl.Element(1), pl.Element(kb), pl.Element(DIL_COLS)), kv_index)
    row_spec = pl.BlockSpec((None, tb, DIL_COLS), lambda b, r, ti: (b, ti, r))

    o, st = pl.pallas_call(
        functools.partial(_dil_kernel, tq=tq, kw=kw, **geom),
        grid=(bsz, dil, t_len // tb),
        in_specs=[row_spec, kv_spec, kv_spec],
        out_specs=[row_spec, pl.BlockSpec((None, tb, LANE), lambda b, r, ti: (b, ti, r))],
        out_shape=[jax.ShapeDtypeStruct((bsz, t_len, dil * DIL_COLS), BF16),
                   jax.ShapeDtypeStruct((bsz, t_len, dil * LANE), F32)],
        compiler_params=_cparams("parallel", "parallel", "parallel"),
        name=f"dilated_d{dil}",
    )(q3, k3, v3)
    return o, st, DIL_HEADS


def _merge_kernel(*refs, hbs, dils):
    ng = len(hbs)
    o_refs, st_refs, y_ref = refs[:ng], refs[ng:2 * ng], refs[2 * ng]
    o_scr, st_scr = refs[2 * ng + 1:]
    tm = y_ref.shape[0]

    for g in range(ng):
        d = dils[g]
        if d == 1:
            continue
        rows = tm // d
        nblk = DIL_HEADS // hbs[g]
        for r in range(d):
            put = pl.ds(r, rows, stride=d)
            for hh in range(DIL_HEADS):
                lo = r * DIL_COLS + hh * HEAD_DIM
                o_scr[g, hh, put, :] = o_refs[g][:, lo:lo + HEAD_DIM].astype(F32)
            for blk in range(nblk):
                lo = (r * nblk + blk) * LANE
                st_scr[g, blk, put, :] = st_refs[g][:, lo:lo + LANE]

    def o_tile(g, h):
        if dils[g] == 1:
            return o_refs[g][:, h * HEAD_DIM:(h + 1) * HEAD_DIM].astype(F32)
        return o_scr[g, h]

    def st_tile(g):
        nblk = DIL_HEADS // hbs[g]
        if dils[g] == 1:
            tiles = [st_refs[g][:, blk * LANE:(blk + 1) * LANE] for blk in range(nblk)]
        else:
            tiles = [st_scr[g, blk] for blk in range(nblk)]
        return functools.reduce(lambda u, w: u + w, tiles)

    sts = [st_tile(g) for g in range(ng)]
    m_all = functools.reduce(jnp.maximum, sts)
    a = [jnp.exp2(sts[g] - m_all) * pltpu.roll(sts[g], LANE - DIL_HEADS, 1) for g in range(ng)]
    den = functools.reduce(lambda u, w: u + w, a)
    head_lane = lax.broadcasted_iota(jnp.int32, (tm, LANE), 1) < DIL_HEADS
    spread = (lax.broadcasted_iota(jnp.int32, (LANE, DIL_COLS), 0)
              == jnp.right_shift(lax.broadcasted_iota(jnp.int32, (LANE, DIL_COLS), 1),
                                 HEAD_DIM.bit_length() - 1)).astype(BF16)
    coef = []
    for g in range(ng):
        cg = jnp.where(head_lane, a[g] / den, 0.0)
        hi = cg.astype(BF16)
        lo = (cg - hi.astype(F32)).astype(BF16)
        coef.append(jnp.dot(hi, spread, preferred_element_type=F32)
                    + jnp.dot(lo, spread, preferred_element_type=F32))
    for h in range(DIL_HEADS):
        sl = slice(h * HEAD_DIM, (h + 1) * HEAD_DIM)
        y = None
        for g in range(ng):
            term = coef[g][:, sl] * o_tile(g, h)
            y = term if y is None else y + term
        y_ref[:, sl] = y.astype(y_ref.dtype)


def _merge_groups(os_, sts, hbs, dils, seq):
    bsz = os_[0].shape[0]
    m = bsz * seq
    tm = _tile(seq, 512)
    per_b = seq // tm

    def spec(arr, d):
        return pl.BlockSpec((None, tm // d, arr.shape[2]), lambda i: (i // per_b, i % per_b, 0))

    in_specs = [spec(o, d) for o, d in zip(os_, dils)] + [spec(s, d) for s, d in zip(sts, dils)]
    ng = len(os_)
    max_blk = max(DIL_HEADS // hb for hb in hbs)
    return pl.pallas_call(
        functools.partial(_merge_kernel, hbs=tuple(hbs), dils=tuple(dils)),
        grid=(m // tm,),
        in_specs=in_specs,
        out_specs=pl.BlockSpec((tm, DIL_COLS), lambda i: (i, 0)),
        out_shape=jax.ShapeDtypeStruct((m, DIL_COLS), BF16),
        scratch_shapes=[pltpu.VMEM((ng, DIL_HEADS, tm, HEAD_DIM), F32),
                        pltpu.VMEM((ng, max_blk, tm, LANE), F32)],
        compiler_params=_cparams("parallel"),
        name="dil_merge",
    )(*os_, *sts)


def _mix_kernel(ya_ref, yb_ref, wa_ref, wb_ref, ga_ref, gb_ref, o_ref):
    pa = jnp.dot(ya_ref[...], wa_ref[...], preferred_element_type=F32)
    pb = jnp.dot(yb_ref[...], wb_ref[...], preferred_element_type=F32)
    o_ref[...] = (ga_ref[...].astype(F32) * pa + gb_ref[...].astype(F32) * pb).astype(o_ref.dtype)


def _mix(ya, yb, w_pa, w_pb, gates):
    m = ya.shape[0]
    d = w_pa.shape[1]
    tm = _tile(m, 1024)
    tn = _tile(d, 1024)
    ga0 = 0
    gb0 = d // tn
    return pl.pallas_call(
        _mix_kernel,
        grid=(m // tm, d // tn),
        in_specs=[pl.BlockSpec((tm, ya.shape[1]), lambda i, j: (i, 0)),
                  pl.BlockSpec((tm, yb.shape[1]), lambda i, j: (i, 0)),
                  pl.BlockSpec((w_pa.shape[0], tn), lambda i, j: (0, j)),
                  pl.BlockSpec((w_pb.shape[0], tn), lambda i, j: (0, j)),
                  pl.BlockSpec((tm, tn), lambda i, j: (i, ga0 + j)),
                  pl.BlockSpec((tm, tn), lambda i, j: (i, gb0 + j))],
        out_specs=pl.BlockSpec((tm, tn), lambda i, j: (i, j)),
        out_shape=jax.ShapeDtypeStruct((m, d), BF16),
        compiler_params=_cparams("parallel", "arbitrary"),
        name="mix",
    )(ya, yb, w_pa, w_pb, gates, gates)


def _outproj_kernel(a_ref, w_ref, x_ref, g_ref, o_ref):
    acc = jnp.dot(a_ref[...], w_ref[...], preferred_element_type=F32)
    o_ref[...] = x_ref[...] + g_ref[...] * acc


def _outproj(a, w, x2d, gate, seq):
    m, k = a.shape
    d = w.shape[1]
    bsz = m // seq
    tm = _tile(seq, 1024)
    tn = _tile(d, 512)
    per_b = seq // tm
    return pl.pallas_call(
        _outproj_kernel,
        grid=(m // tm, d // tn),
        in_specs=[pl.BlockSpec((tm, k), lambda i, j: (i, 0)),
                  pl.BlockSpec((k, tn), lambda i, j: (0, j)),
                  pl.BlockSpec((tm, tn), lambda i, j: (i, j)),
                  pl.BlockSpec((None, 1, tn), lambda i, j: (i // per_b, 0, j))],
        out_specs=pl.BlockSpec((tm, tn), lambda i, j: (i, j)),
        out_shape=jax.ShapeDtypeStruct((m, d), F32),
        compiler_params=_cparams("parallel", "arbitrary"),
        name="out_proj",
    )(a, w, x2d, gate.reshape(bsz, 1, d))


def _ffn_up_kernel(h_ref, wg_ref, wu_ref, o_ref):
    h = h_ref[...]
    g = jnp.dot(h, wg_ref[...], preferred_element_type=F32)
    u = jnp.dot(h, wu_ref[...], preferred_element_type=F32)
    o_ref[...] = (g * jax.nn.sigmoid(g) * u).astype(o_ref.dtype)


def _ffn_up(h, wg, wu):
    m, d = h.shape
    f = wg.shape[1]
    tm = _tile(m, 1024)
    tn = _tile(f, 512)
    return pl.pallas_call(
        _ffn_up_kernel,
        grid=(m // tm, f // tn),
        in_specs=[pl.BlockSpec((tm, d), lambda i, j: (i, 0)),
                  pl.BlockSpec((d, tn), lambda i, j: (0, j)),
                  pl.BlockSpec((d, tn), lambda i, j: (0, j))],
        out_specs=pl.BlockSpec((tm, tn), lambda i, j: (i, j)),
        out_shape=jax.ShapeDtypeStruct((m, f), BF16),
        compiler_params=_cparams("parallel", "arbitrary"),
        name="ffn_up",
    )(h, wg, wu)


def _ffn_down_kernel(a_ref, w_ref, x_ref, g_ref, fg_ref, o_ref, *, nk, final_norm):
    k = pl.program_id(1)

    @pl.when(k == 0)
    def _():
        o_ref[...] = jnp.dot(a_ref[...], w_ref[...], preferred_element_type=F32)

    @pl.when(k > 0)
    def _():
        o_ref[...] = jnp.dot(a_ref[...], w_ref[...], preferred_element_type=F32) + o_ref[...]

    @pl.when(k == nk - 1)
    def _():
        x2 = x_ref[...] + g_ref[...] * o_ref[...]
        o_ref[...] = _rms(x2, fg_ref[...]) if final_norm else x2


def _ffn_down(a, w, x2d, gate, final_g, seq, final_norm):
    m, f = a.shape
    d = w.shape[1]
    bsz = m // seq
    tm = _tile(seq, 512)
    tk = _tile(f, 1024)
    nk = f // tk
    per_b = seq // tm
    return pl.pallas_call(
        functools.partial(_ffn_down_kernel, nk=nk, final_norm=final_norm),
        grid=(m // tm, nk),
        in_specs=[pl.BlockSpec((tm, tk), lambda i, k: (i, k)),
                  pl.BlockSpec((tk, d), lambda i, k: (k, 0)),
                  pl.BlockSpec((tm, d), lambda i, k: (i, 0), pipeline_mode=pl.Buffered(1)),
                  pl.BlockSpec((None, 1, d), lambda i, k: (i // per_b, 0, 0)),
                  pl.BlockSpec((1, d), lambda i, k: (0, 0))],
        out_specs=pl.BlockSpec((tm, d), lambda i, k: (i, 0)),
        out_shape=jax.ShapeDtypeStruct((m, d), F32),
        compiler_params=_cparams("parallel", "arbitrary"),
        name="ffn_down",
    )(a, w, x2d, gate.reshape(bsz, 1, d), final_g.reshape(1, d))


def _cast_pad_kernel(w_ref, o_ref, *, rows, cols):
    tr, tc = o_ref.shape
    r = pl.program_id(0) * tr + lax.broadcasted_iota(jnp.int32, (tr, tc), 0)
    c = pl.program_id(1) * tc + lax.broadcasted_iota(jnp.int32, (tr, tc), 1)
    inside = jnp.logical_and(r < rows, c < cols)
    o_ref[...] = jnp.where(inside, w_ref[...], 0.0).astype(o_ref.dtype)


def _cast_pad_big(w, row_mult, col_mult):
    rows, cols = w.shape
    prow = rows + (-rows) % row_mult
    pcol = cols + (-cols) % col_mult
    tr = _tile(prow, 1024)
    tc = _tile(pcol, 2816) if pcol > 4096 else _tile(pcol, 2048)
    return pl.pallas_call(
        functools.partial(_cast_pad_kernel, rows=rows, cols=cols),
        grid=(prow // tr, pcol // tc),
        in_specs=[pl.BlockSpec((tr, tc), lambda i, j: (i, j))],
        out_specs=pl.BlockSpec((tr, tc), lambda i, j: (i, j)),
        out_shape=jax.ShapeDtypeStruct((prow, pcol), BF16),
        compiler_params=_cparams("parallel", "parallel"),
        name="cast_pad",
    )(w)


def kernel(x, c, positions, w_ada, b_ada, norm1_g, w_in, q_norm_g, w_uq, kv_norm_g, w_ukv,
           w_proj_a, w_proj_b, w_out, norm2_g, w_gate, w_up, w_down, final_g):
    bsz, seq, d = x.shape
    depth = w_ada.shape[0]
    m = bsz * seq
    cos_t, sin_t = _rope_tables(positions)
    x2d = x.reshape(m, d)

    for l in range(depth):
        mod = _ada(c, w_ada[l], b_ada[l])
        sh1, sc1, g1, sh2, sc2, g2 = jnp.split(mod, N_MOD, axis=-1)

        w_dg = w_in[l].T.astype(BF16)
        h = _prenorm(x2d, norm1_g[l], sh1, sc1, seq)
        z_s = _in_small(h, w_dg)
        gates = _gate_proj(h, w_dg, 3 * DIL_GROUPS, 2 * d)

        w_uq_pad = jnp.pad(w_uq[l].reshape(MLA_Q_RANK, MLA_HEADS, MLA_NOPE + MLA_ROPE),
                           ((0, 0), (0, 0), (0, MLA_QK_PAD - MLA_NOPE - MLA_ROPE)))
        w_uq_pad = w_uq_pad.reshape(MLA_Q_RANK, MLA_HEADS * MLA_QK_PAD).astype(BF16)
        q = _qup(z_s, q_norm_g[l], w_uq_pad, cos_t, sin_t)
        w_kv = w_ukv[l].reshape(MLA_KV_RANK, MLA_HEADS, MLA_NOPE + MLA_V)
        w_k = w_kv[:, :, :MLA_NOPE].reshape(MLA_KV_RANK, MLA_HEADS * MLA_NOPE).astype(BF16)
        w_vt = w_kv[:, :, MLA_NOPE:].reshape(MLA_KV_RANK, MLA_HEADS * MLA_V).T.astype(BF16)
        k, v_t = _kvup(z_s, kv_norm_g[l], w_k, w_vt, cos_t, sin_t, seq)
        y_a = _mla_attention(q, k, v_t, bsz, seq)

        os_, sts, hbs, dils = [], [], [], []
        for g, (window, dil) in enumerate(DIL_PATTERNS):
            q3 = _dil_proj(h, w_dg, g, dil, seq, cos_t, sin_t, DIL_QSCALE)
            k3 = _dil_proj(h, w_dg, DIL_GROUPS + g, dil, seq, cos_t, sin_t)
            v3 = _dil_proj(h, w_dg, 2 * DIL_GROUPS + g, dil, seq)
            o_g, st_g, hb = _dilated_group(q3, k3, v3, dil, window)
            os_.append(o_g)
            sts.append(st_g)
            hbs.append(hb)
            dils.append(dil)
        y_b = _merge_groups(os_, sts, hbs, dils, seq)

        mixed = _mix(y_a, y_b, w_proj_a[l].astype(BF16), w_proj_b[l].astype(BF16), gates)
        x2d = _outproj(mixed, w_out[l].astype(BF16), x2d, g1, seq)

        h2 = _prenorm(x2d, norm2_g[l], sh2, sc2, seq)
        ff_mult = 1024
        wg = _cast_pad_big(w_gate[l], 1, ff_mult)
        wu = _cast_pad_big(w_up[l], 1, ff_mult)
        wd = _cast_pad_big(w_down[l], ff_mult, 1)
        hmid = _ffn_up(h2, wg, wu)
        x2d = _ffn_down(hmid, wd, x2d, g2, final_g, seq, final_norm=(l == depth - 1))

    return x2d.reshape(bsz, seq, d)
```

```python
import functools
import math

import jax
import jax.numpy as jnp
from jax import lax
from jax.experimental import pallas as pl
from jax.experimental.pallas import tpu as pltpu

F32 = jnp.float32
BF16 = jnp.bfloat16

NORM_EPS = 1e-6
ROPE_THETA = 10000.0
HEAD_DIM = 128
MLA_HEADS = 16
MLA_Q_RANK = 1024
MLA_KV_RANK = 512
MLA_NOPE = 128
MLA_ROPE = 64
MLA_V = 128
DIL_HEADS = 8
DIL_PATTERNS = ((128, 1), (512, 4), (2048, 16))
DIL_GROUPS = len(DIL_PATTERNS)
N_MOD = 6
LOG2E = 1.4426950408889634

LANE = 128
MLA_QK_PAD = 256
DIL_COLS = DIL_HEADS * HEAD_DIM
VMEM_LIMIT = 56 * 1024 * 1024

MLA_QSCALE = (MLA_NOPE + MLA_ROPE) ** -0.5 * LOG2E
DIL_QSCALE = HEAD_DIM ** -0.5 * LOG2E


def _cparams(*sem):
    return pltpu.CompilerParams(dimension_semantics=sem, vmem_limit_bytes=VMEM_LIMIT)


def _tile(n, pref):
    return pref if n % pref == 0 else n


def _rms(x, g):
    return x * lax.rsqrt(jnp.mean(x * x, axis=-1, keepdims=True) + NORM_EPS) * g


def _ada_kernel(c_ref, w_ref, b_ref, o_ref):
    c = c_ref[...]
    ca = (c * jax.nn.sigmoid(c)).astype(BF16)
    o_ref[...] = jnp.dot(ca, w_ref[...].astype(BF16), preferred_element_type=F32) + b_ref[...]


def _ada(c, w, b):
    bsz, d = c.shape
    n = w.shape[1]
    rows = 8
    cp = jnp.zeros((rows, d), F32).at[:bsz].set(c)
    tn = _tile(n, 512)
    out = pl.pallas_call(
        _ada_kernel,
        grid=(n // tn,),
        in_specs=[pl.BlockSpec((rows, d), lambda j: (0, 0)),
                  pl.BlockSpec((d, tn), lambda j: (0, j)),
                  pl.BlockSpec((1, tn), lambda j: (0, j))],
        out_specs=pl.BlockSpec((rows, tn), lambda j: (0, j)),
        out_shape=jax.ShapeDtypeStruct((rows, n), F32),
        compiler_params=_cparams("parallel"),
        name="ada",
    )(cp, w, b.reshape(1, n))
    return out[:bsz]


def _rope_tab_kernel(pos_ref, inv_ref, sgn_ref, msk_ref, cos_ref, sin_ref):
    ang = pos_ref[...].astype(F32) * inv_ref[...]
    cos_ref[...] = jnp.cos(ang) * msk_ref[...]
    sin_ref[...] = jnp.sin(ang) * sgn_ref[...]


def _rope_tables(positions):
    m = positions.size
    inv_f = ROPE_THETA ** (-jnp.arange(0, HEAD_DIM, 2, dtype=F32) / HEAD_DIM)
    inv_r = ROPE_THETA ** (-jnp.arange(0, MLA_ROPE, 2, dtype=F32) / MLA_ROPE)
    zpad = jnp.zeros((LANE - MLA_ROPE,), F32)
    hf, hr = HEAD_DIM // 2, MLA_ROPE // 2
    inv = jnp.concatenate([inv_f, inv_f, inv_r, inv_r, zpad]).reshape(1, 2 * LANE)
    sgn = jnp.concatenate([-jnp.ones(hf), jnp.ones(hf), -jnp.ones(hr), jnp.ones(hr), zpad]).astype(F32).reshape(1, 2 * LANE)
    msk = jnp.concatenate([jnp.ones(HEAD_DIM + MLA_ROPE), zpad]).astype(F32).reshape(1, 2 * LANE)
    tm = _tile(m, 1024)
    row = pl.BlockSpec((1, 2 * LANE), lambda i: (0, 0))
    tab = pl.BlockSpec((tm, 2 * LANE), lambda i: (i, 0))
    return pl.pallas_call(
        _rope_tab_kernel,
        grid=(m // tm,),
        in_specs=[pl.BlockSpec((tm, 1), lambda i: (i, 0)), row, row, row],
        out_specs=[tab, tab],
        out_shape=[jax.ShapeDtypeStruct((m, 2 * LANE), F32)] * 2,
        compiler_params=_cparams("parallel"),
        name="rope_tables",
    )(positions.reshape(m, 1), inv, sgn, msk)


def _prenorm_kernel(x_ref, g_ref, sh_ref, sc_ref, o_ref):
    y = _rms(x_ref[...], g_ref[...])
    o_ref[...] = (y * (1.0 + sc_ref[...]) + sh_ref[...]).astype(o_ref.dtype)


def _prenorm(x2d, g, shift, scale, seq):
    m, d = x2d.shape
    bsz = m // seq
    tm = _tile(seq, 256)
    per_b = seq // tm
    vec = pl.BlockSpec((None, 1, d), lambda i: (i // per_b, 0, 0))
    return pl.pallas_call(
        _prenorm_kernel,
        grid=(m // tm,),
        in_specs=[pl.BlockSpec((tm, d), lambda i: (i, 0)),
                  pl.BlockSpec((1, d), lambda i: (0, 0)), vec, vec],
        out_specs=pl.BlockSpec((tm, d), lambda i: (i, 0)),
        out_shape=jax.ShapeDtypeStruct((m, d), BF16),
        compiler_params=_cparams("parallel"),
        name="prenorm",
    )(x2d, g.reshape(1, d), shift.reshape(bsz, 1, d), scale.reshape(bsz, 1, d))


def _dot_nt(a, wt):
    return lax.dot_general(a, wt, (((1,), (1,)), ((), ())), preferred_element_type=F32)


def _mm_nt_kernel(a_ref, wt_ref, o_ref):
    o_ref[...] = _dot_nt(a_ref[...], wt_ref[...]).astype(o_ref.dtype)


def _in_small(h, wt_s):
    m, d = h.shape
    n = -(-N_SMALL // LANE) * LANE
    tm = _tile(m, 512)
    return pl.pallas_call(
        _mm_nt_kernel,
        grid=(m // tm,),
        in_specs=[pl.BlockSpec((tm, d), lambda i: (i, 0)),
                  pl.BlockSpec((n, d), lambda i: (0, 0))],
        out_specs=pl.BlockSpec((tm, n), lambda i: (i, 0)),
        out_shape=jax.ShapeDtypeStruct((m, n), F32),
        compiler_params=_cparams("parallel"),
        name="in_small",
    )(h, wt_s)


def _rope128(x, c, s):
    return x * c + pltpu.roll(x, HEAD_DIM // 2, 1) * s


def _rope_mla(x, c, s):
    half = MLA_ROPE // 2
    return x * c + (pltpu.roll(x, LANE - half, 1) + pltpu.roll(x, half, 1)) * s


N_SMALL = MLA_Q_RANK + MLA_KV_RANK + MLA_ROPE


def _wt_rows_spec(d, tile_of):
    def index_map(*ids):
        row = jnp.asarray(N_SMALL + DIL_COLS * tile_of(*ids), jnp.int32)
        return (pl.multiple_of(row, MLA_ROPE), 0)
    return pl.BlockSpec((pl.Element(DIL_COLS), pl.Element(d)), index_map)


def _dil_proj_kernel(*refs, rope, scale, dil):
    if rope:
        h_ref, w_ref, cos_ref, sin_ref, o_ref = refs[:5]
    else:
        h_ref, w_ref, o_ref = refs[:3]
    acc = _dot_nt(h_ref[...], w_ref[...])
    tm = acc.shape[0]

    def finish(y, c, s):
        if rope:
            y = _rope128(y, c, s)
            if scale != 1.0:
                y = y * scale
        return y.astype(o_ref.dtype)

    if dil == 1:
        c, s = (cos_ref[...], sin_ref[...]) if rope else (None, None)
        for hh in range(DIL_HEADS):
            sl = slice(hh * HEAD_DIM, (hh + 1) * HEAD_DIM)
            o_ref[:, sl] = finish(acc[:, sl], c, s)
        return

    scr_ref = refs[-1]
    rows = tm // dil
    for hh in range(DIL_HEADS):
        scr_ref[hh] = acc[:, hh * HEAD_DIM:(hh + 1) * HEAD_DIM]
    for r in range(dil):
        take = pl.ds(r, rows, stride=dil)
        c, s = (cos_ref[take, :], sin_ref[take, :]) if rope else (None, None)
        for hh in range(DIL_HEADS):
            lo = r * DIL_COLS + hh * HEAD_DIM
            o_ref[:, lo:lo + HEAD_DIM] = finish(scr_ref[hh, take, :], c, s)


def _dil_proj(h, wt_dg, col_tile, dil, seq, cos_t=None, sin_t=None, scale=1.0):
    m, d = h.shape
    bsz = m // seq
    tm = _tile(seq, 1024)
    per_b = seq // tm
    rope = cos_t is not None
    in_specs = [pl.BlockSpec((tm, d), lambda i: (i, 0)),
                _wt_rows_spec(d, lambda i: col_tile)]
    args = [h, wt_dg]
    if rope:
        in_specs += [pl.BlockSpec((tm, LANE), lambda i: (i, 0))] * 2
        args += [cos_t, sin_t]
    scratch = [pltpu.VMEM((DIL_HEADS, tm, HEAD_DIM), F32)] if dil > 1 else []
    return pl.pallas_call(
        functools.partial(_dil_proj_kernel, rope=rope, scale=scale, dil=dil),
        grid=(m // tm,),
        in_specs=in_specs,
        out_specs=pl.BlockSpec((None, tm // dil, dil * DIL_COLS), lambda i: (i // per_b, i % per_b, 0)),
        out_shape=jax.ShapeDtypeStruct((bsz, seq // dil, dil * DIL_COLS), BF16),
        scratch_shapes=scratch,
        compiler_params=_cparams("parallel"),
        name=f"dil_proj_d{dil}",
    )(*args)


def _proj_sigmoid_kernel(h_ref, w_ref, o_ref):
    acc = _dot_nt(h_ref[...], w_ref[...])
    o_ref[...] = jax.nn.sigmoid(acc).astype(o_ref.dtype)


def _gate_proj(h, wt_dg, col_tile0, n):
    m, d = h.shape
    tm = _tile(m, 1024)
    tn = DIL_COLS
    return pl.pallas_call(
        _proj_sigmoid_kernel,
        grid=(n // tn, m // tm),
        in_specs=[pl.BlockSpec((tm, d), lambda j, i: (i, 0)),
                  _wt_rows_spec(d, lambda j, i: col_tile0 + j)],
        out_specs=pl.BlockSpec((tm, tn), lambda j, i: (i, j)),
        out_shape=jax.ShapeDtypeStruct((m, n), BF16),
        compiler_params=_cparams("parallel", "arbitrary"),
        name="gate_proj",
    )(h, wt_dg)


def _qup_kernel(z_ref, g_ref, w_ref, cos_ref, sin_ref, o_ref, zn_ref):
    @pl.when(pl.program_id(1) == 0)
    def _():
        zn_ref[...] = _rms(z_ref[...], g_ref[...]).astype(BF16)

    acc = jnp.dot(zn_ref[...], w_ref[...], preferred_element_type=F32)
    c = cos_ref[...]
    s = sin_ref[...]
    for hh in range(acc.shape[1] // MLA_QK_PAD):
        lo = hh * MLA_QK_PAD
        o_ref[:, lo:lo + LANE] = (acc[:, lo:lo + LANE] * MLA_QSCALE).astype(o_ref.dtype)
        rp = _rope_mla(acc[:, lo + LANE:lo + 2 * LANE], c, s)
        o_ref[:, lo + LANE:lo + 2 * LANE] = (rp * MLA_QSCALE).astype(o_ref.dtype)


def _qup(z_s, g, w_uq_pad, cos_t, sin_t):
    m = z_s.shape[0]
    k = MLA_Q_RANK
    n = w_uq_pad.shape[1]
    tm = _tile(m, 1024)
    tn = _tile(n, 1024)
    return pl.pallas_call(
        _qup_kernel,
        grid=(m // tm, n // tn),
        in_specs=[pl.BlockSpec((tm, k), lambda i, j: (i, 0)),
                  pl.BlockSpec((1, k), lambda i, j: (0, 0)),
                  pl.BlockSpec((k, tn), lambda i, j: (0, j)),
                  pl.BlockSpec((tm, LANE), lambda i, j: (i, 1)),
                  pl.BlockSpec((tm, LANE), lambda i, j: (i, 1))],
        out_specs=pl.BlockSpec((tm, tn), lambda i, j: (i, j)),
        out_shape=jax.ShapeDtypeStruct((m, n), BF16),
        scratch_shapes=[pltpu.VMEM((tm, k), BF16)],
        compiler_params=_cparams("parallel", "arbitrary"),
        name="q_up",
    )(z_s, g.reshape(1, k), w_uq_pad, cos_t, sin_t)


def _kvup_kernel(z_ref, kr_ref, g_ref, wk_ref, wvt_ref, cos_ref, sin_ref, k_ref, vt_ref, zn_ref):
    @pl.when(pl.program_id(1) == 0)
    def _():
        zn_ref[...] = _rms(z_ref[...], g_ref[...]).astype(BF16)

    zn = zn_ref[...]
    k_nope = jnp.dot(zn, wk_ref[...], preferred_element_type=F32)
    vt_ref[...] = _dot_nt(wvt_ref[...], zn).astype(vt_ref.dtype)
    kr = kr_ref[...]
    kr = jnp.where(lax.broadcasted_iota(jnp.int32, kr.shape, 1) < MLA_ROPE, kr, 0.0)
    kr = _rope_mla(kr, cos_ref[...], sin_ref[...]).astype(k_ref.dtype)
    for hh in range(k_nope.shape[1] // MLA_NOPE):
        ko = hh * MLA_QK_PAD
        k_ref[:, ko:ko + LANE] = k_nope[:, hh * MLA_NOPE:(hh + 1) * MLA_NOPE].astype(k_ref.dtype)
        k_ref[:, ko + LANE:ko + 2 * LANE] = kr


def _kvup(z_s, g, w_k, w_vt, cos_t, sin_t, seq):
    m = z_s.shape[0]
    bsz = m // seq
    k = MLA_KV_RANK
    tm = _tile(seq, 1024)
    per_b = seq // tm
    nh = 4
    kv_blk = MLA_Q_RANK // MLA_KV_RANK
    kr_blk = (MLA_Q_RANK + MLA_KV_RANK) // LANE
    return pl.pallas_call(
        _kvup_kernel,
        grid=(m // tm, MLA_HEADS // nh),
        in_specs=[pl.BlockSpec((tm, k), lambda i, j: (i, kv_blk)),
                  pl.BlockSpec((tm, LANE), lambda i, j: (i, kr_blk)),
                  pl.BlockSpec((1, k), lambda i, j: (0, 0)),
                  pl.BlockSpec((k, nh * MLA_NOPE), lambda i, j: (0, j)),
                  pl.BlockSpec((nh * MLA_V, k), lambda i, j: (j, 0)),
                  pl.BlockSpec((tm, LANE), lambda i, j: (i, 1)),
                  pl.BlockSpec((tm, LANE), lambda i, j: (i, 1))],
        out_specs=[pl.BlockSpec((tm, nh * MLA_QK_PAD), lambda i, j: (i, j)),
                   pl.BlockSpec((None, nh * MLA_V, tm), lambda i, j: (i // per_b, j, i % per_b))],
        out_shape=[jax.ShapeDtypeStruct((m, MLA_HEADS * MLA_QK_PAD), BF16),
                   jax.ShapeDtypeStruct((bsz, MLA_HEADS * MLA_V, seq), BF16)],
        scratch_shapes=[pltpu.VMEM((tm, k), BF16)],
        compiler_params=_cparams("parallel", "arbitrary"),
        name="kv_up",
    )(z_s, z_s, g.reshape(1, k), w_k, w_vt, cos_t, sin_t)


def _mla_kernel(q_ref, k_ref, vt_ref, o_ref, *, sub):
    n = q_ref.shape[0] // sub
    s_ts = [_dot_nt(k_ref[...], q_ref[i * sub:(i + 1) * sub, :]) for i in range(n)]
    ms = [jnp.max(s_t, axis=0, keepdims=True) for s_t in s_ts]
    for i in range(n):
        p = jnp.exp2(s_ts[i] - ms[i])
        l = jnp.sum(p, axis=0, keepdims=True)
        o_t = jnp.dot(vt_ref[...], p.astype(BF16), preferred_element_type=F32)
        o_ref[i * sub:(i + 1) * sub, :] = (o_t / l).T.astype(o_ref.dtype)


def _mla_attention(q, k, v_t, bsz, seq):
    q3 = q.reshape(bsz, seq, MLA_HEADS * MLA_QK_PAD)
    k3 = k.reshape(bsz, seq, MLA_HEADS * MLA_QK_PAD)
    tq = _tile(seq, 2048)
    out = pl.pallas_call(
        functools.partial(_mla_kernel, sub=min(tq, 1024)),
        grid=(bsz, MLA_HEADS, seq // tq),
        in_specs=[pl.BlockSpec((None, tq, MLA_QK_PAD), lambda b, h, i: (b, i, h)),
                  pl.BlockSpec((None, seq, MLA_QK_PAD), lambda b, h, i: (b, 0, h)),
                  pl.BlockSpec((None, MLA_V, seq), lambda b, h, i: (b, h, 0))],
        out_specs=pl.BlockSpec((None, tq, MLA_V), lambda b, h, i: (b, i, h)),
        out_shape=jax.ShapeDtypeStruct((bsz, seq, MLA_HEADS * MLA_V), BF16),
        compiler_params=_cparams("parallel", "parallel", "arbitrary"),
        name="mla_attn",
    )(q3, k3, v_t)
    return out.reshape(bsz * seq, MLA_HEADS * MLA_V)


def _dil_key_row0(ti, *, t_len, tb, kb, n_side):
    return jnp.clip(ti * tb - n_side, 0, t_len - kb)


def _dil_kernel(q_ref, k_ref, v_ref, o_ref, st_ref, *, t_len, tb, kb, tq, kw, n_side):
    ti = pl.program_id(2)
    k_row0 = _dil_key_row0(ti, t_len=t_len, tb=tb, kb=kb, n_side=n_side)
    lane = lax.broadcasted_iota(jnp.int32, (tq, LANE), 1)
    row = lax.broadcasted_iota(jnp.int32, (tq, kw), 0)
    col = lax.broadcasted_iota(jnp.int32, (tq, kw), 1)

    def tile(qi, carry):
        t0 = pl.multiple_of(qi * tq, tq)
        pos0 = ti * tb + t0
        ws = jnp.clip(pos0 - n_side, 0, t_len - kw)
        ws_loc = pl.multiple_of(ws - k_row0, n_side) if t_len > kw else 0
        valid = jnp.abs((pos0 + row) - (ws + col)) <= n_side
        stats = jnp.zeros((tq, LANE), F32)
        for hh in range(DIL_HEADS):
            sl = slice(hh * HEAD_DIM, (hh + 1) * HEAD_DIM)
            q = q_ref[pl.ds(t0, tq), sl]
            k = k_ref[0, pl.ds(ws_loc, kw), sl]
            v = v_ref[0, pl.ds(ws_loc, kw), sl]
            s = lax.dot_general(q, k, (((1,), (1,)), ((), ())), preferred_element_type=F32)
            s = jnp.where(valid, s, -jnp.inf)
            m = jnp.max(s, axis=-1, keepdims=True)
            p = jnp.exp2(s - m)
            l = jnp.sum(p, axis=-1, keepdims=True)
            o = jnp.dot(p.astype(BF16), v, preferred_element_type=F32)
            o_ref[pl.ds(t0, tq), sl] = (o / l).astype(o_ref.dtype)
            stats = jnp.where(lane == hh, m, stats)
            stats = jnp.where(lane == DIL_HEADS + hh, l, stats)
        st_ref[pl.ds(t0, tq), :] = stats
        return carry

    n_tiles = tb // tq
    lax.fori_loop(0, n_tiles, tile, 0, unroll=min(4, n_tiles))


def _dil_rows_per_block(t_len):
    return t_len if t_len <= 2048 else 2048


def _dilated_group(q3, k3, v3, dil, window):
    bsz, t_len, _ = q3.shape
    n_side = window // (2 * dil)
    tq = min(2 * n_side, t_len)
    kw = min(4 * n_side, t_len)
    tb = _dil_rows_per_block(t_len)
    kb = min(tb + 2 * n_side, t_len)
    geom = dict(t_len=t_len, tb=tb, kb=kb, n_side=n_side)

    def kv_index(b, r, ti):
        row0 = _dil_key_row0(ti, **geom)
        return (b, pl.multiple_of(row0, n_side) if kb < t_len else 0, r * DIL_COLS)

    kv_spec = pl.BlockSpec((pl.Element(1), pl.Element(kb), pl.Element(DIL_COLS)), kv_index)
    row_spec = pl.BlockSpec((None, tb, DIL_COLS), lambda b, r, ti: (b, ti, r))

    o, st = pl.pallas_call(
        functools.partial(_dil_kernel, tq=tq, kw=kw, **geom),
        grid=(bsz, dil, t_len // tb),
        in_specs=[row_spec, kv_spec, kv_spec],
        out_specs=[row_spec, pl.BlockSpec((None, tb, LANE), lambda b, r, ti: (b, ti, r))],
        out_shape=[jax.ShapeDtypeStruct((bsz, t_len, dil * DIL_COLS), BF16),
                   jax.ShapeDtypeStruct((bsz, t_len, dil * LANE), F32)],
        compiler_params=_cparams("parallel", "parallel", "parallel"),
        name=f"dilated_d{dil}",
    )(q3, k3, v3)
    return o, st, DIL_HEADS


def _merge_kernel(*refs, hbs, dils):
    ng = len(hbs)
    o_refs, st_refs, y_ref = refs[:ng], refs[ng:2 * ng], refs[2 * ng]
    o_scr, st_scr = refs[2 * ng + 1:]
    tm = y_ref.shape[0]

    for g in range(ng):
        d = dils[g]
        if d == 1:
            continue
        rows = tm // d
        nblk = DIL_HEADS // hbs[g]
        for r in range(d):
            put = pl.ds(r, rows, stride=d)
            for hh in range(DIL_HEADS):
                lo = r * DIL_COLS + hh * HEAD_DIM
                o_scr[g, hh, put, :] = o_refs[g][:, lo:lo + HEAD_DIM].astype(F32)
            for blk in range(nblk):
                lo = (r * nblk + blk) * LANE
                st_scr[g, blk, put, :] = st_refs[g][:, lo:lo + LANE]

    def o_tile(g, h):
        if dils[g] == 1:
            return o_refs[g][:, h * HEAD_DIM:(h + 1) * HEAD_DIM].astype(F32)
        return o_scr[g, h]

    def st_tile(g):
        nblk = DIL_HEADS // hbs[g]
        if dils[g] == 1:
            tiles = [st_refs[g][:, blk * LANE:(blk + 1) * LANE] for blk in range(nblk)]
        else:
            tiles = [st_scr[g, blk] for blk in range(nblk)]
        return functools.reduce(lambda u, w: u + w, tiles)

    sts = [st_tile(g) for g in range(ng)]
    m_all = functools.reduce(jnp.maximum, sts)
    a = [jnp.exp2(sts[g] - m_all) * pltpu.roll(sts[g], LANE - DIL_HEADS, 1) for g in range(ng)]
    den = functools.reduce(lambda u, w: u + w, a)
    head_lane = lax.broadcasted_iota(jnp.int32, (tm, LANE), 1) < DIL_HEADS
    spread = (lax.broadcasted_iota(jnp.int32, (LANE, DIL_COLS), 0)
              == jnp.right_shift(lax.broadcasted_iota(jnp.int32, (LANE, DIL_COLS), 1),
                                 HEAD_DIM.bit_length() - 1)).astype(BF16)
    coef = []
    for g in range(ng):
        cg = jnp.where(head_lane, a[g] / den, 0.0)
        hi = cg.astype(BF16)
        lo = (cg - hi.astype(F32)).astype(BF16)
        coef.append(jnp.dot(hi, spread, preferred_element_type=F32)
                    + jnp.dot(lo, spread, preferred_element_type=F32))
    for h in range(DIL_HEADS):
        sl = slice(h * HEAD_DIM, (h + 1) * HEAD_DIM)
        y = None
        for g in range(ng):
            term = coef[g][:, sl] * o_tile(g, h)
            y = term if y is None else y + term
        y_ref[:, sl] = y.astype(y_ref.dtype)


def _merge_groups(os_, sts, hbs, dils, seq):
    bsz = os_[0].shape[0]
    m = bsz * seq
    tm = _tile(seq, 512)
    per_b = seq // tm

    def spec(arr, d):
        return pl.BlockSpec((None, tm // d, arr.shape[2]), lambda i: (i // per_b, i % per_b, 0))

    in_specs = [spec(o, d) for o, d in zip(os_, dils)] + [spec(s, d) for s, d in zip(sts, dils)]
    ng = len(os_)
    max_blk = max(DIL_HEADS // hb for hb in hbs)
    return pl.pallas_call(
        functools.partial(_merge_kernel, hbs=tuple(hbs), dils=tuple(dils)),
        grid=(m // tm,),
        in_specs=in_specs,
        out_specs=pl.BlockSpec((tm, DIL_COLS), lambda i: (i, 0)),
        out_shape=jax.ShapeDtypeStruct((m, DIL_COLS), BF16),
        scratch_shapes=[pltpu.VMEM((ng, DIL_HEADS, tm, HEAD_DIM), F32),
                        pltpu.VMEM((ng, max_blk, tm, LANE), F32)],
        compiler_params=_cparams("parallel"),
        name="dil_merge",
    )(*os_, *sts)


def _mix_kernel(ya_ref, yb_ref, wa_ref, wb_ref, ga_ref, gb_ref, o_ref):
    pa = jnp.dot(ya_ref[...], wa_ref[...], preferred_element_type=F32)
    pb = jnp.dot(yb_ref[...], wb_ref[...], preferred_element_type=F32)
    o_ref[...] = (ga_ref[...].astype(F32) * pa + gb_ref[...].astype(F32) * pb).astype(o_ref.dtype)


def _mix(ya, yb, w_pa, w_pb, gates):
    m = ya.shape[0]
    d = w_pa.shape[1]
    tm = _tile(m, 1024)
    tn = _tile(d, 1024)
    ga0 = 0
    gb0 = d // tn
    return pl.pallas_call(
        _mix_kernel,
        grid=(m // tm, d // tn),
        in_specs=[pl.BlockSpec((tm, ya.shape[1]), lambda i, j: (i, 0)),
                  pl.BlockSpec((tm, yb.shape[1]), lambda i, j: (i, 0)),
                  pl.BlockSpec((w_pa.shape[0], tn), lambda i, j: (0, j)),
                  pl.BlockSpec((w_pb.shape[0], tn), lambda i, j: (0, j)),
                  pl.BlockSpec((tm, tn), lambda i, j: (i, ga0 + j)),
                  pl.BlockSpec((tm, tn), lambda i, j: (i, gb0 + j))],
        out_specs=pl.BlockSpec((tm, tn), lambda i, j: (i, j)),
        out_shape=jax.ShapeDtypeStruct((m, d), BF16),
        compiler_params=_cparams("parallel", "arbitrary"),
        name="mix",
    )(ya, yb, w_pa, w_pb, gates, gates)


def _outproj_kernel(a_ref, w_ref, x_ref, g_ref, o_ref):
    acc = jnp.dot(a_ref[...], w_ref[...], preferred_element_type=F32)
    o_ref[...] = x_ref[...] + g_ref[...] * acc


def _outproj(a, w, x2d, gate, seq):
    m, k = a.shape
    d = w.shape[1]
    bsz = m // seq
    tm = _tile(seq, 1024)
    tn = _tile(d, 512)
    per_b = seq // tm
    return pl.pallas_call(
        _outproj_kernel,
        grid=(m // tm, d // tn),
        in_specs=[pl.BlockSpec((tm, k), lambda i, j: (i, 0)),
                  pl.BlockSpec((k, tn), lambda i, j: (0, j)),
                  pl.BlockSpec((tm, tn), lambda i, j: (i, j)),
                  pl.BlockSpec((None, 1, tn), lambda i, j: (i // per_b, 0, j))],
        out_specs=pl.BlockSpec((tm, tn), lambda i, j: (i, j)),
        out_shape=jax.ShapeDtypeStruct((m, d), F32),
        compiler_params=_cparams("parallel", "arbitrary"),
        name="out_proj",
    )(a, w, x2d, gate.reshape(bsz, 1, d))


def _ffn_up_kernel(h_ref, wg_ref, wu_ref, o_ref):
    h = h_ref[...]
    g = jnp.dot(h, wg_ref[...], preferred_element_type=F32)
    u = jnp.dot(h, wu_ref[...], preferred_element_type=F32)
    o_ref[...] = (g * jax.nn.sigmoid(g) * u).astype(o_ref.dtype)


def _ffn_up(h, wg, wu):
    m, d = h.shape
    f = wg.shape[1]
    tm = _tile(m, 1024)
    tn = _tile(f, 512)
    return pl.pallas_call(
        _ffn_up_kernel,
        grid=(m // tm, f // tn),
        in_specs=[pl.BlockSpec((tm, d), lambda i, j: (i, 0)),
                  pl.BlockSpec((d, tn), lambda i, j: (0, j)),
                  pl.BlockSpec((d, tn), lambda i, j: (0, j))],
        out_specs=pl.BlockSpec((tm, tn), lambda i, j: (i, j)),
        out_shape=jax.ShapeDtypeStruct((m, f), BF16),
        compiler_params=_cparams("parallel", "arbitrary"),
        name="ffn_up",
    )(h, wg, wu)


def _ffn_down_kernel(a_ref, w_ref, x_ref, g_ref, fg_ref, o_ref, *, nk, final_norm):
    k = pl.program_id(1)

    @pl.when(k == 0)
    def _():
        o_ref[...] = jnp.dot(a_ref[...], w_ref[...], preferred_element_type=F32)

    @pl.when(k > 0)
    def _():
        o_ref[...] = jnp.dot(a_ref[...], w_ref[...], preferred_element_type=F32) + o_ref[...]

    @pl.when(k == nk - 1)
    def _():
        x2 = x_ref[...] + g_ref[...] * o_ref[...]
        o_ref[...] = _rms(x2, fg_ref[...]) if final_norm else x2


def _ffn_down(a, w, x2d, gate, final_g, seq, final_norm):
    m, f = a.shape
    d = w.shape[1]
    bsz = m // seq
    tm = _tile(seq, 512)
    tk = _tile(f, 1024)
    nk = f // tk
    per_b = seq // tm
    return pl.pallas_call(
        functools.partial(_ffn_down_kernel, nk=nk, final_norm=final_norm),
        grid=(m // tm, nk),
        in_specs=[pl.BlockSpec((tm, tk), lambda i, k: (i, k)),
                  pl.BlockSpec((tk, d), lambda i, k: (k, 0)),
                  pl.BlockSpec((tm, d), lambda i, k: (i, 0), pipeline_mode=pl.Buffered(1)),
                  pl.BlockSpec((None, 1, d), lambda i, k: (i // per_b, 0, 0)),
                  pl.BlockSpec((1, d), lambda i, k: (0, 0))],
        out_specs=pl.BlockSpec((tm, d), lambda i, k: (i, 0)),
        out_shape=jax.ShapeDtypeStruct((m, d), F32),
        compiler_params=_cparams("parallel", "arbitrary"),
        name="ffn_down",
    )(a, w, x2d, gate.reshape(bsz, 1, d), final_g.reshape(1, d))


def _cast_pad_kernel(w_ref, o_ref, *, rows, cols):
    tr, tc = o_ref.shape
    r = pl.program_id(0) * tr + lax.broadcasted_iota(jnp.int32, (tr, tc), 0)
    c = pl.program_id(1) * tc + lax.broadcasted_iota(jnp.int32, (tr, tc), 1)
    inside = jnp.logical_and(r < rows, c < cols)
    o_ref[...] = jnp.where(inside, w_ref[...], 0.0).astype(o_ref.dtype)


def _cast_pad_big(w, row_mult, col_mult):
    rows, cols = w.shape
    prow = rows + (-rows) % row_mult
    pcol = cols + (-cols) % col_mult
    tr = _tile(prow, 1024)
    tc = _tile(pcol, 2816) if pcol > 4096 else _tile(pcol, 2048)
    return pl.pallas_call(
        functools.partial(_cast_pad_kernel, rows=rows, cols=cols),
        grid=(prow // tr, pcol // tc),
        in_specs=[pl.BlockSpec((tr, tc), lambda i, j: (i, j))],
        out_specs=pl.BlockSpec((tr, tc), lambda i, j: (i, j)),
        out_shape=jax.ShapeDtypeStruct((prow, pcol), BF16),
        compiler_params=_cparams("parallel", "parallel"),
        name="cast_pad",
    )(w)


def kernel(x, c, positions, w_ada, b_ada, norm1_g, w_in, q_norm_g, w_uq, kv_norm_g, w_ukv,
           w_proj_a, w_proj_b, w_out, norm2_g, w_gate, w_up, w_down, final_g):
    bsz, seq, d = x.shape
    depth = w_ada.shape[0]
    m = bsz * seq
    cos_t, sin_t = _rope_tables(positions)
    x2d = x.reshape(m, d)

    for l in range(depth):
        mod = _ada(c, w_ada[l], b_ada[l])
        sh1, sc1, g1, sh2, sc2, g2 = jnp.split(mod, N_MOD, axis=-1)

        w_dg = w_in[l].T.astype(BF16)
        h = _prenorm(x2d, norm1_g[l], sh1, sc1, seq)
        z_s = _in_small(h, w_dg)
        gates = _gate_proj(h, w_dg, 3 * DIL_GROUPS, 2 * d)

        w_uq_pad = jnp.pad(w_uq[l].reshape(MLA_Q_RANK, MLA_HEADS, MLA_NOPE + MLA_ROPE),
                           ((0, 0), (0, 0), (0, MLA_QK_PAD - MLA_NOPE - MLA_ROPE)))
        w_uq_pad = w_uq_pad.reshape(MLA_Q_RANK, MLA_HEADS * MLA_QK_PAD).astype(BF16)
        q = _qup(z_s, q_norm_g[l], w_uq_pad, cos_t, sin_t)
        w_kv = w_ukv[l].reshape(MLA_KV_RANK, MLA_HEADS, MLA_NOPE + MLA_V)
        w_k = w_kv[:, :, :MLA_NOPE].reshape(MLA_KV_RANK, MLA_HEADS * MLA_NOPE).astype(BF16)
        w_vt = w_kv[:, :, MLA_NOPE:].reshape(MLA_KV_RANK, MLA_HEADS * MLA_V).T.astype(BF16)
        k, v_t = _kvup(z_s, kv_norm_g[l], w_k, w_vt, cos_t, sin_t, seq)
        y_a = _mla_attention(q, k, v_t, bsz, seq)

        os_, sts, hbs, dils = [], [], [], []
        for g, (window, dil) in enumerate(DIL_PATTERNS):
            q3 = _dil_proj(h, w_dg, g, dil, seq, cos_t, sin_t, DIL_QSCALE)
            k3 = _dil_proj(h, w_dg, DIL_GROUPS + g, dil, seq, cos_t, sin_t)
            v3 = _dil_proj(h, w_dg, 2 * DIL_GROUPS + g, dil, seq)
            o_g, st_g, hb = _dilated_group(q3, k3, v3, dil, window)
            os_.append(o_g)
            sts.append(st_g)
            hbs.append(hb)
            dils.append(dil)
        y_b = _merge_groups(os_, sts, hbs, dils, seq)

        mixed = _mix(y_a, y_b, w_proj_a[l].astype(BF16), w_proj_b[l].astype(BF16), gates)
        x2d = _outproj(mixed, w_out[l].astype(BF16), x2d, g1, seq)

        h2 = _prenorm(x2d, norm2_g[l], sh2, sc2, seq)
        ff_mult = 1024
        wg = _cast_pad_big(w_gate[l], 1, ff_mult)
        wu = _cast_pad_big(w_up[l], 1, ff_mult)
        wd = _cast_pad_big(w_down[l], ff_mult, 1)
        hmid = _ffn_up(h2, wg, wu)
        x2d = _ffn_down(hmid, wd, x2d, g2, final_g, seq, final_norm=(l == depth - 1))

    return x2d.reshape(bsz, seq, d)
```

```python
import functools
import math

import jax
import jax.numpy as jnp
from jax import lax
from jax.experimental import pallas as pl
from jax.experimental.pallas import tpu as pltpu

F32 = jnp.float32
BF16 = jnp.bfloat16

NORM_EPS = 1e-6
ROPE_THETA = 10000.0
HEAD_DIM = 128
MLA_HEADS = 16
MLA_Q_RANK = 1024
MLA_KV_RANK = 512
MLA_NOPE = 128
MLA_ROPE = 64
MLA_V = 128
DIL_HEADS = 8
DIL_PATTERNS = ((128, 1), (512, 4), (2048, 16))
DIL_GROUPS = len(DIL_PATTERNS)
N_MOD = 6
LOG2E = 1.4426950408889634

LANE = 128
MLA_QK_PAD = 256
DIL_COLS = DIL_HEADS * HEAD_DIM
VMEM_LIMIT = 56 * 1024 * 1024

MLA_QSCALE = (MLA_NOPE + MLA_ROPE) ** -0.5 * LOG2E
DIL_QSCALE = HEAD_DIM ** -0.5 * LOG2E


def _cparams(*sem):
    return pltpu.CompilerParams(dimension_semantics=sem, vmem_limit_bytes=VMEM_LIMIT)


def _tile(n, pref):
    return pref if n % pref == 0 else n


def _rms(x, g):
    return x * lax.rsqrt(jnp.mean(x * x, axis=-1, keepdims=True) + NORM_EPS) * g


def _side_cast_rows(rows_padded, n_steps):
    tr = rows_padded // n_steps
    assert tr * n_steps == rows_padded and tr % 16 == 0, (rows_padded, n_steps)
    return tr


def _side_cast_specs(w, rows_padded, cols_padded, n_steps, step_of):
    tr = _side_cast_rows(rows_padded, n_steps)
    last = (w.shape[0] - 1) // tr

    def in_map(*ids):
        return (jnp.minimum(step_of(*ids), last), 0)

    def out_map(*ids):
        return (step_of(*ids), 0)

    return (pl.BlockSpec((tr, cols_padded), in_map), pl.BlockSpec((tr, cols_padded), out_map),
            jax.ShapeDtypeStruct((rows_padded, cols_padded), BF16))


def _side_cast(w_ref, o_ref, step, rows, cols):
    tr, tc = o_ref.shape
    r = step * tr + lax.broadcasted_iota(jnp.int32, (tr, tc), 0)
    c = lax.broadcasted_iota(jnp.int32, (tr, tc), 1)
    inside = jnp.logical_and(r < rows, c < cols)
    o_ref[...] = jnp.where(inside, w_ref[...], 0.0).astype(o_ref.dtype)


def _ada_kernel(c_ref, w_ref, b_ref, o_ref):
    c = c_ref[...]
    ca = (c * jax.nn.sigmoid(c)).astype(BF16)
    o_ref[...] = jnp.dot(ca, w_ref[...].astype(BF16), preferred_element_type=F32) + b_ref[...]


def _ada(c, w, b):
    bsz, d = c.shape
    n = w.shape[1]
    rows = 8
    cp = jnp.zeros((rows, d), F32).at[:bsz].set(c)
    tn = _tile(n, 512)
    out = pl.pallas_call(
        _ada_kernel,
        grid=(n // tn,),
        in_specs=[pl.BlockSpec((rows, d), lambda j: (0, 0)),
                  pl.BlockSpec((d, tn), lambda j: (0, j)),
                  pl.BlockSpec((1, tn), lambda j: (0, j))],
        out_specs=pl.BlockSpec((rows, tn), lambda j: (0, j)),
        out_shape=jax.ShapeDtypeStruct((rows, n), F32),
        compiler_params=_cparams("parallel"),
        name="ada",
    )(cp, w, b.reshape(1, n))
    return out[:bsz]


def _rope_tab_kernel(pos_ref, inv_ref, sgn_ref, msk_ref, cos_ref, sin_ref):
    ang = pos_ref[...].astype(F32) * inv_ref[...]
    cos_ref[...] = jnp.cos(ang) * msk_ref[...]
    sin_ref[...] = jnp.sin(ang) * sgn_ref[...]


def _rope_tables(positions):
    m = positions.size
    inv_f = ROPE_THETA ** (-jnp.arange(0, HEAD_DIM, 2, dtype=F32) / HEAD_DIM)
    inv_r = ROPE_THETA ** (-jnp.arange(0, MLA_ROPE, 2, dtype=F32) / MLA_ROPE)
    zpad = jnp.zeros((LANE - MLA_ROPE,), F32)
    hf, hr = HEAD_DIM // 2, MLA_ROPE // 2
    inv = jnp.concatenate([inv_f, inv_f, inv_r, inv_r, zpad]).reshape(1, 2 * LANE)
    sgn = jnp.concatenate([-jnp.ones(hf), jnp.ones(hf), -jnp.ones(hr), jnp.ones(hr), zpad]).astype(F32).reshape(1, 2 * LANE)
    msk = jnp.concatenate([jnp.ones(HEAD_DIM + MLA_ROPE), zpad]).astype(F32).reshape(1, 2 * LANE)
    tm = _tile(m, 1024)
    row = pl.BlockSpec((1, 2 * LANE), lambda i: (0, 0))
    tab = pl.BlockSpec((tm, 2 * LANE), lambda i: (i, 0))
    return pl.pallas_call(
        _rope_tab_kernel,
        grid=(m // tm,),
        in_specs=[pl.BlockSpec((tm, 1), lambda i: (i, 0)), row, row, row],
        out_specs=[tab, tab],
        out_shape=[jax.ShapeDtypeStruct((m, 2 * LANE), F32)] * 2,
        compiler_params=_cparams("parallel"),
        name="rope_tables",
    )(positions.reshape(m, 1), inv, sgn, msk)


def _prenorm_kernel(x_ref, g_ref, sh_ref, sc_ref, o_ref):
    y = _rms(x_ref[...], g_ref[...])
    o_ref[...] = (y * (1.0 + sc_ref[...]) + sh_ref[...]).astype(o_ref.dtype)


def _prenorm(x2d, g, shift, scale, seq):
    m, d = x2d.shape
    bsz = m // seq
    tm = _tile(seq, 256)
    per_b = seq // tm
    vec = pl.BlockSpec((None, 1, d), lambda i: (i // per_b, 0, 0))
    return pl.pallas_call(
        _prenorm_kernel,
        grid=(m // tm,),
        in_specs=[pl.BlockSpec((tm, d), lambda i: (i, 0)),
                  pl.BlockSpec((1, d), lambda i: (0, 0)), vec, vec],
        out_specs=pl.BlockSpec((tm, d), lambda i: (i, 0)),
        out_shape=jax.ShapeDtypeStruct((m, d), BF16),
        compiler_params=_cparams("parallel"),
        name="prenorm",
    )(x2d, g.reshape(1, d), shift.reshape(bsz, 1, d), scale.reshape(bsz, 1, d))


def _dot_nt(a, wt):
    return lax.dot_general(a, wt, (((1,), (1,)), ((), ())), preferred_element_type=F32)


def _mm_nt_kernel(a_ref, wt_ref, o_ref):
    o_ref[...] = _dot_nt(a_ref[...], wt_ref[...]).astype(o_ref.dtype)


def _in_small(h, wt_s):
    m, d = h.shape
    n = -(-N_SMALL // LANE) * LANE
    tm = _tile(m, 512)
    return pl.pallas_call(
        _mm_nt_kernel,
        grid=(m // tm,),
        in_specs=[pl.BlockSpec((tm, d), lambda i: (i, 0)),
                  pl.BlockSpec((n, d), lambda i: (0, 0))],
        out_specs=pl.BlockSpec((tm, n), lambda i: (i, 0)),
        out_shape=jax.ShapeDtypeStruct((m, n), F32),
        compiler_params=_cparams("parallel"),
        name="in_small",
    )(h, wt_s)


def _rope128(x, c, s):
    return x * c + pltpu.roll(x, HEAD_DIM // 2, 1) * s


def _rope_mla(x, c, s):
    half = MLA_ROPE // 2
    return x * c + (pltpu.roll(x, LANE - half, 1) + pltpu.roll(x, half, 1)) * s


N_SMALL = MLA_Q_RANK + MLA_KV_RANK + MLA_ROPE


def _wt_rows_spec(d, tile_of):
    def index_map(*ids):
        row = jnp.asarray(N_SMALL + DIL_COLS * tile_of(*ids), jnp.int32)
        return (pl.multiple_of(row, MLA_ROPE), 0)
    return pl.BlockSpec((pl.Element(DIL_COLS), pl.Element(d)), index_map)


def _dil_proj_kernel(*refs, rope, scale, dil):
    if rope:
        h_ref, w_ref, cos_ref, sin_ref, o_ref = refs[:5]
    else:
        h_ref, w_ref, o_ref = refs[:3]
    acc = _dot_nt(h_ref[...], w_ref[...])
    tm = acc.shape[0]

    def finish(y, c, s):
        if rope:
            y = _rope128(y, c, s)
            if scale != 1.0:
                y = y * scale
        return y.astype(o_ref.dtype)

    if dil == 1:
        c, s = (cos_ref[...], sin_ref[...]) if rope else (None, None)
        for hh in range(DIL_HEADS):
            sl = slice(hh * HEAD_DIM, (hh + 1) * HEAD_DIM)
            o_ref[:, sl] = finish(acc[:, sl], c, s)
        return

    scr_ref = refs[-1]
    rows = tm // dil
    for hh in range(DIL_HEADS):
        scr_ref[hh] = acc[:, hh * HEAD_DIM:(hh + 1) * HEAD_DIM]
    for r in range(dil):
        take = pl.ds(r, rows, stride=dil)
        c, s = (cos_ref[take, :], sin_ref[take, :]) if rope else (None, None)
        for hh in range(DIL_HEADS):
            lo = r * DIL_COLS + hh * HEAD_DIM
            o_ref[:, lo:lo + HEAD_DIM] = finish(scr_ref[hh, take, :], c, s)


def _dil_proj(h, wt_dg, col_tile, dil, seq, cos_t=None, sin_t=None, scale=1.0):
    m, d = h.shape
    bsz = m // seq
    tm = _tile(seq, 1024)
    per_b = seq // tm
    rope = cos_t is not None
    in_specs = [pl.BlockSpec((tm, d), lambda i: (i, 0)),
                _wt_rows_spec(d, lambda i: col_tile)]
    args = [h, wt_dg]
    if rope:
        in_specs += [pl.BlockSpec((tm, LANE), lambda i: (i, 0))] * 2
        args += [cos_t, sin_t]
    scratch = [pltpu.VMEM((DIL_HEADS, tm, HEAD_DIM), F32)] if dil > 1 else []
    return pl.pallas_call(
        functools.partial(_dil_proj_kernel, rope=rope, scale=scale, dil=dil),
        grid=(m // tm,),
        in_specs=in_specs,
        out_specs=pl.BlockSpec((None, tm // dil, dil * DIL_COLS), lambda i: (i // per_b, i % per_b, 0)),
        out_shape=jax.ShapeDtypeStruct((bsz, seq // dil, dil * DIL_COLS), BF16),
        scratch_shapes=scratch,
        compiler_params=_cparams("parallel"),
        name=f"dil_proj_d{dil}",
    )(*args)


def _gate_proj_kernel(h_ref, w_ref, wa_ref, wb_ref, o_ref, oa_ref, ob_ref, *, n_i, ext_a, ext_b):
    acc = _dot_nt(h_ref[...], w_ref[...])
    o_ref[...] = jax.nn.sigmoid(acc).astype(o_ref.dtype)
    step = pl.program_id(0) * n_i + pl.program_id(1)
    _side_cast(wa_ref, oa_ref, step, *ext_a)
    _side_cast(wb_ref, ob_ref, step, *ext_b)


def _gate_proj(h, wt_dg, col_tile0, n, w_a, w_b, cols_padded):
    m, d = h.shape
    tm = _tile(m, 1024)
    tn = DIL_COLS
    n_j, n_i = n // tn, m // tm
    step_of = lambda j, i: j * n_i + i
    sides = [_side_cast_specs(w, w.shape[0], cols_padded, n_j * n_i, step_of) for w in (w_a, w_b)]
    return pl.pallas_call(
        functools.partial(_gate_proj_kernel, n_i=n_i, ext_a=w_a.shape, ext_b=w_b.shape),
        grid=(n_j, n_i),
        in_specs=[pl.BlockSpec((tm, d), lambda j, i: (i, 0)),
                  _wt_rows_spec(d, lambda j, i: col_tile0 + j)] + [s[0] for s in sides],
        out_specs=[pl.BlockSpec((tm, tn), lambda j, i: (i, j))] + [s[1] for s in sides],
        out_shape=[jax.ShapeDtypeStruct((m, n), BF16)] + [s[2] for s in sides],
        compiler_params=_cparams("arbitrary", "arbitrary"),
        name="gate_proj",
    )(h, wt_dg, w_a, w_b)


def _qup_kernel(z_ref, g_ref, w_ref, cos_ref, sin_ref, o_ref, zn_ref):
    @pl.when(pl.program_id(1) == 0)
    def _():
        zn_ref[...] = _rms(z_ref[...], g_ref[...]).astype(BF16)

    acc = jnp.dot(zn_ref[...], w_ref[...], preferred_element_type=F32)
    c = cos_ref[...]
    s = sin_ref[...]
    for hh in range(acc.shape[1] // MLA_QK_PAD):
        lo = hh * MLA_QK_PAD
        o_ref[:, lo:lo + LANE] = (acc[:, lo:lo + LANE] * MLA_QSCALE).astype(o_ref.dtype)
        rp = _rope_mla(acc[:, lo + LANE:lo + 2 * LANE], c, s)
        o_ref[:, lo + LANE:lo + 2 * LANE] = (rp * MLA_QSCALE).astype(o_ref.dtype)


def _qup(z_s, g, w_uq_pad, cos_t, sin_t):
    m = z_s.shape[0]
    k = MLA_Q_RANK
    n = w_uq_pad.shape[1]
    tm = _tile(m, 1024)
    tn = _tile(n, 1024)
    return pl.pallas_call(
        _qup_kernel,
        grid=(m // tm, n // tn),
        in_specs=[pl.BlockSpec((tm, k), lambda i, j: (i, 0)),
                  pl.BlockSpec((1, k), lambda i, j: (0, 0)),
                  pl.BlockSpec((k, tn), lambda i, j: (0, j)),
                  pl.BlockSpec((tm, LANE), lambda i, j: (i, 1)),
                  pl.BlockSpec((tm, LANE), lambda i, j: (i, 1))],
        out_specs=pl.BlockSpec((tm, tn), lambda i, j: (i, j)),
        out_shape=jax.ShapeDtypeStruct((m, n), BF16),
        scratch_shapes=[pltpu.VMEM((tm, k), BF16)],
        compiler_params=_cparams("parallel", "arbitrary"),
        name="q_up",
    )(z_s, g.reshape(1, k), w_uq_pad, cos_t, sin_t)


def _kvup_kernel(z_ref, kr_ref, g_ref, wk_ref, wvt_ref, cos_ref, sin_ref, k_ref, vt_ref, zn_ref):
    @pl.when(pl.program_id(1) == 0)
    def _():
        zn_ref[...] = _rms(z_ref[...], g_ref[...]).astype(BF16)

    zn = zn_ref[...]
    k_nope = jnp.dot(zn, wk_ref[...], preferred_element_type=F32)
    vt_ref[...] = _dot_nt(wvt_ref[...], zn).astype(vt_ref.dtype)
    kr = kr_ref[...]
    kr = jnp.where(lax.broadcasted_iota(jnp.int32, kr.shape, 1) < MLA_ROPE, kr, 0.0)
    kr = _rope_mla(kr, cos_ref[...], sin_ref[...]).astype(k_ref.dtype)
    for hh in range(k_nope.shape[1] // MLA_NOPE):
        ko = hh * MLA_QK_PAD
        k_ref[:, ko:ko + LANE] = k_nope[:, hh * MLA_NOPE:(hh + 1) * MLA_NOPE].astype(k_ref.dtype)
        k_ref[:, ko + LANE:ko + 2 * LANE] = kr


def _kvup(z_s, g, w_k, w_vt, cos_t, sin_t, seq):
    m = z_s.shape[0]
    bsz = m // seq
    k = MLA_KV_RANK
    tm = _tile(seq, 1024)
    per_b = seq // tm
    nh = 4
    kv_blk = MLA_Q_RANK // MLA_KV_RANK
    kr_blk = (MLA_Q_RANK + MLA_KV_RANK) // LANE
    return pl.pallas_call(
        _kvup_kernel,
        grid=(m // tm, MLA_HEADS // nh),
        in_specs=[pl.BlockSpec((tm, k), lambda i, j: (i, kv_blk)),
                  pl.BlockSpec((tm, LANE), lambda i, j: (i, kr_blk)),
                  pl.BlockSpec((1, k), lambda i, j: (0, 0)),
                  pl.BlockSpec((k, nh * MLA_NOPE), lambda i, j: (0, j)),
                  pl.BlockSpec((nh * MLA_V, k), lambda i, j: (j, 0)),
                  pl.BlockSpec((tm, LANE), lambda i, j: (i, 1)),
                  pl.BlockSpec((tm, LANE), lambda i, j: (i, 1))],
        out_specs=[pl.BlockSpec((tm, nh * MLA_QK_PAD), lambda i, j: (i, j)),
                   pl.BlockSpec((None, nh * MLA_V, tm), lambda i, j: (i // per_b, j, i % per_b))],
        out_shape=[jax.ShapeDtypeStruct((m, MLA_HEADS * MLA_QK_PAD), BF16),
                   jax.ShapeDtypeStruct((bsz, MLA_HEADS * MLA_V, seq), BF16)],
        scratch_shapes=[pltpu.VMEM((tm, k), BF16)],
        compiler_params=_cparams("parallel", "arbitrary"),
        name="kv_up",
    )(z_s, z_s, g.reshape(1, k), w_k, w_vt, cos_t, sin_t)


def _mla_kernel(q_ref, k_ref, vt_ref, o_ref, *, sub):
    n = q_ref.shape[0] // sub
    s_ts = [_dot_nt(k_ref[...], q_ref[i * sub:(i + 1) * sub, :]) for i in range(n)]
    ms = [jnp.max(s_t, axis=0, keepdims=True) for s_t in s_ts]
    for i in range(n):
        p = jnp.exp2(s_ts[i] - ms[i])
        l = jnp.sum(p, axis=0, keepdims=True)
        o_t = jnp.dot(vt_ref[...], p.astype(BF16), preferred_element_type=F32)
        o_ref[i * sub:(i + 1) * sub, :] = (o_t / l).T.astype(o_ref.dtype)


def _mla_attention(q, k, v_t, bsz, seq):
    q3 = q.reshape(bsz, seq, MLA_HEADS * MLA_QK_PAD)
    k3 = k.reshape(bsz, seq, MLA_HEADS * MLA_QK_PAD)
    tq = _tile(seq, 2048)
    out = pl.pallas_call(
        functools.partial(_mla_kernel, sub=min(tq, 1024)),
        grid=(bsz, MLA_HEADS, seq // tq),
        in_specs=[pl.BlockSpec((None, tq, MLA_QK_PAD), lambda b, h, i: (b, i, h)),
                  pl.BlockSpec((None, seq, MLA_QK_PAD), lambda b, h, i: (b, 0, h)),
                  pl.BlockSpec((None, MLA_V, seq), lambda b, h, i: (b, h, 0))],
        out_specs=pl.BlockSpec((None, tq, MLA_V), lambda b, h, i: (b, i, h)),
        out_shape=jax.ShapeDtypeStruct((bsz, seq, MLA_HEADS * MLA_V), BF16),
        compiler_params=_cparams("parallel", "parallel", "arbitrary"),
        name="mla_attn",
    )(q3, k3, v_t)
    return out.reshape(bsz * seq, MLA_HEADS * MLA_V)


def _dil_key_row0(ti, *, t_len, tb, kb, n_side):
    return jnp.clip(ti * tb - n_side, 0, t_len - kb)


def _dil_kernel(q_ref, k_ref, v_ref, o_ref, st_ref, *, t_len, tb, kb, tq, kw, n_side):
    ti = pl.program_id(2)
    k_row0 = _dil_key_row0(ti, t_len=t_len, tb=tb, kb=kb, n_side=n_side)
    lane = lax.broadcasted_iota(jnp.int32, (tq, LANE), 1)
    row = lax.broadcasted_iota(jnp.int32, (tq, kw), 0)
    col = lax.broadcasted_iota(jnp.int32, (tq, kw), 1)

    def tile(qi, carry):
        t0 = pl.multiple_of(qi * tq, tq)
        pos0 = ti * tb + t0
        ws = jnp.clip(pos0 - n_side, 0, t_len - kw)
        ws_loc = pl.multiple_of(ws - k_row0, n_side) if t_len > kw else 0
        valid = jnp.abs((pos0 + row) - (ws + col)) <= n_side
        stats = jnp.zeros((tq, LANE), F32)
        for hh in range(DIL_HEADS):
            sl = slice(hh * HEAD_DIM, (hh + 1) * HEAD_DIM)
            q = q_ref[pl.ds(t0, tq), sl]
            k = k_ref[0, pl.ds(ws_loc, kw), sl]
            v = v_ref[0, pl.ds(ws_loc, kw), sl]
            s = lax.dot_general(q, k, (((1,), (1,)), ((), ())), preferred_element_type=F32)
            s = jnp.where(valid, s, -jnp.inf)
            m = jnp.max(s, axis=-1, keepdims=True)
            p = jnp.exp2(s - m)
            l = jnp.sum(p, axis=-1, keepdims=True)
            o = jnp.dot(p.astype(BF16), v, preferred_element_type=F32)
            o_ref[pl.ds(t0, tq), sl] = (o / l).astype(o_ref.dtype)
            stats = jnp.where(lane == hh, m, stats)
            stats = jnp.where(lane == DIL_HEADS + hh, l, stats)
        st_ref[pl.ds(t0, tq), :] = stats
        return carry

    n_tiles = tb // tq
    lax.fori_loop(0, n_tiles, tile, 0, unroll=min(4, n_tiles))


def _dil_rows_per_block(t_len):
    return t_len if t_len <= 2048 else 2048


def _dilated_group(q3, k3, v3, dil, window):
    bsz, t_len, _ = q3.shape
    n_side = window // (2 * dil)
    tq = min(2 * n_side, t_len)
    kw = min(4 * n_side, t_len)
    tb = _dil_rows_per_block(t_len)
    kb = min(tb + 2 * n_side, t_len)
    geom = dict(t_len=t_len, tb=tb, kb=kb, n_side=n_side)

    def kv_index(b, r, ti):
        row0 = _dil_key_row0(ti, **geom)
        return (b, pl.multiple_of(row0, n_side) if kb < t_len else 0, r * DIL_COLS)

    kv_spec = pl.BlockSpec((pl.Element(1), pl.Element(kb), pl.Element(DIL_COLS)), kv_index)
    row_spec = pl.BlockSpec((None, tb, DIL_COLS), lambda b, r, ti: (b, ti, r))

    o, st = pl.pallas_call(
        functools.partial(_dil_kernel, tq=tq, kw=kw, **geom),
        grid=(bsz, dil, t_len // tb),
        in_specs=[row_spec, kv_spec, kv_spec],
        out_specs=[row_spec, pl.BlockSpec((None, tb, LANE), lambda b, r, ti: (b, ti, r))],
        out_shape=[jax.ShapeDtypeStruct((bsz, t_len, dil * DIL_COLS), BF16),
                   jax.ShapeDtypeStruct((bsz, t_len, dil * LANE), F32)],
        compiler_params=_cparams("parallel", "parallel", "parallel"),
        name=f"dilated_d{dil}",
    )(q3, k3, v3)
    return o, st, DIL_HEADS


def _merge_kernel(*refs, hbs, dils):
    ng = len(hbs)
    o_refs, st_refs, y_ref = refs[:ng], refs[ng:2 * ng], refs[2 * ng]
    o_scr, st_scr = refs[2 * ng + 1:]
    tm = y_ref.shape[0]

    for g in range(ng):
        d = dils[g]
        if d == 1:
            continue
        rows = tm // d
        nblk = DIL_HEADS // hbs[g]
        for r in range(d):
            put = pl.ds(r, rows, stride=d)
            for hh in range(DIL_HEADS):
                lo = r * DIL_COLS + hh * HEAD_DIM
                o_scr[g, hh, put, :] = o_refs[g][:, lo:lo + HEAD_DIM].astype(F32)
            for blk in range(nblk):
                lo = (r * nblk + blk) * LANE
                st_scr[g, blk, put, :] = st_refs[g][:, lo:lo + LANE]

    def o_tile(g, h):
        if dils[g] == 1:
            return o_refs[g][:, h * HEAD_DIM:(h + 1) * HEAD_DIM].astype(F32)
        return o_scr[g, h]

    def st_tile(g):
        nblk = DIL_HEADS // hbs[g]
        if dils[g] == 1:
            tiles = [st_refs[g][:, blk * LANE:(blk + 1) * LANE] for blk in range(nblk)]
        else:
            tiles = [st_scr[g, blk] for blk in range(nblk)]
        return functools.reduce(lambda u, w: u + w, tiles)

    sts = [st_tile(g) for g in range(ng)]
    m_all = functools.reduce(jnp.maximum, sts)
    a = [jnp.exp2(sts[g] - m_all) * pltpu.roll(sts[g], LANE - DIL_HEADS, 1) for g in range(ng)]
    den = functools.reduce(lambda u, w: u + w, a)
    head_lane = lax.broadcasted_iota(jnp.int32, (tm, LANE), 1) < DIL_HEADS
    spread = (lax.broadcasted_iota(jnp.int32, (LANE, DIL_COLS), 0)
              == jnp.right_shift(lax.broadcasted_iota(jnp.int32, (LANE, DIL_COLS), 1),
                                 HEAD_DIM.bit_length() - 1)).astype(BF16)
    coef = []
    for g in range(ng):
        cg = jnp.where(head_lane, a[g] / den, 0.0)
        hi = cg.astype(BF16)
        lo = (cg - hi.astype(F32)).astype(BF16)
        coef.append(jnp.dot(hi, spread, preferred_element_type=F32)
                    + jnp.dot(lo, spread, preferred_element_type=F32))
    for h in range(DIL_HEADS):
        sl = slice(h * HEAD_DIM, (h + 1) * HEAD_DIM)
        y = None
        for g in range(ng):
            term = coef[g][:, sl] * o_tile(g, h)
            y = term if y is None else y + term
        y_ref[:, sl] = y.astype(y_ref.dtype)


def _merge_groups(os_, sts, hbs, dils, seq):
    bsz = os_[0].shape[0]
    m = bsz * seq
    tm = _tile(seq, 512)
    per_b = seq // tm

    def spec(arr, d):
        return pl.BlockSpec((None, tm // d, arr.shape[2]), lambda i: (i // per_b, i % per_b, 0))

    in_specs = [spec(o, d) for o, d in zip(os_, dils)] + [spec(s, d) for s, d in zip(sts, dils)]
    ng = len(os_)
    max_blk = max(DIL_HEADS // hb for hb in hbs)
    return pl.pallas_call(
        functools.partial(_merge_kernel, hbs=tuple(hbs), dils=tuple(dils)),
        grid=(m // tm,),
        in_specs=in_specs,
        out_specs=pl.BlockSpec((tm, DIL_COLS), lambda i: (i, 0)),
        out_shape=jax.ShapeDtypeStruct((m, DIL_COLS), BF16),
        scratch_shapes=[pltpu.VMEM((ng, DIL_HEADS, tm, HEAD_DIM), F32),
                        pltpu.VMEM((ng, max_blk, tm, LANE), F32)],
        compiler_params=_cparams("parallel"),
        name="dil_merge",
    )(*os_, *sts)


def _mix_kernel(ya_ref, yb_ref, wa_ref, wb_ref, ga_ref, gb_ref, o_ref):
    pa = jnp.dot(ya_ref[...], wa_ref[...], preferred_element_type=F32)
    pb = jnp.dot(yb_ref[...], wb_ref[...], preferred_element_type=F32)
    o_ref[...] = (ga_ref[...].astype(F32) * pa + gb_ref[...].astype(F32) * pb).astype(o_ref.dtype)


def _mix(ya, yb, w_pa, w_pb, gates):
    m = ya.shape[0]
    d = w_pa.shape[1]
    tm = _tile(m, 1024)
    tn = _tile(d, 1024)
    ga0 = 0
    gb0 = d // tn
    return pl.pallas_call(
        _mix_kernel,
        grid=(m // tm, d // tn),
        in_specs=[pl.BlockSpec((tm, ya.shape[1]), lambda i, j: (i, 0)),
                  pl.BlockSpec((tm, yb.shape[1]), lambda i, j: (i, 0)),
                  pl.BlockSpec((w_pa.shape[0], tn), lambda i, j: (0, j)),
                  pl.BlockSpec((w_pb.shape[0], tn), lambda i, j: (0, j)),
                  pl.BlockSpec((tm, tn), lambda i, j: (i, ga0 + j)),
                  pl.BlockSpec((tm, tn), lambda i, j: (i, gb0 + j))],
        out_specs=pl.BlockSpec((tm, tn), lambda i, j: (i, j)),
        out_shape=jax.ShapeDtypeStruct((m, d), BF16),
        compiler_params=_cparams("parallel", "arbitrary"),
        name="mix",
    )(ya, yb, w_pa, w_pb, gates, gates)


def _outproj_kernel(a_ref, w_ref, x_ref, g_ref, o_ref):
    acc = jnp.dot(a_ref[...], w_ref[...], preferred_element_type=F32)
    o_ref[...] = x_ref[...] + g_ref[...] * acc


def _outproj(a, w, x2d, gate, seq):
    m, k = a.shape
    d = w.shape[1]
    bsz = m // seq
    tm = _tile(seq, 1024)
    tn = _tile(d, 512)
    per_b = seq // tm
    return pl.pallas_call(
        _outproj_kernel,
        grid=(m // tm, d // tn),
        in_specs=[pl.BlockSpec((tm, k), lambda i, j: (i, 0)),
                  pl.BlockSpec((k, tn), lambda i, j: (0, j)),
                  pl.BlockSpec((tm, tn), lambda i, j: (i, j)),
                  pl.BlockSpec((None, 1, tn), lambda i, j: (i // per_b, 0, j))],
        out_specs=pl.BlockSpec((tm, tn), lambda i, j: (i, j)),
        out_shape=jax.ShapeDtypeStruct((m, d), F32),
        compiler_params=_cparams("parallel", "arbitrary"),
        name="out_proj",
    )(a, w, x2d, gate.reshape(bsz, 1, d))


def _ffn_up_kernel(h_ref, wg_ref, wu_ref, wd_ref, o_ref, od_ref, *, n_j, ext_d):
    h = h_ref[...]
    g = jnp.dot(h, wg_ref[...], preferred_element_type=F32)
    u = jnp.dot(h, wu_ref[...], preferred_element_type=F32)
    o_ref[...] = (g * jax.nn.sigmoid(g) * u).astype(o_ref.dtype)
    _side_cast(wd_ref, od_ref, pl.program_id(0) * n_j + pl.program_id(1), *ext_d)


def _ffn_up(h, wg, wu, w_down):
    m, d = h.shape
    f = wg.shape[1]
    tm = _tile(m, 1024)
    tn = _tile(f, 512)
    n_i, n_j = m // tm, f // tn
    side = _side_cast_specs(w_down, f, w_down.shape[1], n_i * n_j, lambda i, j: i * n_j + j)
    return pl.pallas_call(
        functools.partial(_ffn_up_kernel, n_j=n_j, ext_d=w_down.shape),
        grid=(n_i, n_j),
        in_specs=[pl.BlockSpec((tm, d), lambda i, j: (i, 0)),
                  pl.BlockSpec((d, tn), lambda i, j: (0, j)),
                  pl.BlockSpec((d, tn), lambda i, j: (0, j)),
                  side[0]],
        out_specs=[pl.BlockSpec((tm, tn), lambda i, j: (i, j)), side[1]],
        out_shape=[jax.ShapeDtypeStruct((m, f), BF16), side[2]],
        compiler_params=_cparams("arbitrary", "arbitrary"),
        name="ffn_up",
    )(h, wg, wu, w_down)


def _ffn_down_kernel(a_ref, w_ref, x_ref, g_ref, fg_ref, o_ref, *, nk, final_norm):
    k = pl.program_id(1)

    @pl.when(k == 0)
    def _():
        o_ref[...] = jnp.dot(a_ref[...], w_ref[...], preferred_element_type=F32)

    @pl.when(k > 0)
    def _():
        o_ref[...] = jnp.dot(a_ref[...], w_ref[...], preferred_element_type=F32) + o_ref[...]

    @pl.when(k == nk - 1)
    def _():
        x2 = x_ref[...] + g_ref[...] * o_ref[...]
        o_ref[...] = _rms(x2, fg_ref[...]) if final_norm else x2


def _ffn_down(a, w, x2d, gate, final_g, seq, final_norm):
    m, f = a.shape
    d = w.shape[1]
    bsz = m // seq
    tm = _tile(seq, 512)
    tk = _tile(f, 1024)
    nk = f // tk
    per_b = seq // tm
    return pl.pallas_call(
        functools.partial(_ffn_down_kernel, nk=nk, final_norm=final_norm),
        grid=(m // tm, nk),
        in_specs=[pl.BlockSpec((tm, tk), lambda i, k: (i, k)),
                  pl.BlockSpec((tk, d), lambda i, k: (k, 0)),
                  pl.BlockSpec((tm, d), lambda i, k: (i, 0), pipeline_mode=pl.Buffered(1)),
                  pl.BlockSpec((None, 1, d), lambda i, k: (i // per_b, 0, 0)),
                  pl.BlockSpec((1, d), lambda i, k: (0, 0))],
        out_specs=pl.BlockSpec((tm, d), lambda i, k: (i, 0)),
        out_shape=jax.ShapeDtypeStruct((m, d), F32),
        compiler_params=_cparams("parallel", "arbitrary"),
        name="ffn_down",
    )(a, w, x2d, gate.reshape(bsz, 1, d), final_g.reshape(1, d))


FF_MULT = 1024


def kernel(x, c, positions, w_ada, b_ada, norm1_g, w_in, q_norm_g, w_uq, kv_norm_g, w_ukv,
           w_proj_a, w_proj_b, w_out, norm2_g, w_gate, w_up, w_down, final_g):
    bsz, seq, d = x.shape
    depth = w_ada.shape[0]
    m = bsz * seq
    cos_t, sin_t = _rope_tables(positions)
    x2d = x.reshape(m, d)

    for l in range(depth):
        mod = _ada(c, w_ada[l], b_ada[l])
        sh1, sc1, g1, sh2, sc2, g2 = jnp.split(mod, N_MOD, axis=-1)

        w_dg = w_in[l].T.astype(BF16)
        h = _prenorm(x2d, norm1_g[l], sh1, sc1, seq)
        z_s = _in_small(h, w_dg)
        f_pad = -(-w_gate.shape[2] // FF_MULT) * FF_MULT
        gates, wg, wu = _gate_proj(h, w_dg, 3 * DIL_GROUPS, 2 * d, w_gate[l], w_up[l], f_pad)

        w_uq_pad = jnp.pad(w_uq[l].reshape(MLA_Q_RANK, MLA_HEADS, MLA_NOPE + MLA_ROPE),
                           ((0, 0), (0, 0), (0, MLA_QK_PAD - MLA_NOPE - MLA_ROPE)))
        w_uq_pad = w_uq_pad.reshape(MLA_Q_RANK, MLA_HEADS * MLA_QK_PAD).astype(BF16)
        q = _qup(z_s, q_norm_g[l], w_uq_pad, cos_t, sin_t)
        w_kv = w_ukv[l].reshape(MLA_KV_RANK, MLA_HEADS, MLA_NOPE + MLA_V)
        w_k = w_kv[:, :, :MLA_NOPE].reshape(MLA_KV_RANK, MLA_HEADS * MLA_NOPE).astype(BF16)
        w_vt = w_kv[:, :, MLA_NOPE:].reshape(MLA_KV_RANK, MLA_HEADS * MLA_V).T.astype(BF16)
        k, v_t = _kvup(z_s, kv_norm_g[l], w_k, w_vt, cos_t, sin_t, seq)
        y_a = _mla_attention(q, k, v_t, bsz, seq)

        os_, sts, hbs, dils = [], [], [], []
        for g, (window, dil) in enumerate(DIL_PATTERNS):
            q3 = _dil_proj(h, w_dg, g, dil, seq, cos_t, sin_t, DIL_QSCALE)
            k3 = _dil_proj(h, w_dg, DIL_GROUPS + g, dil, seq, cos_t, sin_t)
            v3 = _dil_proj(h, w_dg, 2 * DIL_GROUPS + g, dil, seq)
            o_g, st_g, hb = _dilated_group(q3, k3, v3, dil, window)
            os_.append(o_g)
            sts.append(st_g)
            hbs.append(hb)
            dils.append(dil)
        y_b = _merge_groups(os_, sts, hbs, dils, seq)

        mixed = _mix(y_a, y_b, w_proj_a[l].astype(BF16), w_proj_b[l].astype(BF16), gates)
        x2d = _outproj(mixed, w_out[l].astype(BF16), x2d, g1, seq)

        h2 = _prenorm(x2d, norm2_g[l], sh2, sc2, seq)
        hmid, wd = _ffn_up(h2, wg, wu, w_down[l])
        x2d = _ffn_down(hmid, wd, x2d, g2, final_g, seq, final_norm=(l == depth - 1))

    return x2d.reshape(bsz, seq, d)
```

```python
import functools
import math

import jax
import jax.numpy as jnp
from jax import lax
from jax.experimental import pallas as pl
from jax.experimental.pallas import tpu as pltpu

F32 = jnp.float32
BF16 = jnp.bfloat16

NORM_EPS = 1e-6
ROPE_THETA = 10000.0
HEAD_DIM = 128
MLA_HEADS = 16
MLA_Q_RANK = 1024
MLA_KV_RANK = 512
MLA_NOPE = 128
MLA_ROPE = 64
MLA_V = 128
DIL_HEADS = 8
DIL_PATTERNS = ((128, 1), (512, 4), (2048, 16))
DIL_GROUPS = len(DIL_PATTERNS)
N_MOD = 6
LOG2E = 1.4426950408889634

LANE = 128
MLA_QK_PAD = 256
DIL_COLS = DIL_HEADS * HEAD_DIM
VMEM_LIMIT = 56 * 1024 * 1024

MLA_QSCALE = (MLA_NOPE + MLA_ROPE) ** -0.5 * LOG2E
DIL_QSCALE = HEAD_DIM ** -0.5 * LOG2E


def _cparams(*sem):
    return pltpu.CompilerParams(dimension_semantics=sem, vmem_limit_bytes=VMEM_LIMIT)


def _tile(n, pref):
    return pref if n % pref == 0 else n


def _rms(x, g):
    return x * lax.rsqrt(jnp.mean(x * x, axis=-1, keepdims=True) + NORM_EPS) * g


def _side_cast_rows(rows_padded, n_steps):
    tr = rows_padded // n_steps
    assert tr * n_steps == rows_padded and tr % 16 == 0, (rows_padded, n_steps)
    return tr


def _side_cast_specs(w, rows_padded, cols_padded, n_steps, step_of):
    tr = _side_cast_rows(rows_padded, n_steps)
    last = (w.shape[0] - 1) // tr

    def in_map(*ids):
        return (jnp.minimum(step_of(*ids), last), 0)

    def out_map(*ids):
        return (step_of(*ids), 0)

    return (pl.BlockSpec((tr, cols_padded), in_map), pl.BlockSpec((tr, cols_padded), out_map),
            jax.ShapeDtypeStruct((rows_padded, cols_padded), BF16))


def _side_cast(w_ref, o_ref, step, rows, cols):
    tr, tc = o_ref.shape
    r = step * tr + lax.broadcasted_iota(jnp.int32, (tr, tc), 0)
    c = lax.broadcasted_iota(jnp.int32, (tr, tc), 1)
    inside = jnp.logical_and(r < rows, c < cols)
    o_ref[...] = jnp.where(inside, w_ref[...], 0.0).astype(o_ref.dtype)


def _ada_kernel(c_ref, w_ref, b_ref, o_ref):
    c = c_ref[...]
    ca = (c * jax.nn.sigmoid(c)).astype(BF16)
    o_ref[...] = jnp.dot(ca, w_ref[...].astype(BF16), preferred_element_type=F32) + b_ref[...]


def _ada(c, w, b):
    bsz, d = c.shape
    n = w.shape[1]
    rows = 8
    cp = jnp.zeros((rows, d), F32).at[:bsz].set(c)
    tn = _tile(n, 512)
    out = pl.pallas_call(
        _ada_kernel,
        grid=(n // tn,),
        in_specs=[pl.BlockSpec((rows, d), lambda j: (0, 0)),
                  pl.BlockSpec((d, tn), lambda j: (0, j)),
                  pl.BlockSpec((1, tn), lambda j: (0, j))],
        out_specs=pl.BlockSpec((rows, tn), lambda j: (0, j)),
        out_shape=jax.ShapeDtypeStruct((rows, n), F32),
        compiler_params=_cparams("parallel"),
        name="ada",
    )(cp, w, b.reshape(1, n))
    return out[:bsz]


def _rope_tab_kernel(pos_ref, inv_ref, sgn_ref, msk_ref, cos_ref, sin_ref):
    ang = pos_ref[...].astype(F32) * inv_ref[...]
    cos_ref[...] = jnp.cos(ang) * msk_ref[...]
    sin_ref[...] = jnp.sin(ang) * sgn_ref[...]


def _rope_tables(positions):
    m = positions.size
    inv_f = ROPE_THETA ** (-jnp.arange(0, HEAD_DIM, 2, dtype=F32) / HEAD_DIM)
    inv_r = ROPE_THETA ** (-jnp.arange(0, MLA_ROPE, 2, dtype=F32) / MLA_ROPE)
    zpad = jnp.zeros((LANE - MLA_ROPE,), F32)
    hf, hr = HEAD_DIM // 2, MLA_ROPE // 2
    inv = jnp.concatenate([inv_f, inv_f, inv_r, inv_r, zpad]).reshape(1, 2 * LANE)
    sgn = jnp.concatenate([-jnp.ones(hf), jnp.ones(hf), -jnp.ones(hr), jnp.ones(hr), zpad]).astype(F32).reshape(1, 2 * LANE)
    msk = jnp.concatenate([jnp.ones(HEAD_DIM + MLA_ROPE), zpad]).astype(F32).reshape(1, 2 * LANE)
    tm = _tile(m, 1024)
    row = pl.BlockSpec((1, 2 * LANE), lambda i: (0, 0))
    tab = pl.BlockSpec((tm, 2 * LANE), lambda i: (i, 0))
    return pl.pallas_call(
        _rope_tab_kernel,
        grid=(m // tm,),
        in_specs=[pl.BlockSpec((tm, 1), lambda i: (i, 0)), row, row, row],
        out_specs=[tab, tab],
        out_shape=[jax.ShapeDtypeStruct((m, 2 * LANE), F32)] * 2,
        compiler_params=_cparams("parallel"),
        name="rope_tables",
    )(positions.reshape(m, 1), inv, sgn, msk)


def _prenorm_kernel(x_ref, g_ref, sh_ref, sc_ref, o_ref):
    y = _rms(x_ref[...], g_ref[...])
    o_ref[...] = (y * (1.0 + sc_ref[...]) + sh_ref[...]).astype(o_ref.dtype)


def _prenorm(x2d, g, shift, scale, seq):
    m, d = x2d.shape
    bsz = m // seq
    tm = _tile(seq, 256)
    per_b = seq // tm
    vec = pl.BlockSpec((None, 1, d), lambda i: (i // per_b, 0, 0))
    return pl.pallas_call(
        _prenorm_kernel,
        grid=(m // tm,),
        in_specs=[pl.BlockSpec((tm, d), lambda i: (i, 0)),
                  pl.BlockSpec((1, d), lambda i: (0, 0)), vec, vec],
        out_specs=pl.BlockSpec((tm, d), lambda i: (i, 0)),
        out_shape=jax.ShapeDtypeStruct((m, d), BF16),
        compiler_params=_cparams("parallel"),
        name="prenorm",
    )(x2d, g.reshape(1, d), shift.reshape(bsz, 1, d), scale.reshape(bsz, 1, d))


def _dot_nt(a, wt):
    return lax.dot_general(a, wt, (((1,), (1,)), ((), ())), preferred_element_type=F32)


N_SMALL = MLA_Q_RANK + MLA_KV_RANK + MLA_ROPE
N_SMALL_PAD = -(-N_SMALL // LANE) * LANE


def _in_small_kernel(a_ref, wt_ref, side_ref, o_ref, oside_ref):
    o_ref[...] = _dot_nt(a_ref[...], wt_ref[...])
    oside_ref[...] = side_ref[...].astype(oside_ref.dtype)


def _in_small(h, wt_s, wt):
    m, d = h.shape
    n = wt_s.shape[0]
    tm = _tile(m, 256)
    n_steps = m // tm
    rows = wt.shape[0] - N_SMALL
    tr = _side_cast_rows(rows, n_steps)

    def side_map(i):
        return (pl.multiple_of(N_SMALL + tr * i, 8), 0)

    return pl.pallas_call(
        _in_small_kernel,
        grid=(n_steps,),
        in_specs=[pl.BlockSpec((tm, d), lambda i: (i, 0)),
                  pl.BlockSpec((n, d), lambda i: (0, 0)),
                  pl.BlockSpec((pl.Element(tr), pl.Element(d)), side_map)],
        out_specs=[pl.BlockSpec((tm, n), lambda i: (i, 0)),
                   pl.BlockSpec((tr, d), lambda i: (i, 0))],
        out_shape=[jax.ShapeDtypeStruct((m, n), F32),
                   jax.ShapeDtypeStruct((rows, d), BF16)],
        compiler_params=_cparams("arbitrary"),
        name="in_small",
    )(h, wt_s, wt)


def _rope128(x, c, s):
    return x * c + pltpu.roll(x, HEAD_DIM // 2, 1) * s


def _rope_mla(x, c, s):
    half = MLA_ROPE // 2
    return x * c + (pltpu.roll(x, LANE - half, 1) + pltpu.roll(x, half, 1)) * s


def _dil_proj_kernel(*refs, rope, scale, dil):
    if rope:
        h_ref, w_ref, cos_ref, sin_ref, o_ref = refs[:5]
    else:
        h_ref, w_ref, o_ref = refs[:3]
    acc = _dot_nt(h_ref[...], w_ref[...])
    tm = acc.shape[0]

    def finish(y, c, s):
        if rope:
            y = _rope128(y, c, s)
            if scale != 1.0:
                y = y * scale
        return y.astype(o_ref.dtype)

    if dil == 1:
        c, s = (cos_ref[...], sin_ref[...]) if rope else (None, None)
        for hh in range(DIL_HEADS):
            sl = slice(hh * HEAD_DIM, (hh + 1) * HEAD_DIM)
            o_ref[:, sl] = finish(acc[:, sl], c, s)
        return

    scr_ref = refs[-1]
    rows = tm // dil
    for hh in range(DIL_HEADS):
        scr_ref[hh] = acc[:, hh * HEAD_DIM:(hh + 1) * HEAD_DIM]
    for r in range(dil):
        take = pl.ds(r, rows, stride=dil)
        c, s = (cos_ref[take, :], sin_ref[take, :]) if rope else (None, None)
        for hh in range(DIL_HEADS):
            lo = r * DIL_COLS + hh * HEAD_DIM
            o_ref[:, lo:lo + HEAD_DIM] = finish(scr_ref[hh, take, :], c, s)


def _dil_proj(h, wt_dg, col_tile, dil, seq, cos_t=None, sin_t=None, scale=1.0):
    m, d = h.shape
    bsz = m // seq
    tm = _tile(seq, 1024)
    per_b = seq // tm
    rope = cos_t is not None
    in_specs = [pl.BlockSpec((tm, d), lambda i: (i, 0)),
                pl.BlockSpec((DIL_COLS, d), lambda i: (col_tile, 0))]
    args = [h, wt_dg]
    if rope:
        in_specs += [pl.BlockSpec((tm, LANE), lambda i: (i, 0))] * 2
        args += [cos_t, sin_t]
    scratch = [pltpu.VMEM((DIL_HEADS, tm, HEAD_DIM), F32)] if dil > 1 else []
    return pl.pallas_call(
        functools.partial(_dil_proj_kernel, rope=rope, scale=scale, dil=dil),
        grid=(m // tm,),
        in_specs=in_specs,
        out_specs=pl.BlockSpec((None, tm // dil, dil * DIL_COLS), lambda i: (i // per_b, i % per_b, 0)),
        out_shape=jax.ShapeDtypeStruct((bsz, seq // dil, dil * DIL_COLS), BF16),
        scratch_shapes=scratch,
        compiler_params=_cparams("parallel"),
        name=f"dil_proj_d{dil}",
    )(*args)


def _gate_proj_kernel(*refs, n_i, extents):
    ns = len(extents)
    h_ref, w_ref = refs[:2]
    side_in, o_ref, side_out = refs[2:2 + ns], refs[2 + ns], refs[3 + ns:]
    acc = _dot_nt(h_ref[...], w_ref[...])
    o_ref[...] = jax.nn.sigmoid(acc).astype(o_ref.dtype)
    step = pl.program_id(0) * n_i + pl.program_id(1)
    for w_in_ref, w_out_ref, ext in zip(side_in, side_out, extents):
        _side_cast(w_in_ref, w_out_ref, step, *ext)


def _gate_proj(h, wt_dg, col_tile0, n, side_weights):
    m, d = h.shape
    tm = _tile(m, 1024)
    tn = DIL_COLS
    n_j, n_i = n // tn, m // tm
    step_of = lambda j, i: j * n_i + i
    sides = [_side_cast_specs(w, w.shape[0], cp, n_j * n_i, step_of) for w, cp in side_weights]
    return pl.pallas_call(
        functools.partial(_gate_proj_kernel, n_i=n_i, extents=tuple(w.shape for w, _ in side_weights)),
        grid=(n_j, n_i),
        in_specs=[pl.BlockSpec((tm, d), lambda j, i: (i, 0)),
                  pl.BlockSpec((tn, d), lambda j, i: (col_tile0 + j, 0))] + [s[0] for s in sides],
        out_specs=[pl.BlockSpec((tm, tn), lambda j, i: (i, j))] + [s[1] for s in sides],
        out_shape=[jax.ShapeDtypeStruct((m, n), BF16)] + [s[2] for s in sides],
        compiler_params=_cparams("arbitrary", "arbitrary"),
        name="gate_proj",
    )(h, wt_dg, *[w for w, _ in side_weights])


def _qup_kernel(z_ref, g_ref, w_ref, cos_ref, sin_ref, o_ref, zn_ref):
    @pl.when(pl.program_id(1) == 0)
    def _():
        zn_ref[...] = _rms(z_ref[...], g_ref[...]).astype(BF16)

    acc = jnp.dot(zn_ref[...], w_ref[...], preferred_element_type=F32)
    c = cos_ref[...]
    s = sin_ref[...]
    for hh in range(acc.shape[1] // MLA_QK_PAD):
        lo = hh * MLA_QK_PAD
        o_ref[:, lo:lo + LANE] = (acc[:, lo:lo + LANE] * MLA_QSCALE).astype(o_ref.dtype)
        rp = _rope_mla(acc[:, lo + LANE:lo + 2 * LANE], c, s)
        o_ref[:, lo + LANE:lo + 2 * LANE] = (rp * MLA_QSCALE).astype(o_ref.dtype)


def _qup(z_s, g, w_uq_pad, cos_t, sin_t):
    m = z_s.shape[0]
    k = MLA_Q_RANK
    n = w_uq_pad.shape[1]
    tm = _tile(m, 1024)
    tn = _tile(n, 1024)
    return pl.pallas_call(
        _qup_kernel,
        grid=(m // tm, n // tn),
        in_specs=[pl.BlockSpec((tm, k), lambda i, j: (i, 0)),
                  pl.BlockSpec((1, k), lambda i, j: (0, 0)),
                  pl.BlockSpec((k, tn), lambda i, j: (0, j)),
                  pl.BlockSpec((tm, LANE), lambda i, j: (i, 1)),
                  pl.BlockSpec((tm, LANE), lambda i, j: (i, 1))],
        out_specs=pl.BlockSpec((tm, tn), lambda i, j: (i, j)),
        out_shape=jax.ShapeDtypeStruct((m, n), BF16),
        scratch_shapes=[pltpu.VMEM((tm, k), BF16)],
        compiler_params=_cparams("parallel", "arbitrary"),
        name="q_up",
    )(z_s, g.reshape(1, k), w_uq_pad, cos_t, sin_t)


def _kvup_kernel(z_ref, kr_ref, g_ref, wk_ref, wvt_ref, cos_ref, sin_ref, k_ref, vt_ref, zn_ref):
    @pl.when(pl.program_id(1) == 0)
    def _():
        zn_ref[...] = _rms(z_ref[...], g_ref[...]).astype(BF16)

    zn = zn_ref[...]
    k_nope = jnp.dot(zn, wk_ref[...], preferred_element_type=F32)
    vt_ref[...] = _dot_nt(wvt_ref[...], zn).astype(vt_ref.dtype)
    kr = kr_ref[...]
    kr = jnp.where(lax.broadcasted_iota(jnp.int32, kr.shape, 1) < MLA_ROPE, kr, 0.0)
    kr = _rope_mla(kr, cos_ref[...], sin_ref[...]).astype(k_ref.dtype)
    for hh in range(k_nope.shape[1] // MLA_NOPE):
        ko = hh * MLA_QK_PAD
        k_ref[:, ko:ko + LANE] = k_nope[:, hh * MLA_NOPE:(hh + 1) * MLA_NOPE].astype(k_ref.dtype)
        k_ref[:, ko + LANE:ko + 2 * LANE] = kr


def _kvup(z_s, g, w_k, w_vt, cos_t, sin_t, seq):
    m = z_s.shape[0]
    bsz = m // seq
    k = MLA_KV_RANK
    tm = _tile(seq, 1024)
    per_b = seq // tm
    nh = 4
    kv_blk = MLA_Q_RANK // MLA_KV_RANK
    kr_blk = (MLA_Q_RANK + MLA_KV_RANK) // LANE
    return pl.pallas_call(
        _kvup_kernel,
        grid=(m // tm, MLA_HEADS // nh),
        in_specs=[pl.BlockSpec((tm, k), lambda i, j: (i, kv_blk)),
                  pl.BlockSpec((tm, LANE), lambda i, j: (i, kr_blk)),
                  pl.BlockSpec((1, k), lambda i, j: (0, 0)),
                  pl.BlockSpec((k, nh * MLA_NOPE), lambda i, j: (0, j)),
                  pl.BlockSpec((nh * MLA_V, k), lambda i, j: (j, 0)),
                  pl.BlockSpec((tm, LANE), lambda i, j: (i, 1)),
                  pl.BlockSpec((tm, LANE), lambda i, j: (i, 1))],
        out_specs=[pl.BlockSpec((tm, nh * MLA_QK_PAD), lambda i, j: (i, j)),
                   pl.BlockSpec((None, nh * MLA_V, tm), lambda i, j: (i // per_b, j, i % per_b))],
        out_shape=[jax.ShapeDtypeStruct((m, MLA_HEADS * MLA_QK_PAD), BF16),
                   jax.ShapeDtypeStruct((bsz, MLA_HEADS * MLA_V, seq), BF16)],
        scratch_shapes=[pltpu.VMEM((tm, k), BF16)],
        compiler_params=_cparams("parallel", "arbitrary"),
        name="kv_up",
    )(z_s, z_s, g.reshape(1, k), w_k, w_vt, cos_t, sin_t)


def _mla_kernel(q_ref, k_ref, vt_ref, o_ref, *, sub):
    n = q_ref.shape[0] // sub
    s_ts = [_dot_nt(k_ref[...], q_ref[i * sub:(i + 1) * sub, :]) for i in range(n)]
    ms = [jnp.max(s_t, axis=0, keepdims=True) for s_t in s_ts]
    for i in range(n):
        p = jnp.exp2(s_ts[i] - ms[i])
        l = jnp.sum(p, axis=0, keepdims=True)
        o_t = jnp.dot(vt_ref[...], p.astype(BF16), preferred_element_type=F32)
        o_ref[i * sub:(i + 1) * sub, :] = (o_t / l).T.astype(o_ref.dtype)


def _mla_attention(q, k, v_t, bsz, seq):
    q3 = q.reshape(bsz, seq, MLA_HEADS * MLA_QK_PAD)
    k3 = k.reshape(bsz, seq, MLA_HEADS * MLA_QK_PAD)
    tq = _tile(seq, 2048)
    out = pl.pallas_call(
        functools.partial(_mla_kernel, sub=min(tq, 1024)),
        grid=(bsz, MLA_HEADS, seq // tq),
        in_specs=[pl.BlockSpec((None, tq, MLA_QK_PAD), lambda b, h, i: (b, i, h)),
                  pl.BlockSpec((None, seq, MLA_QK_PAD), lambda b, h, i: (b, 0, h)),
                  pl.BlockSpec((None, MLA_V, seq), lambda b, h, i: (b, h, 0))],
        out_specs=pl.BlockSpec((None, tq, MLA_V), lambda b, h, i: (b, i, h)),
        out_shape=jax.ShapeDtypeStruct((bsz, seq, MLA_HEADS * MLA_V), BF16),
        compiler_params=_cparams("parallel", "parallel", "arbitrary"),
        name="mla_attn",
    )(q3, k3, v_t)
    return out.reshape(bsz * seq, MLA_HEADS * MLA_V)


def _dil_key_row0(ti, *, t_len, tb, kb, n_side):
    return jnp.clip(ti * tb - n_side, 0, t_len - kb)


def _dil_kernel(q_ref, k_ref, v_ref, o_ref, st_ref, *, t_len, tb, kb, tq, kw, n_side):
    ti = pl.program_id(2)
    k_row0 = _dil_key_row0(ti, t_len=t_len, tb=tb, kb=kb, n_side=n_side)
    lane = lax.broadcasted_iota(jnp.int32, (tq, LANE), 1)
    row = lax.broadcasted_iota(jnp.int32, (tq, kw), 0)
    col = lax.broadcasted_iota(jnp.int32, (tq, kw), 1)

    def tile(qi, carry):
        t0 = pl.multiple_of(qi * tq, tq)
        pos0 = ti * tb + t0
        ws = jnp.clip(pos0 - n_side, 0, t_len - kw)
        ws_loc = pl.multiple_of(ws - k_row0, n_side) if t_len > kw else 0
        valid = jnp.abs((pos0 + row) - (ws + col)) <= n_side
        stats = jnp.zeros((tq, LANE), F32)
        for hh in range(DIL_HEADS):
            sl = slice(hh * HEAD_DIM, (hh + 1) * HEAD_DIM)
            q = q_ref[pl.ds(t0, tq), sl]
            k = k_ref[0, pl.ds(ws_loc, kw), sl]
            v = v_ref[0, pl.ds(ws_loc, kw), sl]
            s = lax.dot_general(q, k, (((1,), (1,)), ((), ())), preferred_element_type=F32)
            s = jnp.where(valid, s, -jnp.inf)
            m = jnp.max(s, axis=-1, keepdims=True)
            p = jnp.exp2(s - m)
            l = jnp.sum(p, axis=-1, keepdims=True)
            o = jnp.dot(p.astype(BF16), v, preferred_element_type=F32)
            o_ref[pl.ds(t0, tq), sl] = (o / l).astype(o_ref.dtype)
            stats = jnp.where(lane == hh, m, stats)
            stats = jnp.where(lane == DIL_HEADS + hh, l, stats)
        st_ref[pl.ds(t0, tq), :] = stats
        return carry

    n_tiles = tb // tq
    lax.fori_loop(0, n_tiles, tile, 0, unroll=min(4, n_tiles))


def _dil_rows_per_block(t_len):
    return t_len if t_len <= 2048 else 2048


def _dilated_group(q3, k3, v3, dil, window):
    bsz, t_len, _ = q3.shape
    n_side = window // (2 * dil)
    tq = min(2 * n_side, t_len)
    kw = min(4 * n_side, t_len)
    tb = _dil_rows_per_block(t_len)
    kb = min(tb + 2 * n_side, t_len)
    geom = dict(t_len=t_len, tb=tb, kb=kb, n_side=n_side)

    def kv_index(b, r, ti):
        row0 = _dil_key_row0(ti, **geom)
        return (b, pl.multiple_of(row0, n_side) if kb < t_len else 0, r * DIL_COLS)

    kv_spec = pl.BlockSpec((pl.Element(1), pl.Element(kb), pl.Element(DIL_COLS)), kv_index)
    row_spec = pl.BlockSpec((None, tb, DIL_COLS), lambda b, r, ti: (b, ti, r))

    o, st = pl.pallas_call(
        functools.partial(_dil_kernel, tq=tq, kw=kw, **geom),
        grid=(bsz, dil, t_len // tb),
        in_specs=[row_spec, kv_spec, kv_spec],
        out_specs=[row_spec, pl.BlockSpec((None, tb, LANE), lambda b, r, ti: (b, ti, r))],
        out_shape=[jax.ShapeDtypeStruct((bsz, t_len, dil * DIL_COLS), BF16),
                   jax.ShapeDtypeStruct((bsz, t_len, dil * LANE), F32)],
        compiler_params=_cparams("parallel", "parallel", "parallel"),
        name=f"dilated_d{dil}",
    )(q3, k3, v3)
    return o, st, DIL_HEADS


def _merge_kernel(*refs, hbs, dils):
    ng = len(hbs)
    o_refs, st_refs, y_ref = refs[:ng], refs[ng:2 * ng], refs[2 * ng]
    o_scr, st_scr = refs[2 * ng + 1:]
    tm = y_ref.shape[0]

    for g in range(ng):
        d = dils[g]
        if d == 1:
            continue
        rows = tm // d
        nblk = DIL_HEADS // hbs[g]
        for r in range(d):
            put = pl.ds(r, rows, stride=d)
            for hh in range(DIL_HEADS):
                lo = r * DIL_COLS + hh * HEAD_DIM
                o_scr[g, hh, put, :] = o_refs[g][:, lo:lo + HEAD_DIM].astype(F32)
            for blk in range(nblk):
                lo = (r * nblk + blk) * LANE
                st_scr[g, blk, put, :] = st_refs[g][:, lo:lo + LANE]

    def o_tile(g, h):
        if dils[g] == 1:
            return o_refs[g][:, h * HEAD_DIM:(h + 1) * HEAD_DIM].astype(F32)
        return o_scr[g, h]

    def st_tile(g):
        nblk = DIL_HEADS // hbs[g]
        if dils[g] == 1:
            tiles = [st_refs[g][:, blk * LANE:(blk + 1) * LANE] for blk in range(nblk)]
        else:
            tiles = [st_scr[g, blk] for blk in range(nblk)]
        return functools.reduce(lambda u, w: u + w, tiles)

    sts = [st_tile(g) for g in range(ng)]
    m_all = functools.reduce(jnp.maximum, sts)
    a = [jnp.exp2(sts[g] - m_all) * pltpu.roll(sts[g], LANE - DIL_HEADS, 1) for g in range(ng)]
    den = functools.reduce(lambda u, w: u + w, a)
    head_lane = lax.broadcasted_iota(jnp.int32, (tm, LANE), 1) < DIL_HEADS
    spread = (lax.broadcasted_iota(jnp.int32, (LANE, DIL_COLS), 0)
              == jnp.right_shift(lax.broadcasted_iota(jnp.int32, (LANE, DIL_COLS), 1),
                                 HEAD_DIM.bit_length() - 1)).astype(BF16)
    coef = []
    for g in range(ng):
        cg = jnp.where(head_lane, a[g] / den, 0.0)
        hi = cg.astype(BF16)
        lo = (cg - hi.astype(F32)).astype(BF16)
        coef.append(jnp.dot(hi, spread, preferred_element_type=F32)
                    + jnp.dot(lo, spread, preferred_element_type=F32))
    for h in range(DIL_HEADS):
        sl = slice(h * HEAD_DIM, (h + 1) * HEAD_DIM)
        y = None
        for g in range(ng):
            term = coef[g][:, sl] * o_tile(g, h)
            y = term if y is None else y + term
        y_ref[:, sl] = y.astype(y_ref.dtype)


def _merge_groups(os_, sts, hbs, dils, seq):
    bsz = os_[0].shape[0]
    m = bsz * seq
    tm = _tile(seq, 512)
    per_b = seq // tm

    def spec(arr, d):
        return pl.BlockSpec((None, tm // d, arr.shape[2]), lambda i: (i // per_b, i % per_b, 0))

    in_specs = [spec(o, d) for o, d in zip(os_, dils)] + [spec(s, d) for s, d in zip(sts, dils)]
    ng = len(os_)
    max_blk = max(DIL_HEADS // hb for hb in hbs)
    return pl.pallas_call(
        functools.partial(_merge_kernel, hbs=tuple(hbs), dils=tuple(dils)),
        grid=(m // tm,),
        in_specs=in_specs,
        out_specs=pl.BlockSpec((tm, DIL_COLS), lambda i: (i, 0)),
        out_shape=jax.ShapeDtypeStruct((m, DIL_COLS), BF16),
        scratch_shapes=[pltpu.VMEM((ng, DIL_HEADS, tm, HEAD_DIM), F32),
                        pltpu.VMEM((ng, max_blk, tm, LANE), F32)],
        compiler_params=_cparams("parallel"),
        name="dil_merge",
    )(*os_, *sts)


def _mix_kernel(ya_ref, yb_ref, wa_ref, wb_ref, ga_ref, gb_ref, o_ref):
    pa = jnp.dot(ya_ref[...], wa_ref[...], preferred_element_type=F32)
    pb = jnp.dot(yb_ref[...], wb_ref[...], preferred_element_type=F32)
    o_ref[...] = (ga_ref[...].astype(F32) * pa + gb_ref[...].astype(F32) * pb).astype(o_ref.dtype)


def _mix(ya, yb, w_pa, w_pb, gates):
    m = ya.shape[0]
    d = w_pa.shape[1]
    tm = _tile(m, 1024)
    tn = _tile(d, 1024)
    ga0 = 0
    gb0 = d // tn
    return pl.pallas_call(
        _mix_kernel,
        grid=(m // tm, d // tn),
        in_specs=[pl.BlockSpec((tm, ya.shape[1]), lambda i, j: (i, 0)),
                  pl.BlockSpec((tm, yb.shape[1]), lambda i, j: (i, 0)),
                  pl.BlockSpec((w_pa.shape[0], tn), lambda i, j: (0, j)),
                  pl.BlockSpec((w_pb.shape[0], tn), lambda i, j: (0, j)),
                  pl.BlockSpec((tm, tn), lambda i, j: (i, ga0 + j)),
                  pl.BlockSpec((tm, tn), lambda i, j: (i, gb0 + j))],
        out_specs=pl.BlockSpec((tm, tn), lambda i, j: (i, j)),
        out_shape=jax.ShapeDtypeStruct((m, d), BF16),
        compiler_params=_cparams("parallel", "arbitrary"),
        name="mix",
    )(ya, yb, w_pa, w_pb, gates, gates)


def _outproj_kernel(a_ref, w_ref, x_ref, g_ref, o_ref):
    acc = jnp.dot(a_ref[...], w_ref[...], preferred_element_type=F32)
    o_ref[...] = x_ref[...] + g_ref[...] * acc


def _outproj(a, w, x2d, gate, seq):
    m, k = a.shape
    d = w.shape[1]
    bsz = m // seq
    tm = _tile(seq, 1024)
    tn = _tile(d, 512)
    per_b = seq // tm
    return pl.pallas_call(
        _outproj_kernel,
        grid=(m // tm, d // tn),
        in_specs=[pl.BlockSpec((tm, k), lambda i, j: (i, 0)),
                  pl.BlockSpec((k, tn), lambda i, j: (0, j)),
                  pl.BlockSpec((tm, tn), lambda i, j: (i, j)),
                  pl.BlockSpec((None, 1, tn), lambda i, j: (i // per_b, 0, j))],
        out_specs=pl.BlockSpec((tm, tn), lambda i, j: (i, j)),
        out_shape=jax.ShapeDtypeStruct((m, d), F32),
        compiler_params=_cparams("parallel", "arbitrary"),
        name="out_proj",
    )(a, w, x2d, gate.reshape(bsz, 1, d))


def _ffn_up_kernel(h_ref, wg_ref, wu_ref, wd_ref, o_ref, od_ref, *, n_j, ext_d):
    h = h_ref[...]
    g = jnp.dot(h, wg_ref[...], preferred_element_type=F32)
    u = jnp.dot(h, wu_ref[...], preferred_element_type=F32)
    o_ref[...] = (g * jax.nn.sigmoid(g) * u).astype(o_ref.dtype)
    _side_cast(wd_ref, od_ref, pl.program_id(0) * n_j + pl.program_id(1), *ext_d)


def _ffn_up(h, wg, wu, w_down):
    m, d = h.shape
    f = wg.shape[1]
    tm = _tile(m, 1024)
    tn = _tile(f, 512)
    n_i, n_j = m // tm, f // tn
    side = _side_cast_specs(w_down, f, w_down.shape[1], n_i * n_j, lambda i, j: i * n_j + j)
    return pl.pallas_call(
        functools.partial(_ffn_up_kernel, n_j=n_j, ext_d=w_down.shape),
        grid=(n_i, n_j),
        in_specs=[pl.BlockSpec((tm, d), lambda i, j: (i, 0)),
                  pl.BlockSpec((d, tn), lambda i, j: (0, j)),
                  pl.BlockSpec((d, tn), lambda i, j: (0, j)),
                  side[0]],
        out_specs=[pl.BlockSpec((tm, tn), lambda i, j: (i, j)), side[1]],
        out_shape=[jax.ShapeDtypeStruct((m, f), BF16), side[2]],
        compiler_params=_cparams("arbitrary", "arbitrary"),
        name="ffn_up",
    )(h, wg, wu, w_down)


def _ffn_down_kernel(a_ref, w_ref, x_ref, g_ref, fg_ref, o_ref, *, nk, final_norm):
    k = pl.program_id(1)

    @pl.when(k == 0)
    def _():
        o_ref[...] = jnp.dot(a_ref[...], w_ref[...], preferred_element_type=F32)

    @pl.when(k > 0)
    def _():
        o_ref[...] = jnp.dot(a_ref[...], w_ref[...], preferred_element_type=F32) + o_ref[...]

    @pl.when(k == nk - 1)
    def _():
        x2 = x_ref[...] + g_ref[...] * o_ref[...]
        o_ref[...] = _rms(x2, fg_ref[...]) if final_norm else x2


def _ffn_down(a, w, x2d, gate, final_g, seq, final_norm):
    m, f = a.shape
    d = w.shape[1]
    bsz = m // seq
    tm = _tile(seq, 512)
    tk = _tile(f, 1024)
    nk = f // tk
    per_b = seq // tm
    return pl.pallas_call(
        functools.partial(_ffn_down_kernel, nk=nk, final_norm=final_norm),
        grid=(m // tm, nk),
        in_specs=[pl.BlockSpec((tm, tk), lambda i, k: (i, k)),
                  pl.BlockSpec((tk, d), lambda i, k: (k, 0)),
                  pl.BlockSpec((tm, d), lambda i, k: (i, 0), pipeline_mode=pl.Buffered(1)),
                  pl.BlockSpec((None, 1, d), lambda i, k: (i // per_b, 0, 0)),
                  pl.BlockSpec((1, d), lambda i, k: (0, 0))],
        out_specs=pl.BlockSpec((tm, d), lambda i, k: (i, 0)),
        out_shape=jax.ShapeDtypeStruct((m, d), F32),
        compiler_params=_cparams("parallel", "arbitrary"),
        name="ffn_down",
    )(a, w, x2d, gate.reshape(bsz, 1, d), final_g.reshape(1, d))


FF_MULT = 1024


def kernel(x, c, positions, w_ada, b_ada, norm1_g, w_in, q_norm_g, w_uq, kv_norm_g, w_ukv,
           w_proj_a, w_proj_b, w_out, norm2_g, w_gate, w_up, w_down, final_g):
    bsz, seq, d = x.shape
    depth = w_ada.shape[0]
    m = bsz * seq
    cos_t, sin_t = _rope_tables(positions)
    x2d = x.reshape(m, d)

    for l in range(depth):
        mod = _ada(c, w_ada[l], b_ada[l])
        sh1, sc1, g1, sh2, sc2, g2 = jnp.split(mod, N_MOD, axis=-1)

        wt = w_in[l].T
        h = _prenorm(x2d, norm1_g[l], sh1, sc1, seq)
        z_s, w_dg = _in_small(h, wt[:N_SMALL_PAD].astype(BF16), wt)
        f_pad = -(-w_gate.shape[2] // FF_MULT) * FF_MULT
        gates, wg, wu, w_o, w_pa = _gate_proj(
            h, w_dg, 3 * DIL_GROUPS, 2 * d,
            [(w_gate[l], f_pad), (w_up[l], f_pad), (w_out[l], d), (w_proj_a[l], d)])

        w_uq_pad = jnp.pad(w_uq[l].reshape(MLA_Q_RANK, MLA_HEADS, MLA_NOPE + MLA_ROPE),
                           ((0, 0), (0, 0), (0, MLA_QK_PAD - MLA_NOPE - MLA_ROPE)))
        w_uq_pad = w_uq_pad.reshape(MLA_Q_RANK, MLA_HEADS * MLA_QK_PAD).astype(BF16)
        q = _qup(z_s, q_norm_g[l], w_uq_pad, cos_t, sin_t)
        w_kv = w_ukv[l].reshape(MLA_KV_RANK, MLA_HEADS, MLA_NOPE + MLA_V)
        w_k = w_kv[:, :, :MLA_NOPE].reshape(MLA_KV_RANK, MLA_HEADS * MLA_NOPE).astype(BF16)
        w_vt = w_kv[:, :, MLA_NOPE:].reshape(MLA_KV_RANK, MLA_HEADS * MLA_V).T.astype(BF16)
        k, v_t = _kvup(z_s, kv_norm_g[l], w_k, w_vt, cos_t, sin_t, seq)
        y_a = _mla_attention(q, k, v_t, bsz, seq)

        os_, sts, hbs, dils = [], [], [], []
        for g, (window, dil) in enumerate(DIL_PATTERNS):
            q3 = _dil_proj(h, w_dg, g, dil, seq, cos_t, sin_t, DIL_QSCALE)
            k3 = _dil_proj(h, w_dg, DIL_GROUPS + g, dil, seq, cos_t, sin_t)
            v3 = _dil_proj(h, w_dg, 2 * DIL_GROUPS + g, dil, seq)
            o_g, st_g, hb = _dilated_group(q3, k3, v3, dil, window)
            os_.append(o_g)
            sts.append(st_g)
            hbs.append(hb)
            dils.append(dil)
        y_b = _merge_groups(os_, sts, hbs, dils, seq)

        mixed = _mix(y_a, y_b, w_pa, w_proj_b[l].astype(BF16), gates)
        x2d = _outproj(mixed, w_o, x2d, g1, seq)

        h2 = _prenorm(x2d, norm2_g[l], sh2, sc2, seq)
        hmid, wd = _ffn_up(h2, wg, wu, w_down[l])
        x2d = _ffn_down(hmid, wd, x2d, g2, final_g, seq, final_norm=(l == depth - 1))

    return x2d.reshape(bsz, seq, d)
```

```python
import functools
import math

import jax
import jax.numpy as jnp
from jax import lax
from jax.experimental import pallas as pl
from jax.experimental.pallas import tpu as pltpu

F32 = jnp.float32
BF16 = jnp.bfloat16

NORM_EPS = 1e-6
ROPE_THETA = 10000.0
HEAD_DIM = 128
MLA_HEADS = 16
MLA_Q_RANK = 1024
MLA_KV_RANK = 512
MLA_NOPE = 128
MLA_ROPE = 64
MLA_V = 128
DIL_HEADS = 8
DIL_PATTERNS = ((128, 1), (512, 4), (2048, 16))
DIL_GROUPS = len(DIL_PATTERNS)
N_MOD = 6
LOG2E = 1.4426950408889634

LANE = 128
MLA_QK_PAD = 256
DIL_COLS = DIL_HEADS * HEAD_DIM
VMEM_LIMIT = 56 * 1024 * 1024

MLA_QSCALE = (MLA_NOPE + MLA_ROPE) ** -0.5 * LOG2E
DIL_QSCALE = HEAD_DIM ** -0.5 * LOG2E


def _cparams(*sem):
    return pltpu.CompilerParams(dimension_semantics=sem, vmem_limit_bytes=VMEM_LIMIT)


def _tile(n, pref):
    return pref if n % pref == 0 else n


def _rms(x, g):
    return x * lax.rsqrt(jnp.mean(x * x, axis=-1, keepdims=True) + NORM_EPS) * g


def _side_cast_rows(rows_padded, n_steps):
    tr = rows_padded // n_steps
    assert tr * n_steps == rows_padded and tr % 16 == 0, (rows_padded, n_steps)
    return tr


def _side_cast_specs(w, rows_padded, cols_padded, n_steps, step_of):
    tr = _side_cast_rows(rows_padded, n_steps)
    last = (w.shape[0] - 1) // tr

    def in_map(*ids):
        return (jnp.minimum(step_of(*ids), last), 0)

    def out_map(*ids):
        return (step_of(*ids), 0)

    return (pl.BlockSpec((tr, cols_padded), in_map), pl.BlockSpec((tr, cols_padded), out_map),
            jax.ShapeDtypeStruct((rows_padded, cols_padded), BF16))


def _side_cast(w_ref, o_ref, step, rows, cols):
    tr, tc = o_ref.shape
    r = step * tr + lax.broadcasted_iota(jnp.int32, (tr, tc), 0)
    c = lax.broadcasted_iota(jnp.int32, (tr, tc), 1)
    inside = jnp.logical_and(r < rows, c < cols)
    o_ref[...] = jnp.where(inside, w_ref[...], 0.0).astype(o_ref.dtype)


def _ada_kernel(c_ref, w_ref, b_ref, o_ref):
    c = c_ref[...]
    ca = (c * jax.nn.sigmoid(c)).astype(BF16)
    o_ref[...] = jnp.dot(ca, w_ref[...].astype(BF16), preferred_element_type=F32) + b_ref[...]


def _ada(c, w, b):
    bsz, d = c.shape
    n = w.shape[1]
    rows = 8
    cp = jnp.zeros((rows, d), F32).at[:bsz].set(c)
    tn = _tile(n, 512)
    out = pl.pallas_call(
        _ada_kernel,
        grid=(n // tn,),
        in_specs=[pl.BlockSpec((rows, d), lambda j: (0, 0)),
                  pl.BlockSpec((d, tn), lambda j: (0, j)),
                  pl.BlockSpec((1, tn), lambda j: (0, j))],
        out_specs=pl.BlockSpec((rows, tn), lambda j: (0, j)),
        out_shape=jax.ShapeDtypeStruct((rows, n), F32),
        compiler_params=_cparams("parallel"),
        name="ada",
    )(cp, w, b.reshape(1, n))
    return out[:bsz]


def _rope_tab_kernel(pos_ref, inv_ref, sgn_ref, msk_ref, cos_ref, sin_ref):
    ang = pos_ref[...].astype(F32) * inv_ref[...]
    cos_ref[...] = jnp.cos(ang) * msk_ref[...]
    sin_ref[...] = jnp.sin(ang) * sgn_ref[...]


def _rope_tables(positions):
    m = positions.size
    inv_f = ROPE_THETA ** (-jnp.arange(0, HEAD_DIM, 2, dtype=F32) / HEAD_DIM)
    inv_r = ROPE_THETA ** (-jnp.arange(0, MLA_ROPE, 2, dtype=F32) / MLA_ROPE)
    zpad = jnp.zeros((LANE - MLA_ROPE,), F32)
    hf, hr = HEAD_DIM // 2, MLA_ROPE // 2
    inv = jnp.concatenate([inv_f, inv_f, inv_r, inv_r, zpad]).reshape(1, 2 * LANE)
    sgn = jnp.concatenate([-jnp.ones(hf), jnp.ones(hf), -jnp.ones(hr), jnp.ones(hr), zpad]).astype(F32).reshape(1, 2 * LANE)
    msk = jnp.concatenate([jnp.ones(HEAD_DIM + MLA_ROPE), zpad]).astype(F32).reshape(1, 2 * LANE)
    tm = _tile(m, 1024)
    row = pl.BlockSpec((1, 2 * LANE), lambda i: (0, 0))
    tab = pl.BlockSpec((tm, 2 * LANE), lambda i: (i, 0))
    return pl.pallas_call(
        _rope_tab_kernel,
        grid=(m // tm,),
        in_specs=[pl.BlockSpec((tm, 1), lambda i: (i, 0)), row, row, row],
        out_specs=[tab, tab],
        out_shape=[jax.ShapeDtypeStruct((m, 2 * LANE), F32)] * 2,
        compiler_params=_cparams("parallel"),
        name="rope_tables",
    )(positions.reshape(m, 1), inv, sgn, msk)


PRENORM_ROWS = 16


def _prenorm_kernel(x_ref, g_ref, sh_ref, sc_ref, o_ref):
    g = g_ref[...]
    mul = 1.0 + sc_ref[...]
    sh = sh_ref[...]

    def band(r, carry):
        rows = pl.ds(pl.multiple_of(r * PRENORM_ROWS, PRENORM_ROWS), PRENORM_ROWS)
        o_ref[rows, :] = (_rms(x_ref[rows, :], g) * mul + sh).astype(o_ref.dtype)
        return carry

    n_bands = x_ref.shape[0] // PRENORM_ROWS
    lax.fori_loop(0, n_bands, band, 0, unroll=min(8, n_bands))


def _prenorm(x2d, g, shift, scale, seq):
    m, d = x2d.shape
    bsz = m // seq
    tm = _tile(seq, 512)
    per_b = seq // tm
    vec = pl.BlockSpec((None, 1, d), lambda i: (i // per_b, 0, 0))
    return pl.pallas_call(
        _prenorm_kernel,
        grid=(m // tm,),
        in_specs=[pl.BlockSpec((tm, d), lambda i: (i, 0)),
                  pl.BlockSpec((1, d), lambda i: (0, 0)), vec, vec],
        out_specs=pl.BlockSpec((tm, d), lambda i: (i, 0)),
        out_shape=jax.ShapeDtypeStruct((m, d), BF16),
        compiler_params=_cparams("parallel"),
        name="prenorm",
    )(x2d, g.reshape(1, d), shift.reshape(bsz, 1, d), scale.reshape(bsz, 1, d))


def _dot_nt(a, wt):
    return lax.dot_general(a, wt, (((1,), (1,)), ((), ())), preferred_element_type=F32)


N_SMALL = MLA_Q_RANK + MLA_KV_RANK + MLA_ROPE
N_SMALL_PAD = -(-N_SMALL // LANE) * LANE


def _in_small_kernel(a_ref, wt_ref, side_ref, o_ref, oside_ref):
    o_ref[...] = _dot_nt(a_ref[...], wt_ref[...])
    oside_ref[...] = side_ref[...].astype(oside_ref.dtype)


def _in_small(h, wt_s, wt):
    m, d = h.shape
    n = wt_s.shape[0]
    tm = _tile(m, 256)
    n_steps = m // tm
    rows = wt.shape[0] - N_SMALL
    tr = _side_cast_rows(rows, n_steps)

    def side_map(i):
        return (pl.multiple_of(N_SMALL + tr * i, 8), 0)

    return pl.pallas_call(
        _in_small_kernel,
        grid=(n_steps,),
        in_specs=[pl.BlockSpec((tm, d), lambda i: (i, 0)),
                  pl.BlockSpec((n, d), lambda i: (0, 0)),
                  pl.BlockSpec((pl.Element(tr), pl.Element(d)), side_map)],
        out_specs=[pl.BlockSpec((tm, n), lambda i: (i, 0)),
                   pl.BlockSpec((tr, d), lambda i: (i, 0))],
        out_shape=[jax.ShapeDtypeStruct((m, n), F32),
                   jax.ShapeDtypeStruct((rows, d), BF16)],
        compiler_params=_cparams("arbitrary"),
        name="in_small",
    )(h, wt_s, wt)


def _rope128(x, c, s):
    return x * c + pltpu.roll(x, HEAD_DIM // 2, 1) * s


def _rope_mla(x, c, s):
    half = MLA_ROPE // 2
    return x * c + (pltpu.roll(x, LANE - half, 1) + pltpu.roll(x, half, 1)) * s


def _dil_proj_kernel(*refs, rope, scale, dil):
    if rope:
        h_ref, w_ref, cos_ref, sin_ref, o_ref = refs[:5]
    else:
        h_ref, w_ref, o_ref = refs[:3]
    acc = _dot_nt(h_ref[...], w_ref[...])
    tm = acc.shape[0]

    def finish(y, c, s):
        if rope:
            y = _rope128(y, c, s)
            if scale != 1.0:
                y = y * scale
        return y.astype(o_ref.dtype)

    if dil == 1:
        c, s = (cos_ref[...], sin_ref[...]) if rope else (None, None)
        for hh in range(DIL_HEADS):
            sl = slice(hh * HEAD_DIM, (hh + 1) * HEAD_DIM)
            o_ref[:, sl] = finish(acc[:, sl], c, s)
        return

    scr_ref = refs[-1]
    rows = tm // dil
    for hh in range(DIL_HEADS):
        scr_ref[hh] = acc[:, hh * HEAD_DIM:(hh + 1) * HEAD_DIM]
    for r in range(dil):
        take = pl.ds(r, rows, stride=dil)
        c, s = (cos_ref[take, :], sin_ref[take, :]) if rope else (None, None)
        for hh in range(DIL_HEADS):
            lo = r * DIL_COLS + hh * HEAD_DIM
            o_ref[:, lo:lo + HEAD_DIM] = finish(scr_ref[hh, take, :], c, s)


def _dil_proj(h, wt_dg, col_tile, dil, seq, cos_t=None, sin_t=None, scale=1.0):
    m, d = h.shape
    bsz = m // seq
    tm = _tile(seq, 1024)
    per_b = seq // tm
    rope = cos_t is not None
    in_specs = [pl.BlockSpec((tm, d), lambda i: (i, 0)),
                pl.BlockSpec((DIL_COLS, d), lambda i: (col_tile, 0))]
    args = [h, wt_dg]
    if rope:
        in_specs += [pl.BlockSpec((tm, LANE), lambda i: (i, 0))] * 2
        args += [cos_t, sin_t]
    scratch = [pltpu.VMEM((DIL_HEADS, tm, HEAD_DIM), F32)] if dil > 1 else []
    return pl.pallas_call(
        functools.partial(_dil_proj_kernel, rope=rope, scale=scale, dil=dil),
        grid=(m // tm,),
        in_specs=in_specs,
        out_specs=pl.BlockSpec((None, tm // dil, dil * DIL_COLS), lambda i: (i // per_b, i % per_b, 0)),
        out_shape=jax.ShapeDtypeStruct((bsz, seq // dil, dil * DIL_COLS), BF16),
        scratch_shapes=scratch,
        compiler_params=_cparams("parallel"),
        name=f"dil_proj_d{dil}",
    )(*args)


def _gate_proj_kernel(*refs, n_i, extents):
    ns = len(extents)
    h_ref, w_ref = refs[:2]
    side_in, o_ref, side_out = refs[2:2 + ns], refs[2 + ns], refs[3 + ns:]
    acc = _dot_nt(h_ref[...], w_ref[...])
    o_ref[...] = jax.nn.sigmoid(acc).astype(o_ref.dtype)
    step = pl.program_id(0) * n_i + pl.program_id(1)
    for w_in_ref, w_out_ref, ext in zip(side_in, side_out, extents):
        _side_cast(w_in_ref, w_out_ref, step, *ext)


def _gate_proj(h, wt_dg, col_tile0, n, side_weights):
    m, d = h.shape
    tm = _tile(m, 1024)
    tn = DIL_COLS
    n_j, n_i = n // tn, m // tm
    step_of = lambda j, i: j * n_i + i
    sides = [_side_cast_specs(w, w.shape[0], cp, n_j * n_i, step_of) for w, cp in side_weights]
    return pl.pallas_call(
        functools.partial(_gate_proj_kernel, n_i=n_i, extents=tuple(w.shape for w, _ in side_weights)),
        grid=(n_j, n_i),
        in_specs=[pl.BlockSpec((tm, d), lambda j, i: (i, 0)),
                  pl.BlockSpec((tn, d), lambda j, i: (col_tile0 + j, 0))] + [s[0] for s in sides],
        out_specs=[pl.BlockSpec((tm, tn), lambda j, i: (i, j))] + [s[1] for s in sides],
        out_shape=[jax.ShapeDtypeStruct((m, n), BF16)] + [s[2] for s in sides],
        compiler_params=_cparams("arbitrary", "arbitrary"),
        name="gate_proj",
    )(h, wt_dg, *[w for w, _ in side_weights])


def _qup_kernel(z_ref, g_ref, w_ref, cos_ref, sin_ref, o_ref, zn_ref):
    @pl.when(pl.program_id(1) == 0)
    def _():
        zn_ref[...] = _rms(z_ref[...], g_ref[...]).astype(BF16)

    acc = jnp.dot(zn_ref[...], w_ref[...], preferred_element_type=F32)
    c = cos_ref[...]
    s = sin_ref[...]
    for hh in range(acc.shape[1] // MLA_QK_PAD):
        lo = hh * MLA_QK_PAD
        o_ref[:, lo:lo + LANE] = (acc[:, lo:lo + LANE] * MLA_QSCALE).astype(o_ref.dtype)
        rp = _rope_mla(acc[:, lo + LANE:lo + 2 * LANE], c, s)
        o_ref[:, lo + LANE:lo + 2 * LANE] = (rp * MLA_QSCALE).astype(o_ref.dtype)


def _qup(z_s, g, w_uq_pad, cos_t, sin_t):
    m = z_s.shape[0]
    k = MLA_Q_RANK
    n = w_uq_pad.shape[1]
    tm = _tile(m, 1024)
    tn = _tile(n, 1024)
    return pl.pallas_call(
        _qup_kernel,
        grid=(m // tm, n // tn),
        in_specs=[pl.BlockSpec((tm, k), lambda i, j: (i, 0)),
                  pl.BlockSpec((1, k), lambda i, j: (0, 0)),
                  pl.BlockSpec((k, tn), lambda i, j: (0, j)),
                  pl.BlockSpec((tm, LANE), lambda i, j: (i, 1)),
                  pl.BlockSpec((tm, LANE), lambda i, j: (i, 1))],
        out_specs=pl.BlockSpec((tm, tn), lambda i, j: (i, j)),
        out_shape=jax.ShapeDtypeStruct((m, n), BF16),
        scratch_shapes=[pltpu.VMEM((tm, k), BF16)],
        compiler_params=_cparams("parallel", "arbitrary"),
        name="q_up",
    )(z_s, g.reshape(1, k), w_uq_pad, cos_t, sin_t)


def _kvup_kernel(z_ref, kr_ref, g_ref, wk_ref, wvt_ref, cos_ref, sin_ref, k_ref, vt_ref, zn_ref):
    @pl.when(pl.program_id(1) == 0)
    def _():
        zn_ref[...] = _rms(z_ref[...], g_ref[...]).astype(BF16)

    zn = zn_ref[...]
    k_nope = jnp.dot(zn, wk_ref[...], preferred_element_type=F32)
    vt_ref[...] = _dot_nt(wvt_ref[...], zn).astype(vt_ref.dtype)
    kr = kr_ref[...]
    kr = jnp.where(lax.broadcasted_iota(jnp.int32, kr.shape, 1) < MLA_ROPE, kr, 0.0)
    kr = _rope_mla(kr, cos_ref[...], sin_ref[...]).astype(k_ref.dtype)
    for hh in range(k_nope.shape[1] // MLA_NOPE):
        ko = hh * MLA_QK_PAD
        k_ref[:, ko:ko + LANE] = k_nope[:, hh * MLA_NOPE:(hh + 1) * MLA_NOPE].astype(k_ref.dtype)
        k_ref[:, ko + LANE:ko + 2 * LANE] = kr


def _kvup(z_s, g, w_k, w_vt, cos_t, sin_t, seq):
    m = z_s.shape[0]
    bsz = m // seq
    k = MLA_KV_RANK
    tm = _tile(seq, 1024)
    per_b = seq // tm
    nh = 4
    kv_blk = MLA_Q_RANK // MLA_KV_RANK
    kr_blk = (MLA_Q_RANK + MLA_KV_RANK) // LANE
    return pl.pallas_call(
        _kvup_kernel,
        grid=(m // tm, MLA_HEADS // nh),
        in_specs=[pl.BlockSpec((tm, k), lambda i, j: (i, kv_blk)),
                  pl.BlockSpec((tm, LANE), lambda i, j: (i, kr_blk)),
                  pl.BlockSpec((1, k), lambda i, j: (0, 0)),
                  pl.BlockSpec((k, nh * MLA_NOPE), lambda i, j: (0, j)),
                  pl.BlockSpec((nh * MLA_V, k), lambda i, j: (j, 0)),
                  pl.BlockSpec((tm, LANE), lambda i, j: (i, 1)),
                  pl.BlockSpec((tm, LANE), lambda i, j: (i, 1))],
        out_specs=[pl.BlockSpec((tm, nh * MLA_QK_PAD), lambda i, j: (i, j)),
                   pl.BlockSpec((None, nh * MLA_V, tm), lambda i, j: (i // per_b, j, i % per_b))],
        out_shape=[jax.ShapeDtypeStruct((m, MLA_HEADS * MLA_QK_PAD), BF16),
                   jax.ShapeDtypeStruct((bsz, MLA_HEADS * MLA_V, seq), BF16)],
        scratch_shapes=[pltpu.VMEM((tm, k), BF16)],
        compiler_params=_cparams("parallel", "arbitrary"),
        name="kv_up",
    )(z_s, z_s, g.reshape(1, k), w_k, w_vt, cos_t, sin_t)


def _mla_kernel(q_ref, k_ref, vt_ref, o_ref, *, sub):
    n = q_ref.shape[0] // sub
    s_ts = [_dot_nt(k_ref[...], q_ref[i * sub:(i + 1) * sub, :]) for i in range(n)]
    ms = [jnp.max(s_t, axis=0, keepdims=True) for s_t in s_ts]
    for i in range(n):
        p = jnp.exp2(s_ts[i] - ms[i])
        l = jnp.sum(p, axis=0, keepdims=True)
        o_t = jnp.dot(vt_ref[...], p.astype(BF16), preferred_element_type=F32)
        o_ref[i * sub:(i + 1) * sub, :] = (o_t / l).T.astype(o_ref.dtype)


def _mla_attention(q, k, v_t, bsz, seq):
    q3 = q.reshape(bsz, seq, MLA_HEADS * MLA_QK_PAD)
    k3 = k.reshape(bsz, seq, MLA_HEADS * MLA_QK_PAD)
    tq = _tile(seq, 2048)
    out = pl.pallas_call(
        functools.partial(_mla_kernel, sub=min(tq, 1024)),
        grid=(bsz, MLA_HEADS, seq // tq),
        in_specs=[pl.BlockSpec((None, tq, MLA_QK_PAD), lambda b, h, i: (b, i, h)),
                  pl.BlockSpec((None, seq, MLA_QK_PAD), lambda b, h, i: (b, 0, h)),
                  pl.BlockSpec((None, MLA_V, seq), lambda b, h, i: (b, h, 0))],
        out_specs=pl.BlockSpec((None, tq, MLA_V), lambda b, h, i: (b, i, h)),
        out_shape=jax.ShapeDtypeStruct((bsz, seq, MLA_HEADS * MLA_V), BF16),
        compiler_params=_cparams("parallel", "parallel", "arbitrary"),
        name="mla_attn",
    )(q3, k3, v_t)
    return out.reshape(bsz * seq, MLA_HEADS * MLA_V)


def _dil_key_row0(ti, *, t_len, tb, kb, n_side):
    return jnp.clip(ti * tb - n_side, 0, t_len - kb)


def _dil_kernel(q_ref, k_ref, v_ref, o_ref, st_ref, *, t_len, tb, kb, tq, kw, n_side):
    ti = pl.program_id(2)
    k_row0 = _dil_key_row0(ti, t_len=t_len, tb=tb, kb=kb, n_side=n_side)
    lane = lax.broadcasted_iota(jnp.int32, (tq, LANE), 1)
    row = lax.broadcasted_iota(jnp.int32, (tq, kw), 0)
    col = lax.broadcasted_iota(jnp.int32, (tq, kw), 1)

    def tile(qi, carry):
        t0 = pl.multiple_of(qi * tq, tq)
        pos0 = ti * tb + t0
        ws = jnp.clip(pos0 - n_side, 0, t_len - kw)
        ws_loc = pl.multiple_of(ws - k_row0, n_side) if t_len > kw else 0
        valid = jnp.abs((pos0 + row) - (ws + col)) <= n_side
        stats = jnp.zeros((tq, LANE), F32)
        for hh in range(DIL_HEADS):
            sl = slice(hh * HEAD_DIM, (hh + 1) * HEAD_DIM)
            q = q_ref[pl.ds(t0, tq), sl]
            k = k_ref[0, pl.ds(ws_loc, kw), sl]
            v = v_ref[0, pl.ds(ws_loc, kw), sl]
            s = lax.dot_general(q, k, (((1,), (1,)), ((), ())), preferred_element_type=F32)
            s = jnp.where(valid, s, -jnp.inf)
            m = jnp.max(s, axis=-1, keepdims=True)
            p = jnp.exp2(s - m)
            l = jnp.sum(p, axis=-1, keepdims=True)
            o = jnp.dot(p.astype(BF16), v, preferred_element_type=F32)
            o_ref[pl.ds(t0, tq), sl] = (o / l).astype(o_ref.dtype)
            stats = jnp.where(lane == hh, m, stats)
            stats = jnp.where(lane == DIL_HEADS + hh, l, stats)
        st_ref[pl.ds(t0, tq), :] = stats
        return carry

    n_tiles = tb // tq
    lax.fori_loop(0, n_tiles, tile, 0, unroll=min(4, n_tiles))


def _dil_rows_per_block(t_len):
    return t_len if t_len <= 2048 else 2048


def _dilated_group(q3, k3, v3, dil, window):
    bsz, t_len, _ = q3.shape
    n_side = window // (2 * dil)
    tq = min(2 * n_side, t_len)
    kw = min(4 * n_side, t_len)
    tb = _dil_rows_per_block(t_len)
    kb = min(tb + 2 * n_side, t_len)
    geom = dict(t_len=t_len, tb=tb, kb=kb, n_side=n_side)

    def kv_index(b, r, ti):
        row0 = _dil_key_row0(ti, **geom)
        return (b, pl.multiple_of(row0, n_side) if kb < t_len else 0, r * DIL_COLS)

    kv_spec = pl.BlockSpec((pl.Element(1), pl.Element(kb), pl.Element(DIL_COLS)), kv_index)
    row_spec = pl.BlockSpec((None, tb, DIL_COLS), lambda b, r, ti: (b, ti, r))

    o, st = pl.pallas_call(
        functools.partial(_dil_kernel, tq=tq, kw=kw, **geom),
        grid=(bsz, dil, t_len // tb),
        in_specs=[row_spec, kv_spec, kv_spec],
        out_specs=[row_spec, pl.BlockSpec((None, tb, LANE), lambda b, r, ti: (b, ti, r))],
        out_shape=[jax.ShapeDtypeStruct((bsz, t_len, dil * DIL_COLS), BF16),
                   jax.ShapeDtypeStruct((bsz, t_len, dil * LANE), F32)],
        compiler_params=_cparams("parallel", "parallel", "parallel"),
        name=f"dilated_d{dil}",
    )(q3, k3, v3)
    return o, st, DIL_HEADS


def _merge_kernel(*refs, hbs, dils):
    ng = len(hbs)
    o_refs, st_refs, y_ref = refs[:ng], refs[ng:2 * ng], refs[2 * ng]
    o_scr, st_scr = refs[2 * ng + 1:]
    tm = y_ref.shape[0]

    for g in range(ng):
        d = dils[g]
        if d == 1:
            continue
        rows = tm // d
        nblk = DIL_HEADS // hbs[g]
        for r in range(d):
            put = pl.ds(r, rows, stride=d)
            for hh in range(DIL_HEADS):
                lo = r * DIL_COLS + hh * HEAD_DIM
                o_scr[g, hh, put, :] = o_refs[g][:, lo:lo + HEAD_DIM].astype(F32)
            for blk in range(nblk):
                lo = (r * nblk + blk) * LANE
                st_scr[g, blk, put, :] = st_refs[g][:, lo:lo + LANE]

    def o_tile(g, h):
        if dils[g] == 1:
            return o_refs[g][:, h * HEAD_DIM:(h + 1) * HEAD_DIM].astype(F32)
        return o_scr[g, h]

    def st_tile(g):
        nblk = DIL_HEADS // hbs[g]
        if dils[g] == 1:
            tiles = [st_refs[g][:, blk * LANE:(blk + 1) * LANE] for blk in range(nblk)]
        else:
            tiles = [st_scr[g, blk] for blk in range(nblk)]
        return functools.reduce(lambda u, w: u + w, tiles)

    sts = [st_tile(g) for g in range(ng)]
    m_all = functools.reduce(jnp.maximum, sts)
    a = [jnp.exp2(sts[g] - m_all) * pltpu.roll(sts[g], LANE - DIL_HEADS, 1) for g in range(ng)]
    den = functools.reduce(lambda u, w: u + w, a)
    head_lane = lax.broadcasted_iota(jnp.int32, (tm, LANE), 1) < DIL_HEADS
    spread = (lax.broadcasted_iota(jnp.int32, (LANE, DIL_COLS), 0)
              == jnp.right_shift(lax.broadcasted_iota(jnp.int32, (LANE, DIL_COLS), 1),
                                 HEAD_DIM.bit_length() - 1)).astype(BF16)
    coef = []
    for g in range(ng):
        cg = jnp.where(head_lane, a[g] / den, 0.0)
        hi = cg.astype(BF16)
        lo = (cg - hi.astype(F32)).astype(BF16)
        coef.append(jnp.dot(hi, spread, preferred_element_type=F32)
                    + jnp.dot(lo, spread, preferred_element_type=F32))
    for h in range(DIL_HEADS):
        sl = slice(h * HEAD_DIM, (h + 1) * HEAD_DIM)
        y = None
        for g in range(ng):
            term = coef[g][:, sl] * o_tile(g, h)
            y = term if y is None else y + term
        y_ref[:, sl] = y.astype(y_ref.dtype)


def _merge_groups(os_, sts, hbs, dils, seq):
    bsz = os_[0].shape[0]
    m = bsz * seq
    tm = _tile(seq, 512)
    per_b = seq // tm

    def spec(arr, d):
        return pl.BlockSpec((None, tm // d, arr.shape[2]), lambda i: (i // per_b, i % per_b, 0))

    in_specs = [spec(o, d) for o, d in zip(os_, dils)] + [spec(s, d) for s, d in zip(sts, dils)]
    ng = len(os_)
    max_blk = max(DIL_HEADS // hb for hb in hbs)
    return pl.pallas_call(
        functools.partial(_merge_kernel, hbs=tuple(hbs), dils=tuple(dils)),
        grid=(m // tm,),
        in_specs=in_specs,
        out_specs=pl.BlockSpec((tm, DIL_COLS), lambda i: (i, 0)),
        out_shape=jax.ShapeDtypeStruct((m, DIL_COLS), BF16),
        scratch_shapes=[pltpu.VMEM((ng, DIL_HEADS, tm, HEAD_DIM), F32),
                        pltpu.VMEM((ng, max_blk, tm, LANE), F32)],
        compiler_params=_cparams("parallel"),
        name="dil_merge",
    )(*os_, *sts)


def _mix_kernel(ya_ref, yb_ref, wa_ref, wb_ref, ga_ref, gb_ref, o_ref):
    pa = jnp.dot(ya_ref[...], wa_ref[...], preferred_element_type=F32)
    pb = jnp.dot(yb_ref[...], wb_ref[...], preferred_element_type=F32)
    o_ref[...] = (ga_ref[...].astype(F32) * pa + gb_ref[...].astype(F32) * pb).astype(o_ref.dtype)


def _mix(ya, yb, w_pa, w_pb, gates):
    m = ya.shape[0]
    d = w_pa.shape[1]
    tm = _tile(m, 1024)
    tn = _tile(d, 1024)
    ga0 = 0
    gb0 = d // tn
    return pl.pallas_call(
        _mix_kernel,
        grid=(m // tm, d // tn),
        in_specs=[pl.BlockSpec((tm, ya.shape[1]), lambda i, j: (i, 0)),
                  pl.BlockSpec((tm, yb.shape[1]), lambda i, j: (i, 0)),
                  pl.BlockSpec((w_pa.shape[0], tn), lambda i, j: (0, j)),
                  pl.BlockSpec((w_pb.shape[0], tn), lambda i, j: (0, j)),
                  pl.BlockSpec((tm, tn), lambda i, j: (i, ga0 + j)),
                  pl.BlockSpec((tm, tn), lambda i, j: (i, gb0 + j))],
        out_specs=pl.BlockSpec((tm, tn), lambda i, j: (i, j)),
        out_shape=jax.ShapeDtypeStruct((m, d), BF16),
        compiler_params=_cparams("parallel", "arbitrary"),
        name="mix",
    )(ya, yb, w_pa, w_pb, gates, gates)


def _outproj_kernel(a_ref, w_ref, x_ref, g_ref, o_ref):
    acc = jnp.dot(a_ref[...], w_ref[...], preferred_element_type=F32)
    o_ref[...] = x_ref[...] + g_ref[...] * acc


def _outproj(a, w, x2d, gate, seq):
    m, k = a.shape
    d = w.shape[1]
    bsz = m // seq
    tm = _tile(seq, 1024)
    tn = _tile(d, 512)
    per_b = seq // tm
    return pl.pallas_call(
        _outproj_kernel,
        grid=(m // tm, d // tn),
        in_specs=[pl.BlockSpec((tm, k), lambda i, j: (i, 0)),
                  pl.BlockSpec((k, tn), lambda i, j: (0, j)),
                  pl.BlockSpec((tm, tn), lambda i, j: (i, j)),
                  pl.BlockSpec((None, 1, tn), lambda i, j: (i // per_b, 0, j))],
        out_specs=pl.BlockSpec((tm, tn), lambda i, j: (i, j)),
        out_shape=jax.ShapeDtypeStruct((m, d), F32),
        compiler_params=_cparams("parallel", "arbitrary"),
        name="out_proj",
    )(a, w, x2d, gate.reshape(bsz, 1, d))


def _ffn_up_kernel(h_ref, wg_ref, wu_ref, wd_ref, o_ref, od_ref, *, n_j, ext_d):
    h = h_ref[...]
    g = jnp.dot(h, wg_ref[...], preferred_element_type=F32)
    u = jnp.dot(h, wu_ref[...], preferred_element_type=F32)
    o_ref[...] = (g * jax.nn.sigmoid(g) * u).astype(o_ref.dtype)
    _side_cast(wd_ref, od_ref, pl.program_id(0) * n_j + pl.program_id(1), *ext_d)


def _ffn_up(h, wg, wu, w_down):
    m, d = h.shape
    f = wg.shape[1]
    tm = _tile(m, 1024)
    tn = _tile(f, 512)
    n_i, n_j = m // tm, f // tn
    side = _side_cast_specs(w_down, f, w_down.shape[1], n_i * n_j, lambda i, j: i * n_j + j)
    return pl.pallas_call(
        functools.partial(_ffn_up_kernel, n_j=n_j, ext_d=w_down.shape),
        grid=(n_i, n_j),
        in_specs=[pl.BlockSpec((tm, d), lambda i, j: (i, 0)),
                  pl.BlockSpec((d, tn), lambda i, j: (0, j)),
                  pl.BlockSpec((d, tn), lambda i, j: (0, j)),
                  side[0]],
        out_specs=[pl.BlockSpec((tm, tn), lambda i, j: (i, j)), side[1]],
        out_shape=[jax.ShapeDtypeStruct((m, f), BF16), side[2]],
        compiler_params=_cparams("arbitrary", "arbitrary"),
        name="ffn_up",
    )(h, wg, wu, w_down)


def _ffn_down_kernel(a_ref, w_ref, x_ref, g_ref, fg_ref, o_ref, *, nk, final_norm):
    k = pl.program_id(1)

    @pl.when(k == 0)
    def _():
        o_ref[...] = jnp.dot(a_ref[...], w_ref[...], preferred_element_type=F32)

    @pl.when(k > 0)
    def _():
        o_ref[...] = jnp.dot(a_ref[...], w_ref[...], preferred_element_type=F32) + o_ref[...]

    @pl.when(k == nk - 1)
    def _():
        x2 = x_ref[...] + g_ref[...] * o_ref[...]
        o_ref[...] = _rms(x2, fg_ref[...]) if final_norm else x2


def _ffn_down(a, w, x2d, gate, final_g, seq, final_norm):
    m, f = a.shape
    d = w.shape[1]
    bsz = m // seq
    tm = _tile(seq, 512)
    tk = _tile(f, 1024)
    nk = f // tk
    per_b = seq // tm
    return pl.pallas_call(
        functools.partial(_ffn_down_kernel, nk=nk, final_norm=final_norm),
        grid=(m // tm, nk),
        in_specs=[pl.BlockSpec((tm, tk), lambda i, k: (i, k)),
                  pl.BlockSpec((tk, d), lambda i, k: (k, 0)),
                  pl.BlockSpec((tm, d), lambda i, k: (i, 0), pipeline_mode=pl.Buffered(1)),
                  pl.BlockSpec((None, 1, d), lambda i, k: (i // per_b, 0, 0)),
                  pl.BlockSpec((1, d), lambda i, k: (0, 0))],
        out_specs=pl.BlockSpec((tm, d), lambda i, k: (i, 0)),
        out_shape=jax.ShapeDtypeStruct((m, d), F32),
        compiler_params=_cparams("parallel", "arbitrary"),
        name="ffn_down",
    )(a, w, x2d, gate.reshape(bsz, 1, d), final_g.reshape(1, d))


FF_MULT = 1024


def kernel(x, c, positions, w_ada, b_ada, norm1_g, w_in, q_norm_g, w_uq, kv_norm_g, w_ukv,
           w_proj_a, w_proj_b, w_out, norm2_g, w_gate, w_up, w_down, final_g):
    bsz, seq, d = x.shape
    depth = w_ada.shape[0]
    m = bsz * seq
    cos_t, sin_t = _rope_tables(positions)
    x2d = x.reshape(m, d)

    for l in range(depth):
        mod = _ada(c, w_ada[l], b_ada[l])
        sh1, sc1, g1, sh2, sc2, g2 = jnp.split(mod, N_MOD, axis=-1)

        wt = w_in[l].T
        h = _prenorm(x2d, norm1_g[l], sh1, sc1, seq)
        z_s, w_dg = _in_small(h, wt[:N_SMALL_PAD].astype(BF16), wt)
        f_pad = -(-w_gate.shape[2] // FF_MULT) * FF_MULT
        gates, wg, wu, w_o, w_pa = _gate_proj(
            h, w_dg, 3 * DIL_GROUPS, 2 * d,
            [(w_gate[l], f_pad), (w_up[l], f_pad), (w_out[l], d), (w_proj_a[l], d)])

        w_uq_pad = jnp.pad(w_uq[l].reshape(MLA_Q_RANK, MLA_HEADS, MLA_NOPE + MLA_ROPE),
                           ((0, 0), (0, 0), (0, MLA_QK_PAD - MLA_NOPE - MLA_ROPE)))
        w_uq_pad = w_uq_pad.reshape(MLA_Q_RANK, MLA_HEADS * MLA_QK_PAD).astype(BF16)
        q = _qup(z_s, q_norm_g[l], w_uq_pad, cos_t, sin_t)
        w_kv = w_ukv[l].reshape(MLA_KV_RANK, MLA_HEADS, MLA_NOPE + MLA_V)
        w_k = w_kv[:, :, :MLA_NOPE].reshape(MLA_KV_RANK, MLA_HEADS * MLA_NOPE).astype(BF16)
        w_vt = w_kv[:, :, MLA_NOPE:].reshape(MLA_KV_RANK, MLA_HEADS * MLA_V).T.astype(BF16)
        k, v_t = _kvup(z_s, kv_norm_g[l], w_k, w_vt, cos_t, sin_t, seq)
        y_a = _mla_attention(q, k, v_t, bsz, seq)

        os_, sts, hbs, dils = [], [], [], []
        for g, (window, dil) in enumerate(DIL_PATTERNS):
            q3 = _dil_proj(h, w_dg, g, dil, seq, cos_t, sin_t, DIL_QSCALE)
            k3 = _dil_proj(h, w_dg, DIL_GROUPS + g, dil, seq, cos_t, sin_t)
            v3 = _dil_proj(h, w_dg, 2 * DIL_GROUPS + g, dil, seq)
            o_g, st_g, hb = _dilated_group(q3, k3, v3, dil, window)
            os_.append(o_g)
            sts.append(st_g)
            hbs.append(hb)
            dils.append(dil)
        y_b = _merge_groups(os_, sts, hbs, dils, seq)

        mixed = _mix(y_a, y_b, w_pa, w_proj_b[l].astype(BF16), gates)
        x2d = _outproj(mixed, w_o, x2d, g1, seq)

        h2 = _prenorm(x2d, norm2_g[l], sh2, sc2, seq)
        hmid, wd = _ffn_up(h2, wg, wu, w_down[l])
        x2d = _ffn_down(hmid, wd, x2d, g2, final_g, seq, final_norm=(l == depth - 1))

    return x2d.reshape(bsz, seq, d)
```

```python
import functools

import jax
import jax.numpy as jnp
from jax import lax
from jax.experimental import pallas as pl
from jax.experimental.pallas import tpu as pltpu

F32 = jnp.float32
BF16 = jnp.bfloat16

NORM_EPS = 1e-6
ROPE_THETA = 10000.0
HEAD_DIM = 128
MLA_HEADS = 16
MLA_Q_RANK = 1024
MLA_KV_RANK = 512
MLA_NOPE = 128
MLA_ROPE = 64
MLA_V = 128
DIL_HEADS = 8
DIL_PATTERNS = ((128, 1), (512, 4), (2048, 16))
DIL_GROUPS = len(DIL_PATTERNS)
N_MOD = 6
LOG2E = 1.4426950408889634

LANE = 128
MLA_QK_PAD = 256
DIL_COLS = DIL_HEADS * HEAD_DIM
VMEM_LIMIT = 56 * 1024 * 1024

MLA_QSCALE = (MLA_NOPE + MLA_ROPE) ** -0.5 * LOG2E
DIL_QSCALE = HEAD_DIM ** -0.5 * LOG2E


def _cparams(*sem):
    return pltpu.CompilerParams(dimension_semantics=sem, vmem_limit_bytes=VMEM_LIMIT)


def _tile(n, pref):
    return pref if n % pref == 0 else n


def _rms(x, g):
    return x * lax.rsqrt(jnp.mean(x * x, axis=-1, keepdims=True) + NORM_EPS) * g


def _side_cast_rows(rows_padded, n_steps):
    tr = rows_padded // n_steps
    assert tr * n_steps == rows_padded and tr % 16 == 0, (rows_padded, n_steps)
    return tr


def _side_cast_specs(w, rows_padded, cols_padded, n_steps, step_of):
    tr = _side_cast_rows(rows_padded, n_steps)
    last = (w.shape[0] - 1) // tr

    def in_map(*ids):
        return (jnp.minimum(step_of(*ids), last), 0)

    def out_map(*ids):
        return (step_of(*ids), 0)

    return (pl.BlockSpec((tr, cols_padded), in_map), pl.BlockSpec((tr, cols_padded), out_map),
            jax.ShapeDtypeStruct((rows_padded, cols_padded), BF16))


def _side_cast(w_ref, o_ref, step, rows, cols):
    tr, tc = o_ref.shape
    r = step * tr + lax.broadcasted_iota(jnp.int32, (tr, tc), 0)
    c = lax.broadcasted_iota(jnp.int32, (tr, tc), 1)
    inside = jnp.logical_and(r < rows, c < cols)
    o_ref[...] = jnp.where(inside, w_ref[...], 0.0).astype(o_ref.dtype)


def _ada_kernel(c_ref, w_ref, b_ref, o_ref):
    c = c_ref[...]
    ca = (c * jax.nn.sigmoid(c)).astype(BF16)
    o_ref[...] = jnp.dot(ca, w_ref[...].astype(BF16), preferred_element_type=F32) + b_ref[...]


def _ada(c, w, b):
    bsz, d = c.shape
    n = w.shape[1]
    rows = 8
    cp = jnp.zeros((rows, d), F32).at[:bsz].set(c)
    tn = _tile(n, 512)
    out = pl.pallas_call(
        _ada_kernel,
        grid=(n // tn,),
        in_specs=[pl.BlockSpec((rows, d), lambda j: (0, 0)),
                  pl.BlockSpec((d, tn), lambda j: (0, j)),
                  pl.BlockSpec((1, tn), lambda j: (0, j))],
        out_specs=pl.BlockSpec((rows, tn), lambda j: (0, j)),
        out_shape=jax.ShapeDtypeStruct((rows, n), F32),
        compiler_params=_cparams("parallel"),
        name="ada",
    )(cp, w, b.reshape(1, n))
    return out[:bsz]


def _rope_tab_kernel(pos_ref, inv_ref, cos_ref, sin_ref):
    ang = pos_ref[...].astype(F32) * inv_ref[...]
    c = jnp.cos(ang)
    s = jnp.sin(ang)
    lane = lax.broadcasted_iota(jnp.int32, ang.shape, 1)
    hf, hr = HEAD_DIM // 2, MLA_ROPE // 2
    c_far, s_far = pltpu.roll(c, LANE - hf, 1), pltpu.roll(s, LANE - hf, 1)
    c_mid, s_mid = pltpu.roll(c, LANE - hr, 1), pltpu.roll(s, LANE - hr, 1)
    cos_ref[:, :LANE] = jnp.where(lane < hf, c, c_far)
    sin_ref[:, :LANE] = jnp.where(lane < hf, -s, s_far)
    cos_ref[:, LANE:] = jnp.where(lane < hr, c_far, jnp.where(lane < 2 * hr, c_mid, 0.0))
    sin_ref[:, LANE:] = jnp.where(lane < hr, -s_far, jnp.where(lane < 2 * hr, s_mid, 0.0))


def _rope_tables(positions):
    m = positions.size
    inv_f = ROPE_THETA ** (-jnp.arange(0, HEAD_DIM, 2, dtype=F32) / HEAD_DIM)
    inv_r = ROPE_THETA ** (-jnp.arange(0, MLA_ROPE, 2, dtype=F32) / MLA_ROPE)
    inv = jnp.concatenate([inv_f, inv_r, jnp.zeros((LANE - inv_f.size - inv_r.size,), F32)]).reshape(1, LANE)
    tm = _tile(m, 1024)
    tab = pl.BlockSpec((tm, 2 * LANE), lambda i: (i, 0))
    return pl.pallas_call(
        _rope_tab_kernel,
        grid=(m // tm,),
        in_specs=[pl.BlockSpec((tm, 1), lambda i: (i, 0)), pl.BlockSpec((1, LANE), lambda i: (0, 0))],
        out_specs=[tab, tab],
        out_shape=[jax.ShapeDtypeStruct((m, 2 * LANE), F32)] * 2,
        compiler_params=_cparams("parallel"),
        name="rope_tables",
    )(positions.reshape(m, 1), inv)


PRENORM_ROWS = 16


def _prenorm_kernel(x_ref, g_ref, sh_ref, sc_ref, o_ref):
    g = g_ref[...]
    mul = 1.0 + sc_ref[...]
    sh = sh_ref[...]

    def band(r, carry):
        rows = pl.ds(pl.multiple_of(r * PRENORM_ROWS, PRENORM_ROWS), PRENORM_ROWS)
        o_ref[rows, :] = (_rms(x_ref[rows, :], g) * mul + sh).astype(o_ref.dtype)
        return carry

    n_bands = x_ref.shape[0] // PRENORM_ROWS
    lax.fori_loop(0, n_bands, band, 0, unroll=min(8, n_bands))


def _prenorm(x2d, g, shift, scale, seq):
    m, d = x2d.shape
    bsz = m // seq
    tm = _tile(seq, 512)
    per_b = seq // tm
    vec = pl.BlockSpec((None, 1, d), lambda i: (i // per_b, 0, 0))
    return pl.pallas_call(
        _prenorm_kernel,
        grid=(m // tm,),
        in_specs=[pl.BlockSpec((tm, d), lambda i: (i, 0)),
                  pl.BlockSpec((1, d), lambda i: (0, 0)), vec, vec],
        out_specs=pl.BlockSpec((tm, d), lambda i: (i, 0)),
        out_shape=jax.ShapeDtypeStruct((m, d), BF16),
        compiler_params=_cparams("parallel"),
        name="prenorm",
    )(x2d, g.reshape(1, d), shift.reshape(bsz, 1, d), scale.reshape(bsz, 1, d))


def _dot_nt(a, wt):
    return lax.dot_general(a, wt, (((1,), (1,)), ((), ())), preferred_element_type=F32)


N_SMALL = MLA_Q_RANK + MLA_KV_RANK + MLA_ROPE
N_SMALL_PAD = -(-N_SMALL // LANE) * LANE


def _in_small_kernel(a_ref, wt_ref, side_ref, o_ref, oside_ref):
    o_ref[...] = _dot_nt(a_ref[...], wt_ref[...])
    oside_ref[...] = side_ref[...].astype(oside_ref.dtype)


def _in_small(h, wt_s, wt):
    m, d = h.shape
    n = wt_s.shape[0]
    tm = _tile(m, 256)
    n_steps = m // tm
    rows = wt.shape[0] - N_SMALL
    tr = _side_cast_rows(rows, n_steps)

    def side_map(i):
        return (pl.multiple_of(N_SMALL + tr * i, 8), 0)

    return pl.pallas_call(
        _in_small_kernel,
        grid=(n_steps,),
        in_specs=[pl.BlockSpec((tm, d), lambda i: (i, 0)),
                  pl.BlockSpec((n, d), lambda i: (0, 0)),
                  pl.BlockSpec((pl.Element(tr), pl.Element(d)), side_map)],
        out_specs=[pl.BlockSpec((tm, n), lambda i: (i, 0)),
                   pl.BlockSpec((tr, d), lambda i: (i, 0))],
        out_shape=[jax.ShapeDtypeStruct((m, n), F32),
                   jax.ShapeDtypeStruct((rows, d), BF16)],
        compiler_params=_cparams("arbitrary"),
        name="in_small",
    )(h, wt_s, wt)


def _rope128(x, c, s):
    return x * c + pltpu.roll(x, HEAD_DIM // 2, 1) * s


def _rope_mla(x, c, s):
    half = MLA_ROPE // 2
    return x * c + (pltpu.roll(x, LANE - half, 1) + pltpu.roll(x, half, 1)) * s


def _dil_proj_kernel(*refs, rope, scale, dil):
    if rope:
        h_ref, w_ref, cos_ref, sin_ref, o_ref = refs[:5]
    else:
        h_ref, w_ref, o_ref = refs[:3]
    acc = _dot_nt(h_ref[...], w_ref[...])
    tm = acc.shape[0]

    def finish(y, c, s):
        if rope:
            y = _rope128(y, c, s)
            if scale != 1.0:
                y = y * scale
        return y.astype(o_ref.dtype)

    if dil == 1:
        c, s = (cos_ref[...], sin_ref[...]) if rope else (None, None)
        for hh in range(DIL_HEADS):
            sl = slice(hh * HEAD_DIM, (hh + 1) * HEAD_DIM)
            o_ref[:, sl] = finish(acc[:, sl], c, s)
        return

    scr_ref = refs[-1]
    rows = tm // dil
    for hh in range(DIL_HEADS):
        scr_ref[hh] = acc[:, hh * HEAD_DIM:(hh + 1) * HEAD_DIM]
    for r in range(dil):
        take = pl.ds(r, rows, stride=dil)
        c, s = (cos_ref[take, :], sin_ref[take, :]) if rope else (None, None)
        for hh in range(DIL_HEADS):
            lo = r * DIL_COLS + hh * HEAD_DIM
            o_ref[:, lo:lo + HEAD_DIM] = finish(scr_ref[hh, take, :], c, s)


def _dil_proj(h, wt_dg, col_tile, dil, seq, cos_t=None, sin_t=None, scale=1.0):
    m, d = h.shape
    bsz = m // seq
    tm = _tile(seq, 1024)
    per_b = seq // tm
    rope = cos_t is not None
    in_specs = [pl.BlockSpec((tm, d), lambda i: (i, 0)),
                pl.BlockSpec((DIL_COLS, d), lambda i: (col_tile, 0))]
    args = [h, wt_dg]
    if rope:
        in_specs += [pl.BlockSpec((tm, LANE), lambda i: (i, 0))] * 2
        args += [cos_t, sin_t]
    scratch = [pltpu.VMEM((DIL_HEADS, tm, HEAD_DIM), F32)] if dil > 1 else []
    return pl.pallas_call(
        functools.partial(_dil_proj_kernel, rope=rope, scale=scale, dil=dil),
        grid=(m // tm,),
        in_specs=in_specs,
        out_specs=pl.BlockSpec((None, tm // dil, dil * DIL_COLS), lambda i: (i // per_b, i % per_b, 0)),
        out_shape=jax.ShapeDtypeStruct((bsz, seq // dil, dil * DIL_COLS), BF16),
        scratch_shapes=scratch,
        compiler_params=_cparams("parallel"),
        name=f"dil_proj_d{dil}",
    )(*args)


def _gate_proj_kernel(*refs, n_i, extents):
    ns = len(extents)
    h_ref, w_ref = refs[:2]
    side_in, o_ref, side_out = refs[2:2 + ns], refs[2 + ns], refs[3 + ns:]
    acc = _dot_nt(h_ref[...], w_ref[...])
    o_ref[...] = jax.nn.sigmoid(acc).astype(o_ref.dtype)
    step = pl.program_id(0) * n_i + pl.program_id(1)
    for w_in_ref, w_out_ref, ext in zip(side_in, side_out, extents):
        _side_cast(w_in_ref, w_out_ref, step, *ext)


def _gate_proj(h, wt_dg, col_tile0, n, side_weights):
    m, d = h.shape
    tm = _tile(m, 1024)
    tn = DIL_COLS
    n_j, n_i = n // tn, m // tm
    step_of = lambda j, i: j * n_i + i
    sides = [_side_cast_specs(w, w.shape[0], cp, n_j * n_i, step_of) for w, cp in side_weights]
    return pl.pallas_call(
        functools.partial(_gate_proj_kernel, n_i=n_i, extents=tuple(w.shape for w, _ in side_weights)),
        grid=(n_j, n_i),
        in_specs=[pl.BlockSpec((tm, d), lambda j, i: (i, 0)),
                  pl.BlockSpec((tn, d), lambda j, i: (col_tile0 + j, 0))] + [s[0] for s in sides],
        out_specs=[pl.BlockSpec((tm, tn), lambda j, i: (i, j))] + [s[1] for s in sides],
        out_shape=[jax.ShapeDtypeStruct((m, n), BF16)] + [s[2] for s in sides],
        compiler_params=_cparams("arbitrary", "arbitrary"),
        name="gate_proj",
    )(h, wt_dg, *[w for w, _ in side_weights])


def _qup_kernel(z_ref, g_ref, w_ref, cos_ref, sin_ref, o_ref, zn_ref):
    @pl.when(pl.program_id(1) == 0)
    def _():
        zn_ref[...] = _rms(z_ref[...], g_ref[...]).astype(BF16)

    acc = jnp.dot(zn_ref[...], w_ref[...], preferred_element_type=F32)
    c = cos_ref[...]
    s = sin_ref[...]
    for hh in range(acc.shape[1] // MLA_QK_PAD):
        lo = hh * MLA_QK_PAD
        o_ref[:, lo:lo + LANE] = (acc[:, lo:lo + LANE] * MLA_QSCALE).astype(o_ref.dtype)
        rp = _rope_mla(acc[:, lo + LANE:lo + 2 * LANE], c, s)
        o_ref[:, lo + LANE:lo + 2 * LANE] = (rp * MLA_QSCALE).astype(o_ref.dtype)


def _qup(z_s, g, w_uq_pad, cos_t, sin_t):
    m = z_s.shape[0]
    k = MLA_Q_RANK
    n = w_uq_pad.shape[1]
    tm = _tile(m, 1024)
    tn = _tile(n, 1024)
    return pl.pallas_call(
        _qup_kernel,
        grid=(m // tm, n // tn),
        in_specs=[pl.BlockSpec((tm, k), lambda i, j: (i, 0)),
                  pl.BlockSpec((1, k), lambda i, j: (0, 0)),
                  pl.BlockSpec((k, tn), lambda i, j: (0, j)),
                  pl.BlockSpec((tm, LANE), lambda i, j: (i, 1)),
                  pl.BlockSpec((tm, LANE), lambda i, j: (i, 1))],
        out_specs=pl.BlockSpec((tm, tn), lambda i, j: (i, j)),
        out_shape=jax.ShapeDtypeStruct((m, n), BF16),
        scratch_shapes=[pltpu.VMEM((tm, k), BF16)],
        compiler_params=_cparams("parallel", "arbitrary"),
        name="q_up",
    )(z_s, g.reshape(1, k), w_uq_pad, cos_t, sin_t)


def _kvup_kernel(z_ref, kr_ref, g_ref, wk_ref, wvt_ref, cos_ref, sin_ref, k_ref, vt_ref, zn_ref):
    @pl.when(pl.program_id(1) == 0)
    def _():
        zn_ref[...] = _rms(z_ref[...], g_ref[...]).astype(BF16)

    zn = zn_ref[...]
    k_nope = jnp.dot(zn, wk_ref[...], preferred_element_type=F32)
    vt_ref[...] = _dot_nt(wvt_ref[...], zn).astype(vt_ref.dtype)
    kr = kr_ref[...]
    kr = jnp.where(lax.broadcasted_iota(jnp.int32, kr.shape, 1) < MLA_ROPE, kr, 0.0)
    kr = _rope_mla(kr, cos_ref[...], sin_ref[...]).astype(k_ref.dtype)
    for hh in range(k_nope.shape[1] // MLA_NOPE):
        ko = hh * MLA_QK_PAD
        k_ref[:, ko:ko + LANE] = k_nope[:, hh * MLA_NOPE:(hh + 1) * MLA_NOPE].astype(k_ref.dtype)
        k_ref[:, ko + LANE:ko + 2 * LANE] = kr


def _kvup(z_s, g, w_k, w_vt, cos_t, sin_t, seq):
    m = z_s.shape[0]
    bsz = m // seq
    k = MLA_KV_RANK
    tm = _tile(seq, 1024)
    per_b = seq // tm
    nh = 4
    kv_blk = MLA_Q_RANK // MLA_KV_RANK
    kr_blk = (MLA_Q_RANK + MLA_KV_RANK) // LANE
    return pl.pallas_call(
        _kvup_kernel,
        grid=(m // tm, MLA_HEADS // nh),
        in_specs=[pl.BlockSpec((tm, k), lambda i, j: (i, kv_blk)),
                  pl.BlockSpec((tm, LANE), lambda i, j: (i, kr_blk)),
                  pl.BlockSpec((1, k), lambda i, j: (0, 0)),
                  pl.BlockSpec((k, nh * MLA_NOPE), lambda i, j: (0, j)),
                  pl.BlockSpec((nh * MLA_V, k), lambda i, j: (j, 0)),
                  pl.BlockSpec((tm, LANE), lambda i, j: (i, 1)),
                  pl.BlockSpec((tm, LANE), lambda i, j: (i, 1))],
        out_specs=[pl.BlockSpec((tm, nh * MLA_QK_PAD), lambda i, j: (i, j)),
                   pl.BlockSpec((None, nh * MLA_V, tm), lambda i, j: (i // per_b, j, i % per_b))],
        out_shape=[jax.ShapeDtypeStruct((m, MLA_HEADS * MLA_QK_PAD), BF16),
                   jax.ShapeDtypeStruct((bsz, MLA_HEADS * MLA_V, seq), BF16)],
        scratch_shapes=[pltpu.VMEM((tm, k), BF16)],
        compiler_params=_cparams("parallel", "arbitrary"),
        name="kv_up",
    )(z_s, z_s, g.reshape(1, k), w_k, w_vt, cos_t, sin_t)


def _mla_kernel(q_ref, k_ref, vt_ref, o_ref, *, sub):
    n = q_ref.shape[0] // sub
    s_ts = [_dot_nt(k_ref[...], q_ref[i * sub:(i + 1) * sub, :]) for i in range(n)]
    ms = [jnp.max(s_t, axis=0, keepdims=True) for s_t in s_ts]
    for i in range(n):
        p = jnp.exp2(s_ts[i] - ms[i])
        l = jnp.sum(p, axis=0, keepdims=True)
        o_t = jnp.dot(vt_ref[...], p.astype(BF16), preferred_element_type=F32)
        o_ref[i * sub:(i + 1) * sub, :] = (o_t / l).T.astype(o_ref.dtype)


def _mla_attention(q, k, v_t, bsz, seq):
    q3 = q.reshape(bsz, seq, MLA_HEADS * MLA_QK_PAD)
    k3 = k.reshape(bsz, seq, MLA_HEADS * MLA_QK_PAD)
    tq = _tile(seq, 2048)
    out = pl.pallas_call(
        functools.partial(_mla_kernel, sub=min(tq, 1024)),
        grid=(bsz, MLA_HEADS, seq // tq),
        in_specs=[pl.BlockSpec((None, tq, MLA_QK_PAD), lambda b, h, i: (b, i, h)),
                  pl.BlockSpec((None, seq, MLA_QK_PAD), lambda b, h, i: (b, 0, h)),
                  pl.BlockSpec((None, MLA_V, seq), lambda b, h, i: (b, h, 0))],
        out_specs=pl.BlockSpec((None, tq, MLA_V), lambda b, h, i: (b, i, h)),
        out_shape=jax.ShapeDtypeStruct((bsz, seq, MLA_HEADS * MLA_V), BF16),
        compiler_params=_cparams("parallel", "parallel", "arbitrary"),
        name="mla_attn",
    )(q3, k3, v_t)
    return out.reshape(bsz * seq, MLA_HEADS * MLA_V)


def _dil_key_row0(ti, *, t_len, tb, kb, n_side):
    return jnp.clip(ti * tb - n_side, 0, t_len - kb)


def _dil_kernel(q_ref, k_ref, v_ref, o_ref, st_ref, *, t_len, tb, kb, tq, kw, n_side):
    ti = pl.program_id(2)
    k_row0 = _dil_key_row0(ti, t_len=t_len, tb=tb, kb=kb, n_side=n_side)
    lane = lax.broadcasted_iota(jnp.int32, (tq, LANE), 1)
    row = lax.broadcasted_iota(jnp.int32, (tq, kw), 0)
    col = lax.broadcasted_iota(jnp.int32, (tq, kw), 1)

    def tile(qi, carry):
        t0 = pl.multiple_of(qi * tq, tq)
        pos0 = ti * tb + t0
        ws = jnp.clip(pos0 - n_side, 0, t_len - kw)
        ws_loc = pl.multiple_of(ws - k_row0, n_side) if t_len > kw else 0
        valid = jnp.abs((pos0 + row) - (ws + col)) <= n_side
        stats = jnp.zeros((tq, LANE), F32)
        for hh in range(DIL_HEADS):
            sl = slice(hh * HEAD_DIM, (hh + 1) * HEAD_DIM)
            q = q_ref[pl.ds(t0, tq), sl]
            k = k_ref[0, pl.ds(ws_loc, kw), sl]
            v = v_ref[0, pl.ds(ws_loc, kw), sl]
            s = lax.dot_general(q, k, (((1,), (1,)), ((), ())), preferred_element_type=F32)
            s = jnp.where(valid, s, -jnp.inf)
            m = jnp.max(s, axis=-1, keepdims=True)
            p = jnp.exp2(s - m)
            l = jnp.sum(p, axis=-1, keepdims=True)
            o = jnp.dot(p.astype(BF16), v, preferred_element_type=F32)
            o_ref[pl.ds(t0, tq), sl] = (o / l).astype(o_ref.dtype)
            stats = jnp.where(lane == hh, m, stats)
            stats = jnp.where(lane == DIL_HEADS + hh, l, stats)
        st_ref[pl.ds(t0, tq), :] = stats
        return carry

    n_tiles = tb // tq
    lax.fori_loop(0, n_tiles, tile, 0, unroll=min(4, n_tiles))


def _dil_rows_per_block(t_len):
    return t_len if t_len <= 2048 else 2048


def _dilated_group(q3, k3, v3, dil, window):
    bsz, t_len, _ = q3.shape
    n_side = window // (2 * dil)
    tq = min(2 * n_side, t_len)
    kw = min(4 * n_side, t_len)
    tb = _dil_rows_per_block(t_len)
    kb = min(tb + 2 * n_side, t_len)
    geom = dict(t_len=t_len, tb=tb, kb=kb, n_side=n_side)

    def kv_index(b, r, ti):
        row0 = _dil_key_row0(ti, **geom)
        return (b, pl.multiple_of(row0, n_side) if kb < t_len else 0, r * DIL_COLS)

    kv_spec = pl.BlockSpec((pl.Element(1), pl.Element(kb), pl.Element(DIL_COLS)), kv_index)
    row_spec = pl.BlockSpec((None, tb, DIL_COLS), lambda b, r, ti: (b, ti, r))

    o, st = pl.pallas_call(
        functools.partial(_dil_kernel, tq=tq, kw=kw, **geom),
        grid=(bsz, dil, t_len // tb),
        in_specs=[row_spec, kv_spec, kv_spec],
        out_specs=[row_spec, pl.BlockSpec((None, tb, LANE), lambda b, r, ti: (b, ti, r))],
        out_shape=[jax.ShapeDtypeStruct((bsz, t_len, dil * DIL_COLS), BF16),
                   jax.ShapeDtypeStruct((bsz, t_len, dil * LANE), F32)],
        compiler_params=_cparams("parallel", "parallel", "parallel"),
        name=f"dilated_d{dil}",
    )(q3, k3, v3)
    return o, st, DIL_HEADS


def _merge_kernel(*refs, hbs, dils):
    ng = len(hbs)
    o_refs, st_refs, y_ref = refs[:ng], refs[ng:2 * ng], refs[2 * ng]
    o_scr, st_scr = refs[2 * ng + 1:]
    tm = y_ref.shape[0]

    for g in range(ng):
        d = dils[g]
        if d == 1:
            continue
        rows = tm // d
        nblk = DIL_HEADS // hbs[g]
        for r in range(d):
            put = pl.ds(r, rows, stride=d)
            for hh in range(DIL_HEADS):
                lo = r * DIL_COLS + hh * HEAD_DIM
                o_scr[g, hh, put, :] = o_refs[g][:, lo:lo + HEAD_DIM].astype(F32)
            for blk in range(nblk):
                lo = (r * nblk + blk) * LANE
                st_scr[g, blk, put, :] = st_refs[g][:, lo:lo + LANE]

    def o_tile(g, h):
        if dils[g] == 1:
            return o_refs[g][:, h * HEAD_DIM:(h + 1) * HEAD_DIM].astype(F32)
        return o_scr[g, h]

    def st_tile(g):
        nblk = DIL_HEADS // hbs[g]
        if dils[g] == 1:
            tiles = [st_refs[g][:, blk * LANE:(blk + 1) * LANE] for blk in range(nblk)]
        else:
            tiles = [st_scr[g, blk] for blk in range(nblk)]
        return functools.reduce(lambda u, w: u + w, tiles)

    sts = [st_tile(g) for g in range(ng)]
    m_all = functools.reduce(jnp.maximum, sts)
    a = [jnp.exp2(sts[g] - m_all) * pltpu.roll(sts[g], LANE - DIL_HEADS, 1) for g in range(ng)]
    den = functools.reduce(lambda u, w: u + w, a)
    head_lane = lax.broadcasted_iota(jnp.int32, (tm, LANE), 1) < DIL_HEADS
    spread = (lax.broadcasted_iota(jnp.int32, (LANE, DIL_COLS), 0)
              == jnp.right_shift(lax.broadcasted_iota(jnp.int32, (LANE, DIL_COLS), 1),
                                 HEAD_DIM.bit_length() - 1)).astype(BF16)
    coef = []
    for g in range(ng):
        cg = jnp.where(head_lane, a[g] / den, 0.0)
        hi = cg.astype(BF16)
        lo = (cg - hi.astype(F32)).astype(BF16)
        coef.append(jnp.dot(hi, spread, preferred_element_type=F32)
                    + jnp.dot(lo, spread, preferred_element_type=F32))
    for h in range(DIL_HEADS):
        sl = slice(h * HEAD_DIM, (h + 1) * HEAD_DIM)
        y = None
        for g in range(ng):
            term = coef[g][:, sl] * o_tile(g, h)
            y = term if y is None else y + term
        y_ref[:, sl] = y.astype(y_ref.dtype)


def _merge_groups(os_, sts, hbs, dils, seq):
    bsz = os_[0].shape[0]
    m = bsz * seq
    tm = _tile(seq, 512)
    per_b = seq // tm

    def spec(arr, d):
        return pl.BlockSpec((None, tm // d, arr.shape[2]), lambda i: (i // per_b, i % per_b, 0))

    in_specs = [spec(o, d) for o, d in zip(os_, dils)] + [spec(s, d) for s, d in zip(sts, dils)]
    ng = len(os_)
    max_blk = max(DIL_HEADS // hb for hb in hbs)
    return pl.pallas_call(
        functools.partial(_merge_kernel, hbs=tuple(hbs), dils=tuple(dils)),
        grid=(m // tm,),
        in_specs=in_specs,
        out_specs=pl.BlockSpec((tm, DIL_COLS), lambda i: (i, 0)),
        out_shape=jax.ShapeDtypeStruct((m, DIL_COLS), BF16),
        scratch_shapes=[pltpu.VMEM((ng, DIL_HEADS, tm, HEAD_DIM), F32),
                        pltpu.VMEM((ng, max_blk, tm, LANE), F32)],
        compiler_params=_cparams("parallel"),
        name="dil_merge",
    )(*os_, *sts)


def _mix_kernel(ya_ref, yb_ref, wa_ref, wb_ref, ga_ref, gb_ref, o_ref):
    pa = jnp.dot(ya_ref[...], wa_ref[...], preferred_element_type=F32)
    pb = jnp.dot(yb_ref[...], wb_ref[...], preferred_element_type=F32)
    o_ref[...] = (ga_ref[...].astype(F32) * pa + gb_ref[...].astype(F32) * pb).astype(o_ref.dtype)


def _mix(ya, yb, w_pa, w_pb, gates):
    m = ya.shape[0]
    d = w_pa.shape[1]
    tm = _tile(m, 1024)
    tn = _tile(d, 1024)
    ga0 = 0
    gb0 = d // tn
    return pl.pallas_call(
        _mix_kernel,
        grid=(m // tm, d // tn),
        in_specs=[pl.BlockSpec((tm, ya.shape[1]), lambda i, j: (i, 0)),
                  pl.BlockSpec((tm, yb.shape[1]), lambda i, j: (i, 0)),
                  pl.BlockSpec((w_pa.shape[0], tn), lambda i, j: (0, j)),
                  pl.BlockSpec((w_pb.shape[0], tn), lambda i, j: (0, j)),
                  pl.BlockSpec((tm, tn), lambda i, j: (i, ga0 + j)),
                  pl.BlockSpec((tm, tn), lambda i, j: (i, gb0 + j))],
        out_specs=pl.BlockSpec((tm, tn), lambda i, j: (i, j)),
        out_shape=jax.ShapeDtypeStruct((m, d), BF16),
        compiler_params=_cparams("parallel", "arbitrary"),
        name="mix",
    )(ya, yb, w_pa, w_pb, gates, gates)


def _outproj_kernel(a_ref, w_ref, x_ref, g_ref, o_ref):
    acc = jnp.dot(a_ref[...], w_ref[...], preferred_element_type=F32)
    o_ref[...] = x_ref[...] + g_ref[...] * acc


def _outproj(a, w, x2d, gate, seq):
    m, k = a.shape
    d = w.shape[1]
    bsz = m // seq
    tm = _tile(seq, 1024)
    tn = _tile(d, 512)
    per_b = seq // tm
    return pl.pallas_call(
        _outproj_kernel,
        grid=(m // tm, d // tn),
        in_specs=[pl.BlockSpec((tm, k), lambda i, j: (i, 0)),
                  pl.BlockSpec((k, tn), lambda i, j: (0, j)),
                  pl.BlockSpec((tm, tn), lambda i, j: (i, j)),
                  pl.BlockSpec((None, 1, tn), lambda i, j: (i // per_b, 0, j))],
        out_specs=pl.BlockSpec((tm, tn), lambda i, j: (i, j)),
        out_shape=jax.ShapeDtypeStruct((m, d), F32),
        compiler_params=_cparams("parallel", "arbitrary"),
        name="out_proj",
    )(a, w, x2d, gate.reshape(bsz, 1, d))


def _ffn_up_kernel(h_ref, wg_ref, wu_ref, wd_ref, o_ref, od_ref, *, n_j, ext_d):
    h = h_ref[...]
    g = jnp.dot(h, wg_ref[...], preferred_element_type=F32)
    u = jnp.dot(h, wu_ref[...], preferred_element_type=F32)
    o_ref[...] = (g * jax.nn.sigmoid(g) * u).astype(o_ref.dtype)
    _side_cast(wd_ref, od_ref, pl.program_id(0) * n_j + pl.program_id(1), *ext_d)


def _ffn_up(h, wg, wu, w_down):
    m, d = h.shape
    f = wg.shape[1]
    tm = _tile(m, 1024)
    tn = _tile(f, 512)
    n_i, n_j = m // tm, f // tn
    side = _side_cast_specs(w_down, f, w_down.shape[1], n_i * n_j, lambda i, j: i * n_j + j)
    return pl.pallas_call(
        functools.partial(_ffn_up_kernel, n_j=n_j, ext_d=w_down.shape),
        grid=(n_i, n_j),
        in_specs=[pl.BlockSpec((tm, d), lambda i, j: (i, 0)),
                  pl.BlockSpec((d, tn), lambda i, j: (0, j)),
                  pl.BlockSpec((d, tn), lambda i, j: (0, j)),
                  side[0]],
        out_specs=[pl.BlockSpec((tm, tn), lambda i, j: (i, j)), side[1]],
        out_shape=[jax.ShapeDtypeStruct((m, f), BF16), side[2]],
        compiler_params=_cparams("arbitrary", "arbitrary"),
        name="ffn_up",
    )(h, wg, wu, w_down)


def _ffn_down_kernel(a_ref, w_ref, x_ref, g_ref, fg_ref, o_ref, *, nk, final_norm):
    k = pl.program_id(1)

    @pl.when(k == 0)
    def _():
        o_ref[...] = jnp.dot(a_ref[...], w_ref[...], preferred_element_type=F32)

    @pl.when(k > 0)
    def _():
        o_ref[...] = jnp.dot(a_ref[...], w_ref[...], preferred_element_type=F32) + o_ref[...]

    @pl.when(k == nk - 1)
    def _():
        x2 = x_ref[...] + g_ref[...] * o_ref[...]
        o_ref[...] = _rms(x2, fg_ref[...]) if final_norm else x2


def _ffn_down(a, w, x2d, gate, final_g, seq, final_norm):
    m, f = a.shape
    d = w.shape[1]
    bsz = m // seq
    tm = _tile(seq, 512)
    tk = _tile(f, 1024)
    nk = f // tk
    per_b = seq // tm
    return pl.pallas_call(
        functools.partial(_ffn_down_kernel, nk=nk, final_norm=final_norm),
        grid=(m // tm, nk),
        in_specs=[pl.BlockSpec((tm, tk), lambda i, k: (i, k)),
                  pl.BlockSpec((tk, d), lambda i, k: (k, 0)),
                  pl.BlockSpec((tm, d), lambda i, k: (i, 0), pipeline_mode=pl.Buffered(1)),
                  pl.BlockSpec((None, 1, d), lambda i, k: (i // per_b, 0, 0)),
                  pl.BlockSpec((1, d), lambda i, k: (0, 0))],
        out_specs=pl.BlockSpec((tm, d), lambda i, k: (i, 0)),
        out_shape=jax.ShapeDtypeStruct((m, d), F32),
        compiler_params=_cparams("parallel", "arbitrary"),
        name="ffn_down",
    )(a, w, x2d, gate.reshape(bsz, 1, d), final_g.reshape(1, d))


FF_MULT = 1024


def kernel(x, c, positions, w_ada, b_ada, norm1_g, w_in, q_norm_g, w_uq, kv_norm_g, w_ukv,
           w_proj_a, w_proj_b, w_out, norm2_g, w_gate, w_up, w_down, final_g):
    bsz, seq, d = x.shape
    depth = w_ada.shape[0]
    m = bsz * seq
    cos_t, sin_t = _rope_tables(positions)
    x2d = x.reshape(m, d)

    for l in range(depth):
        mod = _ada(c, w_ada[l], b_ada[l])
        sh1, sc1, g1, sh2, sc2, g2 = jnp.split(mod, N_MOD, axis=-1)

        wt = w_in[l].T
        h = _prenorm(x2d, norm1_g[l], sh1, sc1, seq)
        z_s, w_dg = _in_small(h, wt[:N_SMALL_PAD].astype(BF16), wt)
        f_pad = -(-w_gate.shape[2] // FF_MULT) * FF_MULT
        gates, wg, wu, w_o, w_pa = _gate_proj(
            h, w_dg, 3 * DIL_GROUPS, 2 * d,
            [(w_gate[l], f_pad), (w_up[l], f_pad), (w_out[l], d), (w_proj_a[l], d)])

        w_uq_pad = jnp.pad(w_uq[l].reshape(MLA_Q_RANK, MLA_HEADS, MLA_NOPE + MLA_ROPE),
                           ((0, 0), (0, 0), (0, MLA_QK_PAD - MLA_NOPE - MLA_ROPE)))
        w_uq_pad = w_uq_pad.reshape(MLA_Q_RANK, MLA_HEADS * MLA_QK_PAD).astype(BF16)
        q = _qup(z_s, q_norm_g[l], w_uq_pad, cos_t, sin_t)
        w_kv = w_ukv[l].reshape(MLA_KV_RANK, MLA_HEADS, MLA_NOPE + MLA_V)
        w_k = w_kv[:, :, :MLA_NOPE].reshape(MLA_KV_RANK, MLA_HEADS * MLA_NOPE).astype(BF16)
        w_vt = w_kv[:, :, MLA_NOPE:].reshape(MLA_KV_RANK, MLA_HEADS * MLA_V).T.astype(BF16)
        k, v_t = _kvup(z_s, kv_norm_g[l], w_k, w_vt, cos_t, sin_t, seq)
        y_a = _mla_attention(q, k, v_t, bsz, seq)

        os_, sts, hbs, dils = [], [], [], []
        for g, (window, dil) in enumerate(DIL_PATTERNS):
            q3 = _dil_proj(h, w_dg, g, dil, seq, cos_t, sin_t, DIL_QSCALE)
            k3 = _dil_proj(h, w_dg, DIL_GROUPS + g, dil, seq, cos_t, sin_t)
            v3 = _dil_proj(h, w_dg, 2 * DIL_GROUPS + g, dil, seq)
            o_g, st_g, hb = _dilated_group(q3, k3, v3, dil, window)
            os_.append(o_g)
            sts.append(st_g)
            hbs.append(hb)
            dils.append(dil)
        y_b = _merge_groups(os_, sts, hbs, dils, seq)

        mixed = _mix(y_a, y_b, w_pa, w_proj_b[l].astype(BF16), gates)
        x2d = _outproj(mixed, w_o, x2d, g1, seq)

        h2 = _prenorm(x2d, norm2_g[l], sh2, sc2, seq)
        hmid, wd = _ffn_up(h2, wg, wu, w_down[l])
        x2d = _ffn_down(hmid, wd, x2d, g2, final_g, seq, final_norm=(l == depth - 1))

    return x2d.reshape(bsz, seq, d)
```

```python
import functools

import jax
import jax.numpy as jnp
from jax import lax
from jax.experimental import pallas as pl
from jax.experimental.pallas import tpu as pltpu

F32 = jnp.float32
BF16 = jnp.bfloat16

NORM_EPS = 1e-6
ROPE_THETA = 10000.0
HEAD_DIM = 128
MLA_HEADS = 16
MLA_Q_RANK = 1024
MLA_KV_RANK = 512
MLA_NOPE = 128
MLA_ROPE = 64
MLA_V = 128
DIL_HEADS = 8
DIL_PATTERNS = ((128, 1), (512, 4), (2048, 16))
DIL_GROUPS = len(DIL_PATTERNS)
N_MOD = 6
LOG2E = 1.4426950408889634

LANE = 128
MLA_QK_PAD = 256
DIL_COLS = DIL_HEADS * HEAD_DIM
VMEM_LIMIT = 56 * 1024 * 1024

MLA_QSCALE = (MLA_NOPE + MLA_ROPE) ** -0.5 * LOG2E
DIL_QSCALE = HEAD_DIM ** -0.5 * LOG2E


def _cparams(*sem):
    return pltpu.CompilerParams(dimension_semantics=sem, vmem_limit_bytes=VMEM_LIMIT)


def _tile(n, pref):
    return pref if n % pref == 0 else n


def _rms(x, g):
    return x * lax.rsqrt(jnp.mean(x * x, axis=-1, keepdims=True) + NORM_EPS) * g


def _side_cast_rows(rows_padded, n_steps):
    tr = rows_padded // n_steps
    assert tr * n_steps == rows_padded and tr % 16 == 0, (rows_padded, n_steps)
    return tr


def _side_cast_specs(w, rows_padded, cols_padded, n_steps, step_of):
    tr = _side_cast_rows(rows_padded, n_steps)
    last = (w.shape[0] - 1) // tr

    def in_map(*ids):
        return (jnp.minimum(step_of(*ids), last), 0)

    def out_map(*ids):
        return (step_of(*ids), 0)

    return (pl.BlockSpec((tr, cols_padded), in_map), pl.BlockSpec((tr, cols_padded), out_map),
            jax.ShapeDtypeStruct((rows_padded, cols_padded), BF16))


def _side_cast(w_ref, o_ref, step, rows, cols):
    tr, tc = o_ref.shape
    r = step * tr + lax.broadcasted_iota(jnp.int32, (tr, tc), 0)
    c = lax.broadcasted_iota(jnp.int32, (tr, tc), 1)
    inside = jnp.logical_and(r < rows, c < cols)
    o_ref[...] = jnp.where(inside, w_ref[...], 0.0).astype(o_ref.dtype)


def _ada_kernel(c_ref, w_ref, b_ref, o_ref):
    c = c_ref[...]
    ca = (c * jax.nn.sigmoid(c)).astype(BF16)
    o_ref[...] = jnp.dot(ca, w_ref[...].astype(BF16), preferred_element_type=F32) + b_ref[...]


def _ada(c, w, b):
    bsz, d = c.shape
    n = w.shape[1]
    rows = 8
    cp = jnp.zeros((rows, d), F32).at[:bsz].set(c)
    tn = _tile(n, 512)
    out = pl.pallas_call(
        _ada_kernel,
        grid=(n // tn,),
        in_specs=[pl.BlockSpec((rows, d), lambda j: (0, 0)),
                  pl.BlockSpec((d, tn), lambda j: (0, j)),
                  pl.BlockSpec((1, tn), lambda j: (0, j))],
        out_specs=pl.BlockSpec((rows, tn), lambda j: (0, j)),
        out_shape=jax.ShapeDtypeStruct((rows, n), F32),
        compiler_params=_cparams("parallel"),
        name="ada",
    )(cp, w, b.reshape(1, n))
    return out[:bsz]


def _rope_tab_kernel(pos_ref, inv_ref, cos_ref, sin_ref):
    ang = pos_ref[...].astype(F32) * inv_ref[...]
    c = jnp.cos(ang)
    s = jnp.sin(ang)
    lane = lax.broadcasted_iota(jnp.int32, ang.shape, 1)
    hf, hr = HEAD_DIM // 2, MLA_ROPE // 2
    c_far, s_far = pltpu.roll(c, LANE - hf, 1), pltpu.roll(s, LANE - hf, 1)
    c_mid, s_mid = pltpu.roll(c, LANE - hr, 1), pltpu.roll(s, LANE - hr, 1)
    cos_ref[:, :LANE] = jnp.where(lane < hf, c, c_far)
    sin_ref[:, :LANE] = jnp.where(lane < hf, -s, s_far)
    cos_ref[:, LANE:] = jnp.where(lane < hr, c_far, jnp.where(lane < 2 * hr, c_mid, 0.0))
    sin_ref[:, LANE:] = jnp.where(lane < hr, -s_far, jnp.where(lane < 2 * hr, s_mid, 0.0))


def _rope_tables(positions):
    m = positions.size
    inv_f = ROPE_THETA ** (-jnp.arange(0, HEAD_DIM, 2, dtype=F32) / HEAD_DIM)
    inv_r = ROPE_THETA ** (-jnp.arange(0, MLA_ROPE, 2, dtype=F32) / MLA_ROPE)
    inv = jnp.concatenate([inv_f, inv_r, jnp.zeros((LANE - inv_f.size - inv_r.size,), F32)]).reshape(1, LANE)
    tm = _tile(m, 1024)
    tab = pl.BlockSpec((tm, 2 * LANE), lambda i: (i, 0))
    return pl.pallas_call(
        _rope_tab_kernel,
        grid=(m // tm,),
        in_specs=[pl.BlockSpec((tm, 1), lambda i: (i, 0)), pl.BlockSpec((1, LANE), lambda i: (0, 0))],
        out_specs=[tab, tab],
        out_shape=[jax.ShapeDtypeStruct((m, 2 * LANE), F32)] * 2,
        compiler_params=_cparams("parallel"),
        name="rope_tables",
    )(positions.reshape(m, 1), inv)


PRENORM_ROWS = 16


def _prenorm_kernel(x_ref, g_ref, sh_ref, sc_ref, o_ref):
    g = g_ref[...]
    mul = 1.0 + sc_ref[...]
    sh = sh_ref[...]

    def band(r, carry):
        rows = pl.ds(pl.multiple_of(r * PRENORM_ROWS, PRENORM_ROWS), PRENORM_ROWS)
        o_ref[rows, :] = (_rms(x_ref[rows, :], g) * mul + sh).astype(o_ref.dtype)
        return carry

    n_bands = x_ref.shape[0] // PRENORM_ROWS
    lax.fori_loop(0, n_bands, band, 0, unroll=min(8, n_bands))


def _prenorm(x2d, g, shift, scale, seq):
    m, d = x2d.shape
    bsz = m // seq
    tm = _tile(seq, 512)
    per_b = seq // tm
    vec = pl.BlockSpec((None, 1, d), lambda i: (i // per_b, 0, 0))
    return pl.pallas_call(
        _prenorm_kernel,
        grid=(m // tm,),
        in_specs=[pl.BlockSpec((tm, d), lambda i: (i, 0)),
                  pl.BlockSpec((1, d), lambda i: (0, 0)), vec, vec],
        out_specs=pl.BlockSpec((tm, d), lambda i: (i, 0)),
        out_shape=jax.ShapeDtypeStruct((m, d), BF16),
        compiler_params=_cparams("parallel"),
        name="prenorm",
    )(x2d, g.reshape(1, d), shift.reshape(bsz, 1, d), scale.reshape(bsz, 1, d))


def _dot_nt(a, wt):
    return lax.dot_general(a, wt, (((1,), (1,)), ((), ())), preferred_element_type=F32)


N_SMALL = MLA_Q_RANK + MLA_KV_RANK + MLA_ROPE
N_SMALL_PAD = -(-N_SMALL // LANE) * LANE


def _in_small_kernel(a_ref, wt_ref, side_ref, o_ref, oside_ref):
    o_ref[...] = _dot_nt(a_ref[...], wt_ref[...])
    oside_ref[...] = side_ref[...].astype(oside_ref.dtype)


def _in_small(h, wt_s, wt):
    m, d = h.shape
    n = wt_s.shape[0]
    tm = _tile(m, 256)
    n_steps = m // tm
    rows = wt.shape[0] - N_SMALL
    tr = _side_cast_rows(rows, n_steps)

    def side_map(i):
        return (pl.multiple_of(N_SMALL + tr * i, 8), 0)

    return pl.pallas_call(
        _in_small_kernel,
        grid=(n_steps,),
        in_specs=[pl.BlockSpec((tm, d), lambda i: (i, 0)),
                  pl.BlockSpec((n, d), lambda i: (0, 0)),
                  pl.BlockSpec((pl.Element(tr), pl.Element(d)), side_map)],
        out_specs=[pl.BlockSpec((tm, n), lambda i: (i, 0)),
                   pl.BlockSpec((tr, d), lambda i: (i, 0))],
        out_shape=[jax.ShapeDtypeStruct((m, n), F32),
                   jax.ShapeDtypeStruct((rows, d), BF16)],
        compiler_params=_cparams("arbitrary"),
        name="in_small",
    )(h, wt_s, wt)


def _rope128(x, c, s):
    return x * c + pltpu.roll(x, HEAD_DIM // 2, 1) * s


def _rope_mla(x, c, s):
    half = MLA_ROPE // 2
    return x * c + (pltpu.roll(x, LANE - half, 1) + pltpu.roll(x, half, 1)) * s


def _dil_proj_kernel(*refs, rope, scale, dil):
    if rope:
        h_ref, w_ref, cos_ref, sin_ref, o_ref = refs[:5]
    else:
        h_ref, w_ref, o_ref = refs[:3]
    acc = _dot_nt(h_ref[...], w_ref[...])
    tm = acc.shape[0]

    def finish(y, c, s):
        if rope:
            y = _rope128(y, c, s)
            if scale != 1.0:
                y = y * scale
        return y.astype(o_ref.dtype)

    if dil == 1:
        c, s = (cos_ref[...], sin_ref[...]) if rope else (None, None)
        for hh in range(DIL_HEADS):
            sl = slice(hh * HEAD_DIM, (hh + 1) * HEAD_DIM)
            o_ref[:, sl] = finish(acc[:, sl], c, s)
        return

    scr_ref = refs[-1]
    rows = tm // dil
    for hh in range(DIL_HEADS):
        scr_ref[hh] = acc[:, hh * HEAD_DIM:(hh + 1) * HEAD_DIM]
    for r in range(dil):
        take = pl.ds(r, rows, stride=dil)
        c, s = (cos_ref[take, :], sin_ref[take, :]) if rope else (None, None)
        for hh in range(DIL_HEADS):
            lo = r * DIL_COLS + hh * HEAD_DIM
            o_ref[:, lo:lo + HEAD_DIM] = finish(scr_ref[hh, take, :], c, s)


def _dil_proj(h, wt_dg, col_tile, dil, seq, cos_t=None, sin_t=None, scale=1.0):
    m, d = h.shape
    bsz = m // seq
    tm = _tile(seq, 1024)
    per_b = seq // tm
    rope = cos_t is not None
    in_specs = [pl.BlockSpec((tm, d), lambda i: (i, 0)),
                pl.BlockSpec((DIL_COLS, d), lambda i: (col_tile, 0))]
    args = [h, wt_dg]
    if rope:
        in_specs += [pl.BlockSpec((tm, LANE), lambda i: (i, 0))] * 2
        args += [cos_t, sin_t]
    scratch = [pltpu.VMEM((DIL_HEADS, tm, HEAD_DIM), F32)] if dil > 1 else []
    return pl.pallas_call(
        functools.partial(_dil_proj_kernel, rope=rope, scale=scale, dil=dil),
        grid=(m // tm,),
        in_specs=in_specs,
        out_specs=pl.BlockSpec((None, tm // dil, dil * DIL_COLS), lambda i: (i // per_b, i % per_b, 0)),
        out_shape=jax.ShapeDtypeStruct((bsz, seq // dil, dil * DIL_COLS), BF16),
        scratch_shapes=scratch,
        compiler_params=_cparams("parallel"),
        name=f"dil_proj_d{dil}",
    )(*args)


def _gate_proj_kernel(*refs, n_i, extents):
    ns = len(extents)
    h_ref, w_ref = refs[:2]
    side_in, o_ref, side_out = refs[2:2 + ns], refs[2 + ns], refs[3 + ns:]
    acc = _dot_nt(h_ref[...], w_ref[...])
    o_ref[...] = jax.nn.sigmoid(acc).astype(o_ref.dtype)
    step = pl.program_id(0) * n_i + pl.program_id(1)
    for w_in_ref, w_out_ref, ext in zip(side_in, side_out, extents):
        _side_cast(w_in_ref, w_out_ref, step, *ext)


def _gate_proj(h, wt_dg, col_tile0, n, side_weights):
    m, d = h.shape
    tm = _tile(m, 1024)
    tn = DIL_COLS
    n_j, n_i = n // tn, m // tm
    step_of = lambda j, i: j * n_i + i
    sides = [_side_cast_specs(w, w.shape[0], cp, n_j * n_i, step_of) for w, cp in side_weights]
    return pl.pallas_call(
        functools.partial(_gate_proj_kernel, n_i=n_i, extents=tuple(w.shape for w, _ in side_weights)),
        grid=(n_j, n_i),
        in_specs=[pl.BlockSpec((tm, d), lambda j, i: (i, 0)),
                  pl.BlockSpec((tn, d), lambda j, i: (col_tile0 + j, 0))] + [s[0] for s in sides],
        out_specs=[pl.BlockSpec((tm, tn), lambda j, i: (i, j))] + [s[1] for s in sides],
        out_shape=[jax.ShapeDtypeStruct((m, n), BF16)] + [s[2] for s in sides],
        compiler_params=_cparams("arbitrary", "arbitrary"),
        name="gate_proj",
    )(h, wt_dg, *[w for w, _ in side_weights])


def _qup_kernel(z_ref, g_ref, w_ref, cos_ref, sin_ref, o_ref, zn_ref):
    @pl.when(pl.program_id(1) == 0)
    def _():
        zn_ref[...] = _rms(z_ref[...], g_ref[...]).astype(BF16)

    acc = jnp.dot(zn_ref[...], w_ref[...], preferred_element_type=F32)
    c = cos_ref[...]
    s = sin_ref[...]
    for hh in range(acc.shape[1] // MLA_QK_PAD):
        lo = hh * MLA_QK_PAD
        o_ref[:, lo:lo + LANE] = (acc[:, lo:lo + LANE] * MLA_QSCALE).astype(o_ref.dtype)
        rp = _rope_mla(acc[:, lo + LANE:lo + 2 * LANE], c, s)
        o_ref[:, lo + LANE:lo + 2 * LANE] = (rp * MLA_QSCALE).astype(o_ref.dtype)


def _qup(z_s, g, w_uq_pad, cos_t, sin_t):
    m = z_s.shape[0]
    k = MLA_Q_RANK
    n = w_uq_pad.shape[1]
    tm = _tile(m, 1024)
    tn = _tile(n, 1024)
    return pl.pallas_call(
        _qup_kernel,
        grid=(m // tm, n // tn),
        in_specs=[pl.BlockSpec((tm, k), lambda i, j: (i, 0)),
                  pl.BlockSpec((1, k), lambda i, j: (0, 0)),
                  pl.BlockSpec((k, tn), lambda i, j: (0, j)),
                  pl.BlockSpec((tm, LANE), lambda i, j: (i, 1)),
                  pl.BlockSpec((tm, LANE), lambda i, j: (i, 1))],
        out_specs=pl.BlockSpec((tm, tn), lambda i, j: (i, j)),
        out_shape=jax.ShapeDtypeStruct((m, n), BF16),
        scratch_shapes=[pltpu.VMEM((tm, k), BF16)],
        compiler_params=_cparams("parallel", "arbitrary"),
        name="q_up",
    )(z_s, g.reshape(1, k), w_uq_pad, cos_t, sin_t)


def _kvup_kernel(z_ref, kr_ref, g_ref, wk_ref, wvt_ref, cos_ref, sin_ref, k_ref, vt_ref, zn_ref):
    @pl.when(pl.program_id(1) == 0)
    def _():
        zn_ref[...] = _rms(z_ref[...], g_ref[...]).astype(BF16)

    zn = zn_ref[...]
    k_nope = jnp.dot(zn, wk_ref[...], preferred_element_type=F32)
    vt_ref[...] = _dot_nt(wvt_ref[...], zn).astype(vt_ref.dtype)
    kr = kr_ref[...]
    kr = jnp.where(lax.broadcasted_iota(jnp.int32, kr.shape, 1) < MLA_ROPE, kr, 0.0)
    kr = _rope_mla(kr, cos_ref[...], sin_ref[...]).astype(k_ref.dtype)
    for hh in range(k_nope.shape[1] // MLA_NOPE):
        ko = hh * MLA_QK_PAD
        k_ref[:, ko:ko + LANE] = k_nope[:, hh * MLA_NOPE:(hh + 1) * MLA_NOPE].astype(k_ref.dtype)
        k_ref[:, ko + LANE:ko + 2 * LANE] = kr


def _kvup(z_s, g, w_k, w_vt, cos_t, sin_t, seq):
    m = z_s.shape[0]
    bsz = m // seq
    k = MLA_KV_RANK
    tm = _tile(seq, 1024)
    per_b = seq // tm
    nh = 4
    kv_blk = MLA_Q_RANK // MLA_KV_RANK
    kr_blk = (MLA_Q_RANK + MLA_KV_RANK) // LANE
    return pl.pallas_call(
        _kvup_kernel,
        grid=(m // tm, MLA_HEADS // nh),
        in_specs=[pl.BlockSpec((tm, k), lambda i, j: (i, kv_blk)),
                  pl.BlockSpec((tm, LANE), lambda i, j: (i, kr_blk)),
                  pl.BlockSpec((1, k), lambda i, j: (0, 0)),
                  pl.BlockSpec((k, nh * MLA_NOPE), lambda i, j: (0, j)),
                  pl.BlockSpec((nh * MLA_V, k), lambda i, j: (j, 0)),
                  pl.BlockSpec((tm, LANE), lambda i, j: (i, 1)),
                  pl.BlockSpec((tm, LANE), lambda i, j: (i, 1))],
        out_specs=[pl.BlockSpec((tm, nh * MLA_QK_PAD), lambda i, j: (i, j)),
                   pl.BlockSpec((None, nh * MLA_V, tm), lambda i, j: (i // per_b, j, i % per_b))],
        out_shape=[jax.ShapeDtypeStruct((m, MLA_HEADS * MLA_QK_PAD), BF16),
                   jax.ShapeDtypeStruct((bsz, MLA_HEADS * MLA_V, seq), BF16)],
        scratch_shapes=[pltpu.VMEM((tm, k), BF16)],
        compiler_params=_cparams("parallel", "arbitrary"),
        name="kv_up",
    )(z_s, z_s, g.reshape(1, k), w_k, w_vt, cos_t, sin_t)


def _mla_kernel(q_ref, k_ref, vt_ref, o_ref, *, sub):
    n = q_ref.shape[0] // sub
    s_ts = [_dot_nt(k_ref[...], q_ref[i * sub:(i + 1) * sub, :]) for i in range(n)]
    ms = [jnp.max(s_t, axis=0, keepdims=True) for s_t in s_ts]
    for i in range(n):
        p = jnp.exp2(s_ts[i] - ms[i])
        l = jnp.sum(p, axis=0, keepdims=True)
        o_t = jnp.dot(vt_ref[...], p.astype(BF16), preferred_element_type=F32)
        o_ref[i * sub:(i + 1) * sub, :] = (o_t / l).T.astype(o_ref.dtype)


def _mla_attention(q, k, v_t, bsz, seq):
    q3 = q.reshape(bsz, seq, MLA_HEADS * MLA_QK_PAD)
    k3 = k.reshape(bsz, seq, MLA_HEADS * MLA_QK_PAD)
    tq = _tile(seq, 2048)
    out = pl.pallas_call(
        functools.partial(_mla_kernel, sub=min(tq, 1024)),
        grid=(bsz, MLA_HEADS, seq // tq),
        in_specs=[pl.BlockSpec((None, tq, MLA_QK_PAD), lambda b, h, i: (b, i, h)),
                  pl.BlockSpec((None, seq, MLA_QK_PAD), lambda b, h, i: (b, 0, h)),
                  pl.BlockSpec((None, MLA_V, seq), lambda b, h, i: (b, h, 0))],
        out_specs=pl.BlockSpec((None, tq, MLA_V), lambda b, h, i: (b, i, h)),
        out_shape=jax.ShapeDtypeStruct((bsz, seq, MLA_HEADS * MLA_V), BF16),
        compiler_params=_cparams("parallel", "parallel", "arbitrary"),
        name="mla_attn",
    )(q3, k3, v_t)
    return out.reshape(bsz * seq, MLA_HEADS * MLA_V)


def _dil_key_row0(ti, *, t_len, tb, kb, n_side):
    return jnp.clip(ti * tb - n_side, 0, t_len - kb)


def _dil_kernel(q_ref, k_ref, v_ref, o_ref, st_ref, *, t_len, tb, kb, tq, kw, n_side):
    ti = pl.program_id(2)
    k_row0 = _dil_key_row0(ti, t_len=t_len, tb=tb, kb=kb, n_side=n_side)
    lane = lax.broadcasted_iota(jnp.int32, (tq, LANE), 1)
    row = lax.broadcasted_iota(jnp.int32, (tq, kw), 0)
    col = lax.broadcasted_iota(jnp.int32, (tq, kw), 1)

    def tile(qi, carry):
        t0 = pl.multiple_of(qi * tq, tq)
        pos0 = ti * tb + t0
        ws = jnp.clip(pos0 - n_side, 0, t_len - kw)
        ws_loc = pl.multiple_of(ws - k_row0, n_side) if t_len > kw else 0
        valid = jnp.abs((pos0 + row) - (ws + col)) <= n_side
        stats = jnp.zeros((tq, LANE), F32)
        for hh in range(DIL_HEADS):
            sl = slice(hh * HEAD_DIM, (hh + 1) * HEAD_DIM)
            q = q_ref[pl.ds(t0, tq), sl]
            k = k_ref[0, pl.ds(ws_loc, kw), sl]
            v = v_ref[0, pl.ds(ws_loc, kw), sl]
            s = lax.dot_general(q, k, (((1,), (1,)), ((), ())), preferred_element_type=F32)
            s = jnp.where(valid, s, -jnp.inf)
            m = jnp.max(s, axis=-1, keepdims=True)
            p = jnp.exp2(s - m)
            l = jnp.sum(p, axis=-1, keepdims=True)
            o = jnp.dot(p.astype(BF16), v, preferred_element_type=F32)
            o_ref[pl.ds(t0, tq), sl] = (o / l).astype(o_ref.dtype)
            stats = jnp.where(lane == hh, m, stats)
            stats = jnp.where(lane == DIL_HEADS + hh, l, stats)
        st_ref[pl.ds(t0, tq), :] = stats
        return carry

    n_tiles = tb // tq
    lax.fori_loop(0, n_tiles, tile, 0, unroll=min(4, n_tiles))


def _dil_rows_per_block(t_len):
    return t_len if t_len <= 2048 else 2048


def _dilated_group(q3, k3, v3, dil, window):
    bsz, t_len, _ = q3.shape
    n_side = window // (2 * dil)
    tq = min(2 * n_side, t_len)
    kw = min(4 * n_side, t_len)
    tb = _dil_rows_per_block(t_len)
    kb = min(tb + 2 * n_side, t_len)
    geom = dict(t_len=t_len, tb=tb, kb=kb, n_side=n_side)

    def kv_index(b, r, ti):
        row0 = _dil_key_row0(ti, **geom)
        return (b, pl.multiple_of(row0, n_side) if kb < t_len else 0, r * DIL_COLS)

    kv_spec = pl.BlockSpec((pl.Element(1), pl.Element(kb), pl.Element(DIL_COLS)), kv_index)
    row_spec = pl.BlockSpec((None, tb, DIL_COLS), lambda b, r, ti: (b, ti, r))

    o, st = pl.pallas_call(
        functools.partial(_dil_kernel, tq=tq, kw=kw, **geom),
        grid=(bsz, dil, t_len // tb),
        in_specs=[row_spec, kv_spec, kv_spec],
        out_specs=[row_spec, pl.BlockSpec((None, tb, LANE), lambda b, r, ti: (b, ti, r))],
        out_shape=[jax.ShapeDtypeStruct((bsz, t_len, dil * DIL_COLS), BF16),
                   jax.ShapeDtypeStruct((bsz, t_len, dil * LANE), F32)],
        compiler_params=_cparams("parallel", "parallel", "parallel"),
        name=f"dilated_d{dil}",
    )(q3, k3, v3)
    return o, st, DIL_HEADS


def _merge_kernel(*refs, hbs, dils):
    ng = len(hbs)
    o_refs, st_refs, y_ref = refs[:ng], refs[ng:2 * ng], refs[2 * ng]
    o_scr, st_scr = refs[2 * ng + 1:]
    tm = y_ref.shape[0]

    for g in range(ng):
        d = dils[g]
        if d == 1:
            continue
        rows = tm // d
        nblk = DIL_HEADS // hbs[g]
        for r in range(d):
            put = pl.ds(r, rows, stride=d)
            for hh in range(DIL_HEADS):
                lo = r * DIL_COLS + hh * HEAD_DIM
                o_scr[g, hh, put, :] = o_refs[g][:, lo:lo + HEAD_DIM].astype(F32)
            for blk in range(nblk):
                lo = (r * nblk + blk) * LANE
                st_scr[g, blk, put, :] = st_refs[g][:, lo:lo + LANE]

    def o_tile(g, h):
        if dils[g] == 1:
            return o_refs[g][:, h * HEAD_DIM:(h + 1) * HEAD_DIM].astype(F32)
        return o_scr[g, h]

    def st_tile(g):
        nblk = DIL_HEADS // hbs[g]
        if dils[g] == 1:
            tiles = [st_refs[g][:, blk * LANE:(blk + 1) * LANE] for blk in range(nblk)]
        else:
            tiles = [st_scr[g, blk] for blk in range(nblk)]
        return functools.reduce(lambda u, w: u + w, tiles)

    sts = [st_tile(g) for g in range(ng)]
    m_all = functools.reduce(jnp.maximum, sts)
    a = [jnp.exp2(sts[g] - m_all) * pltpu.roll(sts[g], LANE - DIL_HEADS, 1) for g in range(ng)]
    den = functools.reduce(lambda u, w: u + w, a)
    head_lane = lax.broadcasted_iota(jnp.int32, (tm, LANE), 1) < DIL_HEADS
    spread = (lax.broadcasted_iota(jnp.int32, (LANE, DIL_COLS), 0)
              == jnp.right_shift(lax.broadcasted_iota(jnp.int32, (LANE, DIL_COLS), 1),
                                 HEAD_DIM.bit_length() - 1)).astype(BF16)
    coef = []
    for g in range(ng):
        cg = jnp.where(head_lane, a[g] / den, 0.0)
        hi = cg.astype(BF16)
        lo = (cg - hi.astype(F32)).astype(BF16)
        coef.append(jnp.dot(hi, spread, preferred_element_type=F32)
                    + jnp.dot(lo, spread, preferred_element_type=F32))
    for h in range(DIL_HEADS):
        sl = slice(h * HEAD_DIM, (h + 1) * HEAD_DIM)
        y = None
        for g in range(ng):
            term = coef[g][:, sl] * o_tile(g, h)
            y = term if y is None else y + term
        y_ref[:, sl] = y.astype(y_ref.dtype)


def _merge_groups(os_, sts, hbs, dils, seq):
    bsz = os_[0].shape[0]
    m = bsz * seq
    tm = _tile(seq, 512)
    per_b = seq // tm

    def spec(arr, d):
        return pl.BlockSpec((None, tm // d, arr.shape[2]), lambda i: (i // per_b, i % per_b, 0))

    in_specs = [spec(o, d) for o, d in zip(os_, dils)] + [spec(s, d) for s, d in zip(sts, dils)]
    ng = len(os_)
    max_blk = max(DIL_HEADS // hb for hb in hbs)
    return pl.pallas_call(
        functools.partial(_merge_kernel, hbs=tuple(hbs), dils=tuple(dils)),
        grid=(m // tm,),
        in_specs=in_specs,
        out_specs=pl.BlockSpec((tm, DIL_COLS), lambda i: (i, 0)),
        out_shape=jax.ShapeDtypeStruct((m, DIL_COLS), BF16),
        scratch_shapes=[pltpu.VMEM((ng, DIL_HEADS, tm, HEAD_DIM), F32),
                        pltpu.VMEM((ng, max_blk, tm, LANE), F32)],
        compiler_params=_cparams("parallel"),
        name="dil_merge",
    )(*os_, *sts)


def _mix_kernel(ya_ref, yb_ref, wa_ref, wb_ref, ga_ref, gb_ref, o_ref):
    pa = jnp.dot(ya_ref[...], wa_ref[...], preferred_element_type=F32)
    pb = jnp.dot(yb_ref[...], wb_ref[...], preferred_element_type=F32)
    o_ref[...] = (ga_ref[...].astype(F32) * pa + gb_ref[...].astype(F32) * pb).astype(o_ref.dtype)


def _mix(ya, yb, w_pa, w_pb, gates):
    m = ya.shape[0]
    d = w_pa.shape[1]
    tm = _tile(m, 1024)
    tn = _tile(d, 1024)
    ga0 = 0
    gb0 = d // tn
    return pl.pallas_call(
        _mix_kernel,
        grid=(m // tm, d // tn),
        in_specs=[pl.BlockSpec((tm, ya.shape[1]), lambda i, j: (i, 0)),
                  pl.BlockSpec((tm, yb.shape[1]), lambda i, j: (i, 0)),
                  pl.BlockSpec((w_pa.shape[0], tn), lambda i, j: (0, j)),
                  pl.BlockSpec((w_pb.shape[0], tn), lambda i, j: (0, j)),
                  pl.BlockSpec((tm, tn), lambda i, j: (i, ga0 + j)),
                  pl.BlockSpec((tm, tn), lambda i, j: (i, gb0 + j))],
        out_specs=pl.BlockSpec((tm, tn), lambda i, j: (i, j)),
        out_shape=jax.ShapeDtypeStruct((m, d), BF16),
        compiler_params=_cparams("parallel", "arbitrary"),
        name="mix",
    )(ya, yb, w_pa, w_pb, gates, gates)


def _outproj_kernel(a_ref, w_ref, x_ref, g_ref, o_ref):
    acc = jnp.dot(a_ref[...], w_ref[...], preferred_element_type=F32)
    o_ref[...] = x_ref[...] + g_ref[...] * acc


def _outproj(a, w, x2d, gate, seq):
    m, k = a.shape
    d = w.shape[1]
    bsz = m // seq
    tm = _tile(seq, 1024)
    tn = _tile(d, 512)
    per_b = seq // tm
    return pl.pallas_call(
        _outproj_kernel,
        grid=(m // tm, d // tn),
        in_specs=[pl.BlockSpec((tm, k), lambda i, j: (i, 0)),
                  pl.BlockSpec((k, tn), lambda i, j: (0, j)),
                  pl.BlockSpec((tm, tn), lambda i, j: (i, j)),
                  pl.BlockSpec((None, 1, tn), lambda i, j: (i // per_b, 0, j))],
        out_specs=pl.BlockSpec((tm, tn), lambda i, j: (i, j)),
        out_shape=jax.ShapeDtypeStruct((m, d), F32),
        compiler_params=_cparams("parallel", "arbitrary"),
        name="out_proj",
    )(a, w, x2d, gate.reshape(bsz, 1, d))


def _ffn_up_kernel(h_ref, wg_ref, wu_ref, wd_ref, o_ref, od_ref, *, n_j, ext_d):
    h = h_ref[...]
    g = jnp.dot(h, wg_ref[...], preferred_element_type=F32)
    u = jnp.dot(h, wu_ref[...], preferred_element_type=F32)
    o_ref[...] = (g * jax.nn.sigmoid(g) * u).astype(o_ref.dtype)
    _side_cast(wd_ref, od_ref, pl.program_id(0) * n_j + pl.program_id(1), *ext_d)


def _ffn_up(h, wg, wu, w_down):
    m, d = h.shape
    f = wg.shape[1]
    tm = _tile(m, 1024)
    tn = _tile(f, 512)
    n_i, n_j = m // tm, f // tn
    side = _side_cast_specs(w_down, f, w_down.shape[1], n_i * n_j, lambda i, j: i * n_j + j)
    return pl.pallas_call(
        functools.partial(_ffn_up_kernel, n_j=n_j, ext_d=w_down.shape),
        grid=(n_i, n_j),
        in_specs=[pl.BlockSpec((tm, d), lambda i, j: (i, 0)),
                  pl.BlockSpec((d, tn), lambda i, j: (0, j)),
                  pl.BlockSpec((d, tn), lambda i, j: (0, j)),
                  side[0]],
        out_specs=[pl.BlockSpec((tm, tn), lambda i, j: (i, j)), side[1]],
        out_shape=[jax.ShapeDtypeStruct((m, f), BF16), side[2]],
        compiler_params=_cparams("arbitrary", "arbitrary"),
        name="ffn_up",
    )(h, wg, wu, w_down)


def _ffn_down_kernel(a_ref, w_ref, xc_ref, g_ref, fg_ref, o_ref, x_scr, *, nk, nx, final_norm):
    k = pl.program_id(1)
    cw = xc_ref.shape[1]

    for c in range(nx):
        @pl.when(k == c)
        def _(c=c):
            x_scr[:, c * cw:(c + 1) * cw] = xc_ref[...]

    @pl.when(k == 0)
    def _():
        o_ref[...] = jnp.dot(a_ref[...], w_ref[...], preferred_element_type=F32)

    @pl.when(k > 0)
    def _():
        o_ref[...] = jnp.dot(a_ref[...], w_ref[...], preferred_element_type=F32) + o_ref[...]

    @pl.when(k == nk - 1)
    def _():
        x2 = x_scr[...] + g_ref[...] * o_ref[...]
        o_ref[...] = _rms(x2, fg_ref[...]) if final_norm else x2


def _ffn_down(a, w, x2d, gate, final_g, seq, final_norm):
    m, f = a.shape
    d = w.shape[1]
    bsz = m // seq
    tm = _tile(seq, 512)
    tk = _tile(f, 1024)
    nk = f // tk
    per_b = seq // tm
    nx = max(n for n in (8, 4, 2, 1) if n <= nk and d % (n * LANE) == 0)
    return pl.pallas_call(
        functools.partial(_ffn_down_kernel, nk=nk, nx=nx, final_norm=final_norm),
        grid=(m // tm, nk),
        in_specs=[pl.BlockSpec((tm, tk), lambda i, k: (i, k)),
                  pl.BlockSpec((tk, d), lambda i, k: (k, 0)),
                  pl.BlockSpec((tm, d // nx), lambda i, k: (i, jnp.minimum(k, nx - 1))),
                  pl.BlockSpec((None, 1, d), lambda i, k: (i // per_b, 0, 0)),
                  pl.BlockSpec((1, d), lambda i, k: (0, 0))],
        out_specs=pl.BlockSpec((tm, d), lambda i, k: (i, 0)),
        out_shape=jax.ShapeDtypeStruct((m, d), F32),
        scratch_shapes=[pltpu.VMEM((tm, d), F32)],
        compiler_params=_cparams("parallel", "arbitrary"),
        name="ffn_down",
    )(a, w, x2d, gate.reshape(bsz, 1, d), final_g.reshape(1, d))


FF_MULT = 1024


def kernel(x, c, positions, w_ada, b_ada, norm1_g, w_in, q_norm_g, w_uq, kv_norm_g, w_ukv,
           w_proj_a, w_proj_b, w_out, norm2_g, w_gate, w_up, w_down, final_g):
    bsz, seq, d = x.shape
    depth = w_ada.shape[0]
    m = bsz * seq
    cos_t, sin_t = _rope_tables(positions)
    x2d = x.reshape(m, d)

    for l in range(depth):
        mod = _ada(c, w_ada[l], b_ada[l])
        sh1, sc1, g1, sh2, sc2, g2 = jnp.split(mod, N_MOD, axis=-1)

        wt = w_in[l].T
        h = _prenorm(x2d, norm1_g[l], sh1, sc1, seq)
        z_s, w_dg = _in_small(h, wt[:N_SMALL_PAD].astype(BF16), wt)
        f_pad = -(-w_gate.shape[2] // FF_MULT) * FF_MULT
        gates, wg, wu, w_o, w_pa = _gate_proj(
            h, w_dg, 3 * DIL_GROUPS, 2 * d,
            [(w_gate[l], f_pad), (w_up[l], f_pad), (w_out[l], d), (w_proj_a[l], d)])

        w_uq_pad = jnp.pad(w_uq[l].reshape(MLA_Q_RANK, MLA_HEADS, MLA_NOPE + MLA_ROPE),
                           ((0, 0), (0, 0), (0, MLA_QK_PAD - MLA_NOPE - MLA_ROPE)))
        w_uq_pad = w_uq_pad.reshape(MLA_Q_RANK, MLA_HEADS * MLA_QK_PAD).astype(BF16)
        q = _qup(z_s, q_norm_g[l], w_uq_pad, cos_t, sin_t)
        w_kv = w_ukv[l].reshape(MLA_KV_RANK, MLA_HEADS, MLA_NOPE + MLA_V)
        w_k = w_kv[:, :, :MLA_NOPE].reshape(MLA_KV_RANK, MLA_HEADS * MLA_NOPE).astype(BF16)
        w_vt = w_kv[:, :, MLA_NOPE:].reshape(MLA_KV_RANK, MLA_HEADS * MLA_V).T.astype(BF16)
        k, v_t = _kvup(z_s, kv_norm_g[l], w_k, w_vt, cos_t, sin_t, seq)
        y_a = _mla_attention(q, k, v_t, bsz, seq)

        os_, sts, hbs, dils = [], [], [], []
        for g, (window, dil) in enumerate(DIL_PATTERNS):
            q3 = _dil_proj(h, w_dg, g, dil, seq, cos_t, sin_t, DIL_QSCALE)
            k3 = _dil_proj(h, w_dg, DIL_GROUPS + g, dil, seq, cos_t, sin_t)
            v3 = _dil_proj(h, w_dg, 2 * DIL_GROUPS + g, dil, seq)
            o_g, st_g, hb = _dilated_group(q3, k3, v3, dil, window)
            os_.append(o_g)
            sts.append(st_g)
            hbs.append(hb)
            dils.append(dil)
        y_b = _merge_groups(os_, sts, hbs, dils, seq)

        mixed = _mix(y_a, y_b, w_pa, w_proj_b[l].astype(BF16), gates)
        x2d = _outproj(mixed, w_o, x2d, g1, seq)

        h2 = _prenorm(x2d, norm2_g[l], sh2, sc2, seq)
        hmid, wd = _ffn_up(h2, wg, wu, w_down[l])
        x2d = _ffn_down(hmid, wd, x2d, g2, final_g, seq, final_norm=(l == depth - 1))

    return x2d.reshape(bsz, seq, d)
```
